```python
import math
import jax, jax.numpy as jnp
from jax import lax
import numpy as np

D_MODEL = 1024
BATCH = 4
SEQ = 8192
DEPTH = 2

CHUNK = 64
SSD_HEADS = 8
SSD_HEAD_DIM = 64
SSD_WIDTH = SSD_HEADS * SSD_HEAD_DIM
SSD_GROUPS = 2
SSD_STATE = 128
CONV_WIDTH = 4
XBC_WIDTH = SSD_WIDTH + 2 * SSD_GROUPS * SSD_STATE
HGRN_HEADS = 4
HGRN_HEAD_DIM = 128
HGRN_WIDTH = HGRN_HEADS * HGRN_HEAD_DIM
HGRN_BLOCK = 32
MIX_WIDTH = SSD_WIDTH + HGRN_WIDTH
IN_WIDTH = SSD_WIDTH + XBC_WIDTH + SSD_HEADS + 4 * HGRN_WIDTH
N_EXPERT_GROUPS = 4
EXPERTS_PER_GROUP = 4
N_EXPERTS = N_EXPERT_GROUPS * EXPERTS_PER_GROUP
TOP_K = 2
EXPERT_DIM = 256
EPS = 1e-6

kernel_name = "hymba_ssd_hgrn2_hmoe_trunk"


def rmsnorm(x, w):
    xf = x.astype(jnp.float32)
    y = xf * lax.rsqrt(jnp.mean(xf * xf, axis=-1, keepdims=True) + EPS)
    return (y * w.astype(jnp.float32)).astype(x.dtype)


def causal_depthwise_conv(u, w, b):
    s = u.shape[1]
    up = jnp.pad(u, ((0, 0), (CONV_WIDTH - 1, 0), (0, 0)))
    out = b
    for k in range(CONV_WIDTH):
        out = out + w[k] * up[:, k:k + s]
    return out


def ssd_mixer(z, xbc, dt_raw, conv_w, conv_b, dt_bias, a_log, d_skip, norm_w):
    f32 = jnp.float32
    bsz, s, _ = z.shape
    nc = s // CHUNK
    hpg = SSD_HEADS // SSD_GROUPS
    xbc = jax.nn.silu(causal_depthwise_conv(xbc, conv_w, conv_b))
    xs, bm, cm = jnp.split(xbc, [SSD_WIDTH, SSD_WIDTH + SSD_GROUPS * SSD_STATE], axis=-1)
    xs = xs.astype(f32).reshape(bsz, nc, CHUNK, SSD_GROUPS, hpg, SSD_HEAD_DIM)
    bm = bm.astype(f32).reshape(bsz, nc, CHUNK, SSD_GROUPS, SSD_STATE)
    cm = cm.astype(f32).reshape(bsz, nc, CHUNK, SSD_GROUPS, SSD_STATE)
    dt = jax.nn.softplus(dt_raw.astype(f32) + dt_bias.astype(f32)).reshape(bsz, nc, CHUNK, SSD_GROUPS, hpg)
    a = -jnp.exp(a_log.astype(f32)).reshape(SSD_GROUPS, hpg)
    cum = jnp.cumsum(dt * a, axis=2)
    causal = jnp.tril(jnp.ones((CHUNK, CHUNK), dtype=bool))[None, None, :, :, None, None]
    seg = cum[:, :, :, None] - cum[:, :, None]
    decay = jnp.exp(jnp.where(causal, seg, -jnp.inf))
    cb = jnp.einsum('bctgn,bcsgn->bctsg', cm, bm)
    scores = cb[..., None] * decay * dt[:, :, None]
    y_intra = jnp.einsum('bctsgh,bcsghp->bctghp', scores, xs)
    decay_end = jnp.exp(cum[:, :, -1:] - cum) * dt
    states = jnp.einsum('bcsgn,bcsghp->bcghpn', bm, xs * decay_end[..., None])
    chunk_decay = jnp.exp(cum[:, :, -1])

    def step(h, inp):
        dec, st = inp
        return dec[..., None, None] * h + st, h

    h0 = jnp.zeros((bsz, SSD_GROUPS, hpg, SSD_HEAD_DIM, SSD_STATE), f32)
    _, h_prev = lax.scan(step, h0, (jnp.moveaxis(chunk_decay, 1, 0), jnp.moveaxis(states, 1, 0)))
    h_prev = jnp.moveaxis(h_prev, 0, 1)
    y_inter = jnp.einsum('bctgn,bcghpn->bctghp', cm, h_prev) * jnp.exp(cum)[..., None]
    y = y_intra + y_inter + d_skip.astype(f32).reshape(SSD_GROUPS, hpg)[..., None] * xs
    y = y.reshape(bsz, s, SSD_WIDTH) * jax.nn.silu(z.astype(f32))
    y = y.reshape(bsz, s, SSD_GROUPS, SSD_WIDTH // SSD_GROUPS)
    y = rmsnorm(y, norm_w.reshape(SSD_GROUPS, SSD_WIDTH // SSD_GROUPS))
    return y.reshape(bsz, s, SSD_WIDTH).astype(z.dtype)


def hgrn2_mixer(q, f_logit, i_in, g, lb, norm_w):
    f32 = jnp.float32
    bsz, s, _ = q.shape
    nb = s // HGRN_BLOCK
    shp = (bsz, nb, HGRN_BLOCK, HGRN_HEADS, HGRN_HEAD_DIM)
    forget = lb.astype(f32) + (1.0 - lb.astype(f32)) * jax.nn.sigmoid(f_logit.astype(f32))
    log_f = jnp.log(forget).reshape(shp)
    k = (1.0 - forget).reshape(shp)
    qf = q.astype(f32).reshape(shp)
    v = i_in.astype(f32).reshape(shp)
    b = jnp.cumsum(log_f, axis=2)
    b_ref = b[:, :, HGRN_BLOCK // 2:HGRN_BLOCK // 2 + 1]
    qs = qf * jnp.exp(b - b_ref)
    ks = k * jnp.exp(b_ref - b)
    causal = jnp.tril(jnp.ones((HGRN_BLOCK, HGRN_BLOCK), dtype=bool))
    scores = jnp.where(causal, jnp.einsum('bnthk,bnshk->bnhts', qs, ks), 0.0)
    o_intra = jnp.einsum('bnhts,bnshv->bnthv', scores, v)
    q_dec = qf * jnp.exp(b)
    k_end = k * jnp.exp(b[:, :, -1:] - b)
    blk_decay = jnp.exp(b[:, :, -1])

    def step(state, inp):
        qd, ke, vv, dec = inp
        o = jnp.einsum('bthk,bhkv->bthv', qd, state)
        state = dec[..., None] * state + jnp.einsum('bshk,bshv->bhkv', ke, vv)
        return state, o

    s0 = jnp.zeros((bsz, HGRN_HEADS, HGRN_HEAD_DIM, HGRN_HEAD_DIM), f32)
    _, o_inter = lax.scan(step, s0, (jnp.moveaxis(q_dec, 1, 0), jnp.moveaxis(k_end, 1, 0),
                                     jnp.moveaxis(v, 1, 0), jnp.moveaxis(blk_decay, 1, 0)))
    o = (o_intra + jnp.moveaxis(o_inter, 0, 1)).reshape(bsz, s, HGRN_HEADS, HGRN_HEAD_DIM)
    o = rmsnorm(o, norm_w.reshape(HGRN_HEADS, HGRN_HEAD_DIM)).reshape(bsz, s, HGRN_WIDTH)
    return (o * jax.nn.silu(g.astype(f32))).astype(q.dtype)


def hier_moe(h, router_group, router_expert, w_gate, w_up, w_down):
    f32 = jnp.float32
    bsz, s, d = h.shape
    t = h.reshape(-1, d)
    p_group = jax.nn.softmax((t @ router_group).astype(f32), axis=-1)
    g_idx = jnp.argmax(p_group, axis=-1)
    g_w = jnp.max(p_group, axis=-1)
    logits = (t @ router_expert).astype(f32).reshape(-1, N_EXPERT_GROUPS, EXPERTS_PER_GROUP)
    in_group = jnp.take_along_axis(logits, g_idx[:, None, None], axis=1)[:, 0]
    top_w, top_i = lax.top_k(jax.nn.softmax(in_group, axis=-1), TOP_K)
    top_w = top_w / jnp.sum(top_w, axis=-1, keepdims=True)
    expert_id = g_idx[:, None] * EXPERTS_PER_GROUP + top_i
    combine = jnp.sum(jax.nn.one_hot(expert_id, N_EXPERTS, dtype=f32)
                      * (g_w[:, None] * top_w)[..., None], axis=1)

    def expert_step(acc, inp):
        wg, wu, wd, c = inp
        hid = jax.nn.silu(t @ wg) * (t @ wu)
        return acc + c[:, None].astype(t.dtype) * (hid @ wd), None

    y, _ = lax.scan(expert_step, jnp.zeros_like(t), (w_gate, w_up, w_down, combine.T))
    return y.reshape(bsz, s, d)


def setup_inputs(seed: int = 0) -> dict:
    key = jax.random.key(seed)
    ks = jax.random.split(key, 24)
    f32 = jnp.float32
    nrm = lambda k, shp, sc: jax.random.normal(k, shp, f32) * sc
    dt0 = jnp.exp(jax.random.uniform(ks[5], (DEPTH, SSD_HEADS), f32, math.log(1e-3), math.log(1e-1)))
    return {
        "x": nrm(ks[0], (BATCH, SEQ, D_MODEL), 1.0),
        "norm_mix_w": 1.0 + nrm(ks[1], (DEPTH, D_MODEL), 0.02),
        "w_in": nrm(ks[2], (DEPTH, D_MODEL, IN_WIDTH), D_MODEL ** -0.5),
        "conv_w": nrm(ks[3], (DEPTH, CONV_WIDTH, XBC_WIDTH), CONV_WIDTH ** -0.5),
        "conv_b": nrm(ks[4], (DEPTH, XBC_WIDTH), 0.02),
        "dt_bias": dt0 + jnp.log(-jnp.expm1(-dt0)),
        "a_log": jnp.log(jax.random.uniform(ks[6], (DEPTH, SSD_HEADS), f32, 1.0, 16.0)),
        "d_skip": 1.0 + nrm(ks[7], (DEPTH, SSD_HEADS), 0.1),
        "ssd_norm_w": 1.0 + nrm(ks[8], (DEPTH, SSD_WIDTH), 0.02),
        "hgrn_lb_logits": nrm(ks[9], (DEPTH, HGRN_WIDTH), 0.1),
        "hgrn_norm_w": 1.0 + nrm(ks[10], (DEPTH, HGRN_WIDTH), 0.02),
        "w_out": nrm(ks[11], (DEPTH, MIX_WIDTH, D_MODEL), MIX_WIDTH ** -0.5),
        "norm_ffn_w": 1.0 + nrm(ks[12], (DEPTH, D_MODEL), 0.02),
        "router_group": nrm(ks[13], (DEPTH, D_MODEL, N_EXPERT_GROUPS), D_MODEL ** -0.5),
        "router_expert": nrm(ks[14], (DEPTH, D_MODEL, N_EXPERTS), D_MODEL ** -0.5),
        "w_gate": nrm(ks[15], (DEPTH, N_EXPERTS, D_MODEL, EXPERT_DIM), D_MODEL ** -0.5),
        "w_up": nrm(ks[16], (DEPTH, N_EXPERTS, D_MODEL, EXPERT_DIM), D_MODEL ** -0.5),
        "w_down": nrm(ks[17], (DEPTH, N_EXPERTS, EXPERT_DIM, D_MODEL), EXPERT_DIM ** -0.5),
        "final_norm_w": 1.0 + nrm(ks[18], (D_MODEL,), 0.02),
    }


def reference(x, norm_mix_w, w_in, conv_w, conv_b, dt_bias, a_log, d_skip, ssd_norm_w,
              hgrn_lb_logits, hgrn_norm_w, w_out, norm_ffn_w, router_group, router_expert,
              w_gate, w_up, w_down, final_norm_w):
    p = jax.nn.softmax(hgrn_lb_logits.astype(jnp.float32), axis=0)
    lower_bounds = jnp.cumsum(p, axis=0) - p[0:1]
    splits = [SSD_WIDTH, SSD_WIDTH + XBC_WIDTH, SSD_WIDTH + XBC_WIDTH + SSD_HEADS]
    splits = splits + [splits[-1] + HGRN_WIDTH * j for j in (1, 2, 3)]
    for l in range(DEPTH):
        h = rmsnorm(x, norm_mix_w[l])
        proj = h @ w_in[l]
        z, xbc, dt_raw, hq, hf, hi, hg = jnp.split(proj, splits, axis=-1)
        y_ssd = ssd_mixer(z, xbc, dt_raw, conv_w[l], conv_b[l], dt_bias[l], a_log[l],
                          d_skip[l], ssd_norm_w[l])
        y_hgrn = hgrn2_mixer(hq, hf, hi, hg, lower_bounds[l], hgrn_norm_w[l])
        x = x + jnp.concatenate([y_ssd, y_hgrn], axis=-1) @ w_out[l]
        h = rmsnorm(x, norm_ffn_w[l])
        x = x + hier_moe(h, router_group[l], router_expert[l], w_gate[l], w_up[l], w_down[l])
    return rmsnorm(x, final_norm_w)
```

```python
import functools

import jax
import jax.numpy as jnp
from jax import lax
from jax.experimental import pallas as pl
from jax.experimental.pallas import tpu as pltpu

F32 = jnp.float32
BF16 = jnp.bfloat16

D_MODEL = 1024
SSD_HEADS = 8
SSD_HEAD_DIM = 64
SSD_WIDTH = SSD_HEADS * SSD_HEAD_DIM
SSD_GROUPS = 2
SSD_STATE = 128
CONV_WIDTH = 4
XBC_WIDTH = SSD_WIDTH + 2 * SSD_GROUPS * SSD_STATE
HGRN_HEADS = 4
HGRN_HEAD_DIM = 128
HGRN_WIDTH = HGRN_HEADS * HGRN_HEAD_DIM
HGRN_BLOCK = 32
N_EXPERT_GROUPS = 4
EXPERTS_PER_GROUP = 4
N_EXPERTS = N_EXPERT_GROUPS * EXPERTS_PER_GROUP
EXPERT_DIM = 256
EPS = 1e-6

LANES = 128
N_PAIRS = 6
N_BUCKETS = N_EXPERT_GROUPS * N_PAIRS
ROUTE_LANES = LANES
XE_WIDTH = D_MODEL + ROUTE_LANES
NEG_BIG = -1e30

TM_PROJ = 512
TS_MIX = 256
SSD_SUB = 128
TM_FFN = 256
VMEM_LIMIT = 56 * 1024 * 1024


def _mm(a, b):
    return jnp.dot(a.astype(BF16), b.astype(BF16), preferred_element_type=F32)


def _mm_nt(a, b):
    return lax.dot_general(a.astype(BF16), b.astype(BF16), (((1,), (1,)), ((), ())),
                           preferred_element_type=F32)


def _mm_tn(a, b):
    return lax.dot_general(a.astype(BF16), b.astype(BF16), (((0,), (0,)), ((), ())),
                           preferred_element_type=F32)


def _split3(x):
    p1 = x.astype(BF16)
    r1 = x - p1.astype(F32)
    p2 = r1.astype(BF16)
    p3 = (r1 - p2.astype(F32)).astype(BF16)
    return p1, p2, p3


def _cumsum_mm(tri, x):
    p1, p2, p3 = _split3(x)
    acc = jnp.dot(tri, p1, preferred_element_type=F32)
    acc = acc + jnp.dot(tri, p2, preferred_element_type=F32)
    return acc + jnp.dot(tri, p3, preferred_element_type=F32)


def _sigmoid(x):
    return 1.0 / (1.0 + jnp.exp(-x))


def _silu(x):
    return x * _sigmoid(x)


def _rms(x, w):
    return x * lax.rsqrt(jnp.mean(x * x, axis=-1, keepdims=True) + EPS) * w


def _inproj_kernel(x_ref, nw_ref, wz_ref, wxbc_ref, wq_ref, wf_ref, wi_ref, wg_ref, wdt_ref,
                   z_ref, xbc_ref, q_ref, f_ref, i_ref, g_ref, dt_ref):
    h = _rms(x_ref[...], nw_ref[...]).astype(BF16)
    for w_ref, o_ref in ((wz_ref, z_ref), (wxbc_ref, xbc_ref), (wq_ref, q_ref),
                         (wf_ref, f_ref), (wi_ref, i_ref), (wg_ref, g_ref), (wdt_ref, dt_ref)):
        o_ref[...] = jnp.dot(h, w_ref[...], preferred_element_type=F32).astype(o_ref.dtype)


def _inproj(x, n_tok, nw, wz, wxbc, wq, wf, wi, wg, wdt):
    tm = TM_PROJ
    row = lambda i: (i, 0)
    fixed = lambda i: (0, 0)
    widths = (SSD_WIDTH, XBC_WIDTH, HGRN_WIDTH, HGRN_WIDTH, HGRN_WIDTH, HGRN_WIDTH, LANES)
    dtypes = (BF16,) * 6 + (F32,)
    return pl.pallas_call(
        _inproj_kernel,
        grid=(n_tok // tm,),
        in_specs=[pl.BlockSpec((tm, D_MODEL), row), pl.BlockSpec((1, D_MODEL), fixed)]
        + [pl.BlockSpec((D_MODEL, w), fixed) for w in widths],
        out_specs=[pl.BlockSpec((tm, w), row) for w in widths],
        out_shape=[jax.ShapeDtypeStruct((n_tok, w), dt) for w, dt in zip(widths, dtypes)],
        compiler_params=pltpu.CompilerParams(dimension_semantics=("arbitrary",),
                                             vmem_limit_bytes=VMEM_LIMIT),
        name="inproj",
    )(x, nw, wz, wxbc, wq, wf, wi, wg, wdt)


def _lane_pair(col_even, col_odd, rows):
    lane = lax.broadcasted_iota(jnp.int32, (rows, LANES), 1)
    return jnp.where(lane < SSD_HEAD_DIM, jnp.broadcast_to(col_even, (rows, LANES)),
                     jnp.broadcast_to(col_odd, (rows, LANES)))


def _ssd_kernel(z_ref, xbc_ref, dt_ref, cw_ref, cb_ref, dtb_ref, alog_ref, dsk_ref, nw_ref,
                y_ref, ext_ref, h_ref):
    ts = z_ref.shape[0]
    L = SSD_SUB
    hpg = SSD_HEADS // SSD_GROUPS
    gw = SSD_WIDTH // SSD_GROUPS

    @pl.when(pl.program_id(1) == 0)
    def _():
        ext_ref[0:8, :] = jnp.zeros((8, XBC_WIDTH), F32)
        h_ref[...] = jnp.zeros_like(h_ref)

    ext_ref[8:8 + ts, :] = xbc_ref[...].astype(F32)

    r_i = lax.broadcasted_iota(jnp.int32, (L, L), 0)
    c_i = lax.broadcasted_iota(jnp.int32, (L, L), 1)
    causal = r_i >= c_i
    tri = causal.astype(BF16)
    lane = lax.broadcasted_iota(jnp.int32, (L, LANES), 1)
    even = lane < SSD_HEAD_DIM
    a_neg = -jnp.exp(alog_ref[...])

    for c in range(ts // L):
        r0 = c * L
        conv = cb_ref[...]
        for k in range(CONV_WIDTH):
            conv = conv + cw_ref[k:k + 1, :] * ext_ref[r0 + 5 + k:r0 + 5 + k + L, :]
        conv = _silu(conv)
        xs = conv[:, :SSD_WIDTH]
        bm = conv[:, SSD_WIDTH:SSD_WIDTH + SSD_GROUPS * SSD_STATE]
        cm = conv[:, SSD_WIDTH + SSD_GROUPS * SSD_STATE:]

        dt_raw = dt_ref[r0:r0 + L, :] + dtb_ref[...]
        dt = jnp.maximum(dt_raw, 0.0) + jnp.log(1.0 + jnp.exp(-jnp.abs(dt_raw)))
        cum = _cumsum_mm(tri, dt * a_neg)
        cum_t = cum.T
        dt_t = dt.T
        cum_last = cum[L - 1:L, :]
        dec_end = jnp.exp(cum_last - cum) * dt
        e_cum = jnp.exp(cum)
        e_last = jnp.exp(cum_last)

        zc = z_ref[r0:r0 + L, :].astype(F32)
        for g in range(SSD_GROUPS):
            bm_g = bm[:, g * SSD_STATE:(g + 1) * SSD_STATE].astype(BF16)
            cm_g = cm[:, g * SSD_STATE:(g + 1) * SSD_STATE].astype(BF16)
            cb = _mm_nt(cm_g, bm_g)
            h_t = h_ref[g]
            y_in = _mm(cm_g, h_t)
            y_parts, xsc_parts, el_parts = [], [], []
            for p in range(hpg // 2):
                h0 = g * hpg + 2 * p
                xp = xs[:, h0 * SSD_HEAD_DIM:(h0 + 2) * SSD_HEAD_DIM]
                xp_b = xp.astype(BF16)
                y_heads = []
                for hh in (h0, h0 + 1):
                    seg = cum[:, hh:hh + 1] - cum_t[hh:hh + 1, :]
                    sc = cb * jnp.exp(jnp.where(causal, seg, NEG_BIG)) * dt_t[hh:hh + 1, :]
                    y_heads.append(jnp.dot(sc.astype(BF16), xp_b, preferred_element_type=F32))
                y_intra = jnp.where(even, y_heads[0], y_heads[1])
                e_pair = _lane_pair(e_cum[:, h0:h0 + 1], e_cum[:, h0 + 1:h0 + 2], L)
                d_pair = _lane_pair(dec_end[:, h0:h0 + 1], dec_end[:, h0 + 1:h0 + 2], L)
                y_parts.append(y_intra + y_in[:, 2 * p * SSD_HEAD_DIM:(2 * p + 2) * SSD_HEAD_DIM] * e_pair
                               + dsk_ref[:, h0 * SSD_HEAD_DIM:(h0 + 2) * SSD_HEAD_DIM] * xp)
                xsc_parts.append(xp * d_pair)
                el_parts.append(_lane_pair(e_last[:, h0:h0 + 1], e_last[:, h0 + 1:h0 + 2], 1))
            xsc = jnp.concatenate(xsc_parts, axis=1)
            el = jnp.concatenate(el_parts, axis=1)
            h_ref[g] = h_t * el + _mm_tn(bm_g, xsc)
            yg = jnp.concatenate(y_parts, axis=1) * _silu(zc[:, g * gw:(g + 1) * gw])
            yg = _rms(yg, nw_ref[:, g * gw:(g + 1) * gw])
            y_ref[r0:r0 + L, g * gw:(g + 1) * gw] = yg.astype(y_ref.dtype)

    ext_ref[0:8, :] = ext_ref[ts:ts + 8, :]


def _ssd(z, xbc, dt, bsz, seq, cw, cb, dtb, alog, dsk, nw):
    ts = TS_MIX
    nst = seq // ts
    row = lambda b, s: (b * nst + s, 0)
    fixed = lambda b, s: (0, 0)
    return pl.pallas_call(
        _ssd_kernel,
        grid=(bsz, nst),
        in_specs=[pl.BlockSpec((ts, SSD_WIDTH), row), pl.BlockSpec((ts, XBC_WIDTH), row),
                  pl.BlockSpec((ts, LANES), row),
                  pl.BlockSpec((CONV_WIDTH, XBC_WIDTH), fixed), pl.BlockSpec((1, XBC_WIDTH), fixed),
                  pl.BlockSpec((1, LANES), fixed), pl.BlockSpec((1, LANES), fixed),
                  pl.BlockSpec((1, SSD_WIDTH), fixed), pl.BlockSpec((1, SSD_WIDTH), fixed)],
        out_specs=pl.BlockSpec((ts, SSD_WIDTH), row),
        out_shape=jax.ShapeDtypeStruct((bsz * seq, SSD_WIDTH), BF16),
        scratch_shapes=[pltpu.VMEM((ts + 8, XBC_WIDTH), F32),
                        pltpu.VMEM((SSD_GROUPS, SSD_STATE, SSD_WIDTH // SSD_GROUPS), F32)],
        compiler_params=pltpu.CompilerParams(dimension_semantics=("arbitrary", "arbitrary"),
                                             vmem_limit_bytes=VMEM_LIMIT),
        name="ssd",
    )(z, xbc, dt, cw, cb, dtb, alog, dsk, nw)


def _hgrn_kernel(q_ref, f_ref, i_ref, g_ref, lb_ref, nw_ref, o_ref, st_ref):
    ts = q_ref.shape[0]
    blk = HGRN_BLOCK
    nb = ts // blk
    hd = HGRN_HEAD_DIM

    @pl.when(pl.program_id(1) == 0)
    def _():
        st_ref[...] = jnp.zeros_like(st_ref)

    r_i = lax.broadcasted_iota(jnp.int32, (ts, ts), 0)
    c_i = lax.broadcasted_iota(jnp.int32, (ts, ts), 1)
    blk_causal = (r_i >= c_i) & ((r_i // blk) == (c_i // blk))
    tri = blk_causal.astype(BF16)

    for h in range(HGRN_HEADS):
        sl = slice(h * hd, (h + 1) * hd)
        lb = lb_ref[:, sl]
        forget = lb + (1.0 - lb) * _sigmoid(f_ref[:, sl].astype(F32))
        kk = 1.0 - forget
        cum = _cumsum_mm(tri, jnp.log(forget))
        cum3 = cum.reshape(nb, blk, hd)
        b_mid = cum3[:, blk // 2:blk // 2 + 1, :]
        b_end = cum3[:, blk - 1:blk, :]
        rel = (cum3 - b_mid).reshape(ts, hd)
        to_end = (b_end - cum3).reshape(ts, hd)
        q = q_ref[:, sl].astype(F32)
        v = i_ref[:, sl].astype(BF16)
        qs = q * jnp.exp(rel)
        ks = kk * jnp.exp(-rel)
        sc = jnp.where(blk_causal, _mm_nt(qs, ks), 0.0)
        o_intra = jnp.dot(sc.astype(BF16), v, preferred_element_type=F32)
        q_dec = (q * jnp.exp(cum)).astype(BF16)
        k_end = (kk * jnp.exp(to_end)).astype(BF16)
        dec = jnp.exp(b_end)
        st = st_ref[h]
        o_parts = []
        for n in range(nb):
            rs = slice(n * blk, (n + 1) * blk)
            o_parts.append(_mm_nt(q_dec[rs], st))
            st = st * dec[n] + _mm_tn(v[rs], k_end[rs])
        st_ref[h] = st
        o = o_intra + jnp.concatenate(o_parts, axis=0)
        o = _rms(o, nw_ref[:, sl]) * _silu(g_ref[:, sl].astype(F32))
        o_ref[:, sl] = o.astype(o_ref.dtype)


def _hgrn(q, f, i, g, bsz, seq, lb, nw):
    ts = TS_MIX
    nst = seq // ts
    row = lambda b, s: (b * nst + s, 0)
    fixed = lambda b, s: (0, 0)
    return pl.pallas_call(
        _hgrn_kernel,
        grid=(bsz, nst),
        in_specs=[pl.BlockSpec((ts, HGRN_WIDTH), row)] * 4
        + [pl.BlockSpec((1, HGRN_WIDTH), fixed)] * 2,
        out_specs=pl.BlockSpec((ts, HGRN_WIDTH), row),
        out_shape=jax.ShapeDtypeStruct((bsz * seq, HGRN_WIDTH), BF16),
        scratch_shapes=[pltpu.VMEM((HGRN_HEADS, HGRN_HEAD_DIM, HGRN_HEAD_DIM), F32)],
        compiler_params=pltpu.CompilerParams(dimension_semantics=("arbitrary", "arbitrary"),
                                             vmem_limit_bytes=VMEM_LIMIT),
        name="hgrn",
    )(q, f, i, g, lb, nw)


def _first_argmax4(v):
    m = jnp.maximum(jnp.maximum(v[0], v[1]), jnp.maximum(v[2], v[3]))
    idx = jnp.where(v[0] == m, 0.0, jnp.where(v[1] == m, 1.0, jnp.where(v[2] == m, 2.0, 3.0)))
    return m, idx


def _outproj_kernel(ys_ref, yh_ref, x_ref, wo1_ref, wo2_ref, nw_ref, wr_ref, xe_ref, rt_ref):
    tm = x_ref.shape[0]
    xm = (x_ref[...] + jnp.dot(ys_ref[...], wo1_ref[...], preferred_element_type=F32)
          + jnp.dot(yh_ref[...], wo2_ref[...], preferred_element_type=F32))
    h = _rms(xm, nw_ref[...])
    lt = _mm_nt(wr_ref[...], h)
    row = lambda r: lt[r:r + 1, :]
    gl = [row(j) for j in range(N_EXPERT_GROUPS)]
    g_max, g_idx = _first_argmax4(gl)
    g_w = 1.0 / (jnp.exp(gl[0] - g_max) + jnp.exp(gl[1] - g_max)
                 + jnp.exp(gl[2] - g_max) + jnp.exp(gl[3] - g_max))
    el = []
    for j in range(EXPERTS_PER_GROUP):
        base = N_EXPERT_GROUPS + j
        el.append(jnp.where(g_idx == 0.0, row(base),
                  jnp.where(g_idx == 1.0, row(base + EXPERTS_PER_GROUP),
                  jnp.where(g_idx == 2.0, row(base + 2 * EXPERTS_PER_GROUP),
                            row(base + 3 * EXPERTS_PER_GROUP)))))
    e1, i1 = _first_argmax4(el)
    el2 = [jnp.where(i1 == float(j), NEG_BIG, el[j]) for j in range(EXPERTS_PER_GROUP)]
    e2, i2 = _first_argmax4(el2)
    w1 = 1.0 / (1.0 + jnp.exp(e2 - e1))
    w2 = jnp.exp(e2 - e1) * w1
    lo = jnp.minimum(i1, i2)
    hi = jnp.maximum(i1, i2)
    w_lo = g_w * jnp.where(i1 < i2, w1, w2)
    w_hi = g_w * jnp.where(i1 < i2, w2, w1)
    pair = lo * (7.0 - lo) * 0.5 + (hi - lo - 1.0)
    bucket = g_idx * float(N_PAIRS) + pair
    sub = lax.broadcasted_iota(jnp.int32, (8, tm), 0)
    slab = jnp.where(sub == 0, bucket, jnp.where(sub == 1, w_lo, jnp.where(sub == 2, w_hi, 0.0)))
    rt_ref[0] = slab
    sub_l = lax.broadcasted_iota(jnp.int32, (ROUTE_LANES, tm), 0)
    wide = jnp.where(sub_l == 0, bucket, jnp.where(sub_l == 1, w_lo, jnp.where(sub_l == 2, w_hi, 0.0)))
    xe_ref[:, :D_MODEL] = xm
    xe_ref[:, D_MODEL:] = wide.T


def _outproj(ys, yh, x, n_tok, wo1, wo2, nw, wr_t):
    tm = TM_PROJ
    row = lambda i: (i, 0)
    fixed = lambda i: (0, 0)
    return pl.pallas_call(
        _outproj_kernel,
        grid=(n_tok // tm,),
        in_specs=[pl.BlockSpec((tm, SSD_WIDTH), row), pl.BlockSpec((tm, HGRN_WIDTH), row),
                  pl.BlockSpec((tm, D_MODEL), row),
                  pl.BlockSpec((SSD_WIDTH, D_MODEL), fixed), pl.BlockSpec((HGRN_WIDTH, D_MODEL), fixed),
                  pl.BlockSpec((1, D_MODEL), fixed), pl.BlockSpec((LANES, D_MODEL), fixed)],
        out_specs=[pl.BlockSpec((tm, XE_WIDTH), row), pl.BlockSpec((1, 8, tm), lambda i: (i, 0, 0))],
        out_shape=[jax.ShapeDtypeStruct((n_tok, XE_WIDTH), F32),
                   jax.ShapeDtypeStruct((n_tok // tm, 8, tm), F32)],
        compiler_params=pltpu.CompilerParams(dimension_semantics=("arbitrary",),
                                             vmem_limit_bytes=VMEM_LIMIT),
        name="outproj_router",
    )(ys, yh, x, wo1, wo2, nw, wr_t)


def _ffn_kernel(ea_ref, eb_ref, nvalid_ref, nused_ref,
                src0_ref, src1_ref, xe_hbm, nw_ref, fw_ref,
                wga_ref, wgb_ref, wua_ref, wub_ref, wda_ref, wdb_ref,
                out_hbm, xbuf, obuf, gsem, ssem, *, final):
    del ea_ref, eb_ref
    tm = TM_FFN
    j = pl.program_id(0)
    nt = pl.num_programs(0)
    n_used = nused_ref[0]
    slot = j % 2

    def gather_copy(idx, r, s):
        return pltpu.make_async_copy(xe_hbm.at[pl.ds(idx, 1)], xbuf.at[s, pl.ds(r, 1)], gsem.at[s])

    def issue_gather(src_ref, s):
        def body(r, carry):
            gather_copy(src_ref[0, 0, r], r, s).start()
            return carry
        lax.fori_loop(0, tm, body, 0)

    def wait_gather(s):
        pltpu.make_async_copy(xe_hbm.at[pl.ds(0, tm)], xbuf.at[s], gsem.at[s]).wait()

    def scatter_copy(idx, r, s):
        return pltpu.make_async_copy(obuf.at[s, pl.ds(r, 1)], out_hbm.at[pl.ds(idx, 1)], ssem.at[s])

    def wait_scatter(s, tile):
        n = nvalid_ref[tile]
        n8 = pl.multiple_of((n // 8) * 8, 8)

        @pl.when(n8 > 0)
        def _():
            pltpu.make_async_copy(obuf.at[s, pl.ds(0, n8)], out_hbm.at[pl.ds(0, n8)], ssem.at[s]).wait()

        def body(r, carry):
            scatter_copy(0, 0, s).wait()
            return carry
        lax.fori_loop(0, n - n8, body, 0)

    @pl.when((j == 0) & (n_used > 0))
    def _():
        issue_gather(src0_ref, 0)

    @pl.when(j + 1 < n_used)
    def _():
        issue_gather(src1_ref, 1 - slot)

    @pl.when((j >= 2) & (j - 2 < n_used))
    def _():
        wait_scatter(slot, j - 2)

    @pl.when(j < n_used)
    def _():
        wait_gather(slot)
        xe = xbuf[slot]
        xm = xe[:, :D_MODEL]
        w_lo = xe[:, D_MODEL + 1:D_MODEL + 2]
        w_hi = xe[:, D_MODEL + 2:D_MODEL + 3]
        h = _rms(xm, nw_ref[...]).astype(BF16)
        hid_a = _silu(jnp.dot(h, wga_ref[0], preferred_element_type=F32)) \
            * jnp.dot(h, wua_ref[0], preferred_element_type=F32)
        hid_b = _silu(jnp.dot(h, wgb_ref[0], preferred_element_type=F32)) \
            * jnp.dot(h, wub_ref[0], preferred_element_type=F32)
        y = (jnp.dot((hid_a * w_lo).astype(BF16), wda_ref[0], preferred_element_type=F32)
             + jnp.dot((hid_b * w_hi).astype(BF16), wdb_ref[0], preferred_element_type=F32))
        out = xm + y
        if final:
            out = _rms(out, fw_ref[...])
        obuf[slot] = out

        def body(r, carry):
            scatter_copy(src0_ref[0, 0, r], r, slot).start()
            return carry
        lax.fori_loop(0, nvalid_ref[j], body, 0)

    @pl.when(j == nt - 1)
    def _():
        @pl.when((j >= 1) & (j - 1 < n_used))
        def _():
            wait_scatter(1 - slot, j - 1)

        @pl.when(j < n_used)
        def _():
            wait_scatter(slot, j)


def _ffn(xe, n_tok, tile_ea, tile_eb, tile_nvalid, n_used, src, nw, fw, wg, wu, wd, final):
    tm = TM_FFN
    nt = src.shape[0]
    clamp = lambda j, nu: jnp.minimum(j, jnp.maximum(nu[0] - 1, 0))
    wa = lambda j, ea, eb, nv, nu: (ea[clamp(j, nu)], 0, 0)
    wb = lambda j, ea, eb, nv, nu: (eb[clamp(j, nu)], 0, 0)
    fixed = lambda j, ea, eb, nv, nu: (0, 0)
    smem_blk = lambda f: pl.BlockSpec((1, 1, tm), f, memory_space=pltpu.SMEM)
    grid_spec = pltpu.PrefetchScalarGridSpec(
        num_scalar_prefetch=4,
        grid=(nt,),
        in_specs=[smem_blk(lambda j, ea, eb, nv, nu: (j, 0, 0)),
                  smem_blk(lambda j, ea, eb, nv, nu: (jnp.minimum(j + 1, nt - 1), 0, 0)),
                  pl.BlockSpec(memory_space=pl.ANY),
                  pl.BlockSpec((1, D_MODEL), fixed), pl.BlockSpec((1, D_MODEL), fixed),
                  pl.BlockSpec((1, D_MODEL, EXPERT_DIM), wa), pl.BlockSpec((1, D_MODEL, EXPERT_DIM), wb),
                  pl.BlockSpec((1, D_MODEL, EXPERT_DIM), wa), pl.BlockSpec((1, D_MODEL, EXPERT_DIM), wb),
                  pl.BlockSpec((1, EXPERT_DIM, D_MODEL), wa), pl.BlockSpec((1, EXPERT_DIM, D_MODEL), wb)],
        out_specs=pl.BlockSpec(memory_space=pl.ANY),
        scratch_shapes=[pltpu.VMEM((2, tm, XE_WIDTH), F32), pltpu.VMEM((2, tm, D_MODEL), F32),
                        pltpu.SemaphoreType.DMA((2,)), pltpu.SemaphoreType.DMA((2,))],
    )
    return pl.pallas_call(
        functools.partial(_ffn_kernel, final=final),
        grid_spec=grid_spec,
        out_shape=jax.ShapeDtypeStruct((n_tok, D_MODEL), F32),
        compiler_params=pltpu.CompilerParams(dimension_semantics=("arbitrary",),
                                             vmem_limit_bytes=VMEM_LIMIT),
        name="moe_ffn",
    )(tile_ea, tile_eb, tile_nvalid, n_used, src, src, xe, nw, fw, wg, wg, wu, wu, wd, wd)


def _pair_tables():
    lo, hi = [], []
    for a in range(EXPERTS_PER_GROUP):
        for b in range(a + 1, EXPERTS_PER_GROUP):
            lo.append(a)
            hi.append(b)
    return lo, hi


def _dispatch_plan(bucket, n_tok):
    tm = TM_FFN
    nt = n_tok // tm + N_BUCKETS
    i32 = jnp.int32
    onehot = (bucket[:, None] == jnp.arange(N_BUCKETS, dtype=i32)[None, :]).astype(i32)
    csum = jnp.cumsum(onehot, axis=0)
    counts = csum[-1]
    rank = jnp.sum(csum * onehot, axis=1) - 1
    tiles_b = (counts + tm - 1) // tm
    tile_end = jnp.cumsum(tiles_b)
    tile_start = tile_end - tiles_b
    n_used = tile_end[-1:]
    pos = tile_start[bucket] * tm + rank
    tile_id = jnp.arange(nt, dtype=i32)
    tile_bucket = jnp.minimum(jnp.searchsorted(tile_end, tile_id, side="right").astype(i32),
                              N_BUCKETS - 1)
    tile_nvalid = jnp.clip(counts[tile_bucket] - (tile_id - tile_start[tile_bucket]) * tm, 0, tm)
    tile_nvalid = jnp.where(tile_id < n_used[0], tile_nvalid, 0)
    src = jnp.zeros((nt * tm,), i32).at[pos].set(jnp.arange(n_tok, dtype=i32))
    lo, hi = _pair_tables()
    grp = tile_bucket // N_PAIRS
    pr = tile_bucket % N_PAIRS
    tile_ea = grp * EXPERTS_PER_GROUP + jnp.asarray(lo, i32)[pr]
    tile_eb = grp * EXPERTS_PER_GROUP + jnp.asarray(hi, i32)[pr]
    return (tile_ea, tile_eb, tile_nvalid.astype(i32), n_used.astype(i32), src.reshape(nt, 1, tm))


def kernel(x, norm_mix_w, w_in, conv_w, conv_b, dt_bias, a_log, d_skip, ssd_norm_w, hgrn_lb_logits,
           hgrn_norm_w, w_out, norm_ffn_w, router_group, router_expert, w_gate, w_up, w_down,
           final_norm_w):
    bsz, seq, _ = x.shape
    depth = w_in.shape[0]
    n_tok = bsz * seq
    assert n_tok % TM_PROJ == 0 and seq % TS_MIX == 0 and n_tok % TM_FFN == 0

    p = jax.nn.softmax(hgrn_lb_logits.astype(F32), axis=0)
    lower_bounds = jnp.cumsum(p, axis=0) - p[0:1]

    o_xbc = SSD_WIDTH
    o_dt = o_xbc + XBC_WIDTH
    o_q = o_dt + SSD_HEADS
    pad_h = lambda v: jnp.pad(v.astype(F32), (0, LANES - SSD_HEADS)).reshape(1, LANES)
    row = lambda v: v.astype(F32).reshape(1, -1)

    xcur = x.reshape(n_tok, D_MODEL)
    for l in range(depth):
        wl = w_in[l].astype(BF16)
        wq, wf, wi, wg = (wl[:, o_q + k * HGRN_WIDTH:o_q + (k + 1) * HGRN_WIDTH] for k in range(4))
        wdt = jnp.pad(wl[:, o_dt:o_q], ((0, 0), (0, LANES - SSD_HEADS)))
        z, xbc, q, f, i, g, dt = _inproj(xcur, n_tok, row(norm_mix_w[l]), wl[:, :o_xbc],
                                         wl[:, o_xbc:o_dt], wq, wf, wi, wg, wdt)
        y_ssd = _ssd(z, xbc, dt, bsz, seq, conv_w[l].astype(F32), row(conv_b[l]), pad_h(dt_bias[l]),
                     pad_h(a_log[l]), row(jnp.repeat(d_skip[l], SSD_HEAD_DIM)), row(ssd_norm_w[l]))
        y_hgrn = _hgrn(q, f, i, g, bsz, seq, row(lower_bounds[l]), row(hgrn_norm_w[l]))
        wo = w_out[l].astype(BF16)
        wr_t = jnp.concatenate([router_group[l], router_expert[l]], axis=1).T
        wr_t = jnp.pad(wr_t, ((0, LANES - wr_t.shape[0]), (0, 0))).astype(BF16)
        xe, route = _outproj(y_ssd, y_hgrn, xcur, n_tok, wo[:SSD_WIDTH], wo[SSD_WIDTH:],
                             row(norm_ffn_w[l]), wr_t)
        bucket = route[:, 0, :].reshape(n_tok).astype(jnp.int32)
        tile_ea, tile_eb, tile_nvalid, n_used, src = _dispatch_plan(bucket, n_tok)
        xcur = _ffn(xe, n_tok, tile_ea, tile_eb, tile_nvalid, n_used, src, row(norm_ffn_w[l]),
                    row(final_norm_w), w_gate[l].astype(BF16), w_up[l].astype(BF16),
                    w_down[l].astype(BF16), final=(l == depth - 1))
    return xcur.reshape(bsz, seq, D_MODEL)
```

```python
import functools

import jax
import jax.numpy as jnp
from jax import lax
from jax.experimental import pallas as pl
from jax.experimental.pallas import tpu as pltpu

F32 = jnp.float32
BF16 = jnp.bfloat16

D_MODEL = 1024
SSD_HEADS = 8
SSD_HEAD_DIM = 64
SSD_WIDTH = SSD_HEADS * SSD_HEAD_DIM
SSD_GROUPS = 2
SSD_STATE = 128
CONV_WIDTH = 4
XBC_WIDTH = SSD_WIDTH + 2 * SSD_GROUPS * SSD_STATE
HGRN_HEADS = 4
HGRN_HEAD_DIM = 128
HGRN_WIDTH = HGRN_HEADS * HGRN_HEAD_DIM
HGRN_BLOCK = 32
N_EXPERT_GROUPS = 4
EXPERTS_PER_GROUP = 4
N_EXPERTS = N_EXPERT_GROUPS * EXPERTS_PER_GROUP
EXPERT_DIM = 256
EPS = 1e-6

LANES = 128
N_PAIRS = 6
N_BUCKETS = N_EXPERT_GROUPS * N_PAIRS
ROUTE_LANES = LANES
XE_WIDTH = D_MODEL + ROUTE_LANES
NEG_BIG = -1e30

TM_PROJ = 512
TS_MIX = 256
SSD_SUB = 128
TM_FFN = 256
DMA_UNROLL = 8
VMEM_LIMIT = 56 * 1024 * 1024


def _mm(a, b):
    return jnp.dot(a.astype(BF16), b.astype(BF16), preferred_element_type=F32)


def _mm_nt(a, b):
    return lax.dot_general(a.astype(BF16), b.astype(BF16), (((1,), (1,)), ((), ())),
                           preferred_element_type=F32)


def _mm_tn(a, b):
    return lax.dot_general(a.astype(BF16), b.astype(BF16), (((0,), (0,)), ((), ())),
                           preferred_element_type=F32)


def _split3(x):
    p1 = x.astype(BF16)
    r1 = x - p1.astype(F32)
    p2 = r1.astype(BF16)
    p3 = (r1 - p2.astype(F32)).astype(BF16)
    return p1, p2, p3


def _cumsum_mm(tri, x):
    p1, p2, p3 = _split3(x)
    acc = jnp.dot(tri, p1, preferred_element_type=F32)
    acc = acc + jnp.dot(tri, p2, preferred_element_type=F32)
    return acc + jnp.dot(tri, p3, preferred_element_type=F32)


def _sigmoid(x):
    return 1.0 / (1.0 + jnp.exp(-x))


def _silu(x):
    return x * _sigmoid(x)


def _rms(x, w):
    return x * lax.rsqrt(jnp.mean(x * x, axis=-1, keepdims=True) + EPS) * w


def _inproj_kernel(x_ref, nw_ref, wz_ref, wxbc_ref, wq_ref, wf_ref, wi_ref, wg_ref, wdt_ref,
                   z_ref, xbc_ref, q_ref, f_ref, i_ref, g_ref, dt_ref):
    h = _rms(x_ref[...], nw_ref[...]).astype(BF16)
    for w_ref, o_ref in ((wz_ref, z_ref), (wxbc_ref, xbc_ref), (wq_ref, q_ref),
                         (wf_ref, f_ref), (wi_ref, i_ref), (wg_ref, g_ref), (wdt_ref, dt_ref)):
        o_ref[...] = jnp.dot(h, w_ref[...], preferred_element_type=F32).astype(o_ref.dtype)


def _inproj(x, n_tok, nw, wz, wxbc, wq, wf, wi, wg, wdt):
    tm = TM_PROJ
    row = lambda i: (i, 0)
    fixed = lambda i: (0, 0)
    widths = (SSD_WIDTH, XBC_WIDTH, HGRN_WIDTH, HGRN_WIDTH, HGRN_WIDTH, HGRN_WIDTH, LANES)
    dtypes = (BF16,) * 6 + (F32,)
    return pl.pallas_call(
        _inproj_kernel,
        grid=(n_tok // tm,),
        in_specs=[pl.BlockSpec((tm, D_MODEL), row), pl.BlockSpec((1, D_MODEL), fixed)]
        + [pl.BlockSpec((D_MODEL, w), fixed) for w in widths],
        out_specs=[pl.BlockSpec((tm, w), row) for w in widths],
        out_shape=[jax.ShapeDtypeStruct((n_tok, w), dt) for w, dt in zip(widths, dtypes)],
        compiler_params=pltpu.CompilerParams(dimension_semantics=("arbitrary",),
                                             vmem_limit_bytes=VMEM_LIMIT),
        name="inproj",
    )(x, nw, wz, wxbc, wq, wf, wi, wg, wdt)


def _lane_pair(col_even, col_odd, rows):
    lane = lax.broadcasted_iota(jnp.int32, (rows, LANES), 1)
    return jnp.where(lane < SSD_HEAD_DIM, jnp.broadcast_to(col_even, (rows, LANES)),
                     jnp.broadcast_to(col_odd, (rows, LANES)))


def _ssd_kernel(z_ref, xbc_ref, dt_ref, cw_ref, cb_ref, dtb_ref, alog_ref, dsk_ref, nw_ref,
                y_ref, ext_ref, h_ref):
    ts = z_ref.shape[0]
    L = SSD_SUB
    hpg = SSD_HEADS // SSD_GROUPS
    gw = SSD_WIDTH // SSD_GROUPS

    @pl.when(pl.program_id(1) == 0)
    def _():
        ext_ref[0:8, :] = jnp.zeros((8, XBC_WIDTH), F32)
        h_ref[...] = jnp.zeros_like(h_ref)

    ext_ref[8:8 + ts, :] = xbc_ref[...].astype(F32)

    r_i = lax.broadcasted_iota(jnp.int32, (L, L), 0)
    c_i = lax.broadcasted_iota(jnp.int32, (L, L), 1)
    causal = r_i >= c_i
    tri = causal.astype(BF16)
    lane = lax.broadcasted_iota(jnp.int32, (L, LANES), 1)
    even = lane < SSD_HEAD_DIM
    a_neg = -jnp.exp(alog_ref[...])

    for c in range(ts // L):
        r0 = c * L
        conv = cb_ref[...]
        for k in range(CONV_WIDTH):
            conv = conv + cw_ref[k:k + 1, :] * ext_ref[r0 + 5 + k:r0 + 5 + k + L, :]
        conv = _silu(conv)
        xs = conv[:, :SSD_WIDTH]
        bm = conv[:, SSD_WIDTH:SSD_WIDTH + SSD_GROUPS * SSD_STATE]
        cm = conv[:, SSD_WIDTH + SSD_GROUPS * SSD_STATE:]

        dt_raw = dt_ref[r0:r0 + L, :] + dtb_ref[...]
        dt = jnp.maximum(dt_raw, 0.0) + jnp.log(1.0 + jnp.exp(-jnp.abs(dt_raw)))
        cum = _cumsum_mm(tri, dt * a_neg)
        cum_t = cum.T
        dt_t = dt.T
        cum_last = cum[L - 1:L, :]
        dec_end = jnp.exp(cum_last - cum) * dt
        e_cum = jnp.exp(cum)
        e_last = jnp.exp(cum_last)

        zc = z_ref[r0:r0 + L, :].astype(F32)
        for g in range(SSD_GROUPS):
            bm_g = bm[:, g * SSD_STATE:(g + 1) * SSD_STATE].astype(BF16)
            cm_g = cm[:, g * SSD_STATE:(g + 1) * SSD_STATE].astype(BF16)
            cb = _mm_nt(cm_g, bm_g)
            h_t = h_ref[g]
            y_in = _mm(cm_g, h_t)
            y_parts, xsc_parts, el_parts = [], [], []
            for p in range(hpg // 2):
                h0 = g * hpg + 2 * p
                xp = xs[:, h0 * SSD_HEAD_DIM:(h0 + 2) * SSD_HEAD_DIM]
                xp_b = xp.astype(BF16)
                y_heads = []
                for hh in (h0, h0 + 1):
                    seg = cum[:, hh:hh + 1] - cum_t[hh:hh + 1, :]
                    sc = cb * jnp.exp(jnp.where(causal, seg, NEG_BIG)) * dt_t[hh:hh + 1, :]
                    y_heads.append(jnp.dot(sc.astype(BF16), xp_b, preferred_element_type=F32))
                y_intra = jnp.where(even, y_heads[0], y_heads[1])
                e_pair = _lane_pair(e_cum[:, h0:h0 + 1], e_cum[:, h0 + 1:h0 + 2], L)
                d_pair = _lane_pair(dec_end[:, h0:h0 + 1], dec_end[:, h0 + 1:h0 + 2], L)
                y_parts.append(y_intra + y_in[:, 2 * p * SSD_HEAD_DIM:(2 * p + 2) * SSD_HEAD_DIM] * e_pair
                               + dsk_ref[:, h0 * SSD_HEAD_DIM:(h0 + 2) * SSD_HEAD_DIM] * xp)
                xsc_parts.append(xp * d_pair)
                el_parts.append(_lane_pair(e_last[:, h0:h0 + 1], e_last[:, h0 + 1:h0 + 2], 1))
            xsc = jnp.concatenate(xsc_parts, axis=1)
            el = jnp.concatenate(el_parts, axis=1)
            h_ref[g] = h_t * el + _mm_tn(bm_g, xsc)
            yg = jnp.concatenate(y_parts, axis=1) * _silu(zc[:, g * gw:(g + 1) * gw])
            yg = _rms(yg, nw_ref[:, g * gw:(g + 1) * gw])
            y_ref[r0:r0 + L, g * gw:(g + 1) * gw] = yg.astype(y_ref.dtype)

    ext_ref[0:8, :] = ext_ref[ts:ts + 8, :]


def _ssd(z, xbc, dt, bsz, seq, cw, cb, dtb, alog, dsk, nw):
    ts = TS_MIX
    nst = seq // ts
    row = lambda b, s: (b * nst + s, 0)
    fixed = lambda b, s: (0, 0)
    return pl.pallas_call(
        _ssd_kernel,
        grid=(bsz, nst),
        in_specs=[pl.BlockSpec((ts, SSD_WIDTH), row), pl.BlockSpec((ts, XBC_WIDTH), row),
                  pl.BlockSpec((ts, LANES), row),
                  pl.BlockSpec((CONV_WIDTH, XBC_WIDTH), fixed), pl.BlockSpec((1, XBC_WIDTH), fixed),
                  pl.BlockSpec((1, LANES), fixed), pl.BlockSpec((1, LANES), fixed),
                  pl.BlockSpec((1, SSD_WIDTH), fixed), pl.BlockSpec((1, SSD_WIDTH), fixed)],
        out_specs=pl.BlockSpec((ts, SSD_WIDTH), row),
        out_shape=jax.ShapeDtypeStruct((bsz * seq, SSD_WIDTH), BF16),
        scratch_shapes=[pltpu.VMEM((ts + 8, XBC_WIDTH), F32),
                        pltpu.VMEM((SSD_GROUPS, SSD_STATE, SSD_WIDTH // SSD_GROUPS), F32)],
        compiler_params=pltpu.CompilerParams(dimension_semantics=("arbitrary", "arbitrary"),
                                             vmem_limit_bytes=VMEM_LIMIT),
        name="ssd",
    )(z, xbc, dt, cw, cb, dtb, alog, dsk, nw)


def _hgrn_kernel(q_ref, f_ref, i_ref, g_ref, lb_ref, nw_ref, o_ref, st_ref):
    ts = q_ref.shape[0]
    blk = HGRN_BLOCK
    nb = ts // blk
    hd = HGRN_HEAD_DIM

    @pl.when(pl.program_id(1) == 0)
    def _():
        st_ref[...] = jnp.zeros_like(st_ref)

    r_i = lax.broadcasted_iota(jnp.int32, (ts, ts), 0)
    c_i = lax.broadcasted_iota(jnp.int32, (ts, ts), 1)
    blk_causal = (r_i >= c_i) & ((r_i // blk) == (c_i // blk))
    tri = blk_causal.astype(BF16)

    for h in range(HGRN_HEADS):
        sl = slice(h * hd, (h + 1) * hd)
        lb = lb_ref[:, sl]
        forget = lb + (1.0 - lb) * _sigmoid(f_ref[:, sl].astype(F32))
        kk = 1.0 - forget
        cum = _cumsum_mm(tri, jnp.log(forget))
        cum3 = cum.reshape(nb, blk, hd)
        b_mid = cum3[:, blk // 2:blk // 2 + 1, :]
        b_end = cum3[:, blk - 1:blk, :]
        rel = (cum3 - b_mid).reshape(ts, hd)
        to_end = (b_end - cum3).reshape(ts, hd)
        q = q_ref[:, sl].astype(F32)
        v = i_ref[:, sl].astype(BF16)
        qs = q * jnp.exp(rel)
        ks = kk * jnp.exp(-rel)
        sc = jnp.where(blk_causal, _mm_nt(qs, ks), 0.0)
        o_intra = jnp.dot(sc.astype(BF16), v, preferred_element_type=F32)
        q_dec = (q * jnp.exp(cum)).astype(BF16)
        k_end = (kk * jnp.exp(to_end)).astype(BF16)
        dec = jnp.exp(b_end)
        st = st_ref[h]
        o_parts = []
        for n in range(nb):
            rs = slice(n * blk, (n + 1) * blk)
            o_parts.append(_mm_nt(q_dec[rs], st))
            st = st * dec[n] + _mm_tn(v[rs], k_end[rs])
        st_ref[h] = st
        o = o_intra + jnp.concatenate(o_parts, axis=0)
        o = _rms(o, nw_ref[:, sl]) * _silu(g_ref[:, sl].astype(F32))
        o_ref[:, sl] = o.astype(o_ref.dtype)


def _hgrn(q, f, i, g, bsz, seq, lb, nw):
    ts = TS_MIX
    nst = seq // ts
    row = lambda b, s: (b * nst + s, 0)
    fixed = lambda b, s: (0, 0)
    return pl.pallas_call(
        _hgrn_kernel,
        grid=(bsz, nst),
        in_specs=[pl.BlockSpec((ts, HGRN_WIDTH), row)] * 4
        + [pl.BlockSpec((1, HGRN_WIDTH), fixed)] * 2,
        out_specs=pl.BlockSpec((ts, HGRN_WIDTH), row),
        out_shape=jax.ShapeDtypeStruct((bsz * seq, HGRN_WIDTH), BF16),
        scratch_shapes=[pltpu.VMEM((HGRN_HEADS, HGRN_HEAD_DIM, HGRN_HEAD_DIM), F32)],
        compiler_params=pltpu.CompilerParams(dimension_semantics=("arbitrary", "arbitrary"),
                                             vmem_limit_bytes=VMEM_LIMIT),
        name="hgrn",
    )(q, f, i, g, lb, nw)


def _first_argmax4(v):
    m = jnp.maximum(jnp.maximum(v[0], v[1]), jnp.maximum(v[2], v[3]))
    idx = jnp.where(v[0] == m, 0.0, jnp.where(v[1] == m, 1.0, jnp.where(v[2] == m, 2.0, 3.0)))
    return m, idx


def _outproj_kernel(ys_ref, yh_ref, x_ref, wo1_ref, wo2_ref, nw_ref, wr_ref, xe_ref, rt_ref, cnt_ref,
                    base_ref):
    tm = x_ref.shape[0]

    @pl.when(pl.program_id(0) == 0)
    def _():
        base_ref[...] = jnp.zeros_like(base_ref)

    xm = (x_ref[...] + jnp.dot(ys_ref[...], wo1_ref[...], preferred_element_type=F32)
          + jnp.dot(yh_ref[...], wo2_ref[...], preferred_element_type=F32))
    h = _rms(xm, nw_ref[...])
    lt = _mm_nt(wr_ref[...], h)
    row = lambda r: lt[r:r + 1, :]
    gl = [row(j) for j in range(N_EXPERT_GROUPS)]
    g_max, g_idx = _first_argmax4(gl)
    g_w = 1.0 / (jnp.exp(gl[0] - g_max) + jnp.exp(gl[1] - g_max)
                 + jnp.exp(gl[2] - g_max) + jnp.exp(gl[3] - g_max))
    el = []
    for j in range(EXPERTS_PER_GROUP):
        base = N_EXPERT_GROUPS + j
        el.append(jnp.where(g_idx == 0.0, row(base),
                  jnp.where(g_idx == 1.0, row(base + EXPERTS_PER_GROUP),
                  jnp.where(g_idx == 2.0, row(base + 2 * EXPERTS_PER_GROUP),
                            row(base + 3 * EXPERTS_PER_GROUP)))))
    e1, i1 = _first_argmax4(el)
    el2 = [jnp.where(i1 == float(j), NEG_BIG, el[j]) for j in range(EXPERTS_PER_GROUP)]
    e2, i2 = _first_argmax4(el2)
    w1 = 1.0 / (1.0 + jnp.exp(e2 - e1))
    w2 = jnp.exp(e2 - e1) * w1
    lo = jnp.minimum(i1, i2)
    hi = jnp.maximum(i1, i2)
    w_lo = g_w * jnp.where(i1 < i2, w1, w2)
    w_hi = g_w * jnp.where(i1 < i2, w2, w1)
    pair = lo * (7.0 - lo) * 0.5 + (hi - lo - 1.0)
    bucket = g_idx * float(N_PAIRS) + pair
    sub_l = lax.broadcasted_iota(jnp.int32, (ROUTE_LANES, tm), 0)
    onehot = jnp.where(sub_l.astype(F32) == bucket, 1.0, 0.0)
    r_i = lax.broadcasted_iota(jnp.int32, (tm, tm), 0)
    c_i = lax.broadcasted_iota(jnp.int32, (tm, tm), 1)
    before = jnp.dot(onehot.astype(BF16), (r_i < c_i).astype(BF16), preferred_element_type=F32)
    base = base_ref[...]
    before = before + jnp.concatenate([base] * (tm // LANES), axis=1)
    rank = jnp.sum(onehot * before, axis=0, keepdims=True)
    base = base + jnp.sum(onehot, axis=1, keepdims=True)
    base_ref[...] = base
    cnt_ref[...] = base
    sub = lax.broadcasted_iota(jnp.int32, (8, tm), 0)
    slab = jnp.where(sub == 0, bucket, jnp.where(sub == 1, w_lo, jnp.where(sub == 2, w_hi,
                     jnp.where(sub == 3, rank, 0.0))))
    rt_ref[0] = slab
    wide = jnp.where(sub_l == 0, bucket, jnp.where(sub_l == 1, w_lo, jnp.where(sub_l == 2, w_hi, 0.0)))
    xe_ref[:, :D_MODEL] = xm
    xe_ref[:, D_MODEL:] = wide.T


def _outproj(ys, yh, x, n_tok, wo1, wo2, nw, wr_t):
    tm = TM_PROJ
    row = lambda i: (i, 0)
    fixed = lambda i: (0, 0)
    return pl.pallas_call(
        _outproj_kernel,
        grid=(n_tok // tm,),
        in_specs=[pl.BlockSpec((tm, SSD_WIDTH), row), pl.BlockSpec((tm, HGRN_WIDTH), row),
                  pl.BlockSpec((tm, D_MODEL), row),
                  pl.BlockSpec((SSD_WIDTH, D_MODEL), fixed), pl.BlockSpec((HGRN_WIDTH, D_MODEL), fixed),
                  pl.BlockSpec((1, D_MODEL), fixed), pl.BlockSpec((LANES, D_MODEL), fixed)],
        out_specs=[pl.BlockSpec((tm, XE_WIDTH), row), pl.BlockSpec((1, 8, tm), lambda i: (i, 0, 0)),
                   pl.BlockSpec((ROUTE_LANES, LANES), fixed)],
        out_shape=[jax.ShapeDtypeStruct((n_tok, XE_WIDTH), F32),
                   jax.ShapeDtypeStruct((n_tok // tm, 8, tm), F32),
                   jax.ShapeDtypeStruct((ROUTE_LANES, LANES), F32)],
        scratch_shapes=[pltpu.VMEM((ROUTE_LANES, LANES), F32)],
        compiler_params=pltpu.CompilerParams(dimension_semantics=("arbitrary",),
                                             vmem_limit_bytes=VMEM_LIMIT),
        name="outproj_router",
    )(ys, yh, x, wo1, wo2, nw, wr_t)


def _ffn_kernel(ea_ref, eb_ref, nvalid_ref, nused_ref,
                src0_ref, src1_ref, xe_hbm, nw_ref, fw_ref,
                wga_ref, wgb_ref, wua_ref, wub_ref, wda_ref, wdb_ref,
                out_hbm, xbuf0, xbuf1, obuf0, obuf1, gsem, ssem, *, final):
    del ea_ref, eb_ref
    tm = TM_FFN
    j = pl.program_id(0)
    nt = pl.num_programs(0)
    n_used = nused_ref[0]
    xbufs = (xbuf0, xbuf1)
    obufs = (obuf0, obuf1)

    def issue_gather(src_ref, p):
        for r in range(tm):
            pltpu.make_async_copy(xe_hbm.at[pl.ds(src_ref[0, 0, r], 1)], xbufs[p].at[pl.ds(r, 1)],
                                  gsem.at[p]).start()

    def wait_gather(p):
        pltpu.make_async_copy(xe_hbm.at[pl.ds(0, tm)], xbufs[p], gsem.at[p]).wait()

    def scatter_copy(r, p):
        return pltpu.make_async_copy(obufs[p].at[pl.ds(r, 1)],
                                     out_hbm.at[pl.ds(src0_ref[0, 0, r], 1)], ssem.at[p])

    def wait_scatter(p, tile):
        n = nvalid_ref[tile]
        n8 = pl.multiple_of((n // 8) * 8, 8)

        @pl.when(n8 > 0)
        def _():
            pltpu.make_async_copy(obufs[p].at[pl.ds(0, n8)], out_hbm.at[pl.ds(0, n8)],
                                  ssem.at[p]).wait()

        def body(r, carry):
            pltpu.make_async_copy(obufs[p].at[pl.ds(0, 1)], out_hbm.at[pl.ds(0, 1)], ssem.at[p]).wait()
            return carry
        lax.fori_loop(0, n - n8, body, 0)

    @pl.when((j == 0) & (n_used > 0))
    def _():
        issue_gather(src0_ref, 0)

    def step(p):
        @pl.when(j >= 2)
        def _():
            wait_scatter(p, j - 2)

        wait_gather(p)
        issue_gather(src1_ref, 1 - p)
        xe = xbufs[p][...]
        xm = xe[:, :D_MODEL]
        w_lo = xe[:, D_MODEL + 1:D_MODEL + 2]
        w_hi = xe[:, D_MODEL + 2:D_MODEL + 3]
        h = _rms(xm, nw_ref[...]).astype(BF16)
        hid_a = _silu(jnp.dot(h, wga_ref[0], preferred_element_type=F32)) \
            * jnp.dot(h, wua_ref[0], preferred_element_type=F32)
        hid_b = _silu(jnp.dot(h, wgb_ref[0], preferred_element_type=F32)) \
            * jnp.dot(h, wub_ref[0], preferred_element_type=F32)
        y = (jnp.dot((hid_a * w_lo).astype(BF16), wda_ref[0], preferred_element_type=F32)
             + jnp.dot((hid_b * w_hi).astype(BF16), wdb_ref[0], preferred_element_type=F32))
        out = xm + y
        if final:
            out = _rms(out, fw_ref[...])
        obufs[p][...] = out

        n_valid = nvalid_ref[j]

        @pl.when(n_valid == tm)
        def _():
            for r in range(tm):
                scatter_copy(r, p).start()

        @pl.when(n_valid < tm)
        def _():
            def single(r, carry):
                scatter_copy(r, p).start()
                return carry
            lax.fori_loop(0, n_valid, single, 0)

    for p in (0, 1):
        @pl.when((j % 2 == p) & (j < n_used))
        def _():
            step(p)

    @pl.when(j == nt - 1)
    def _():
        for p in (0, 1):
            @pl.when((n_used >= 1) & (n_used % 2 == p))
            def _():
                wait_gather(p)

            @pl.when((n_used >= 1) & ((n_used - 1) % 2 == p))
            def _():
                wait_scatter(p, n_used - 1)

            @pl.when((n_used >= 2) & ((n_used - 2) % 2 == p))
            def _():
                wait_scatter(p, n_used - 2)


def _ffn(xe, n_tok, tile_ea, tile_eb, tile_nvalid, n_used, src, nw, fw, wg, wu, wd, final):
    tm = TM_FFN
    nt = src.shape[0]
    clamp = lambda j, nu: jnp.minimum(j, jnp.maximum(nu[0] - 1, 0))
    wa = lambda j, ea, eb, nv, nu: (ea[clamp(j, nu)], 0, 0)
    wb = lambda j, ea, eb, nv, nu: (eb[clamp(j, nu)], 0, 0)
    fixed = lambda j, ea, eb, nv, nu: (0, 0)
    smem_blk = lambda f: pl.BlockSpec((1, 1, tm), f, memory_space=pltpu.SMEM)
    grid_spec = pltpu.PrefetchScalarGridSpec(
        num_scalar_prefetch=4,
        grid=(nt,),
        in_specs=[smem_blk(lambda j, ea, eb, nv, nu: (j, 0, 0)),
                  smem_blk(lambda j, ea, eb, nv, nu: (jnp.minimum(j + 1, nt - 1), 0, 0)),
                  pl.BlockSpec(memory_space=pl.ANY),
                  pl.BlockSpec((1, D_MODEL), fixed), pl.BlockSpec((1, D_MODEL), fixed),
                  pl.BlockSpec((1, D_MODEL, EXPERT_DIM), wa), pl.BlockSpec((1, D_MODEL, EXPERT_DIM), wb),
                  pl.BlockSpec((1, D_MODEL, EXPERT_DIM), wa), pl.BlockSpec((1, D_MODEL, EXPERT_DIM), wb),
                  pl.BlockSpec((1, EXPERT_DIM, D_MODEL), wa), pl.BlockSpec((1, EXPERT_DIM, D_MODEL), wb)],
        out_specs=pl.BlockSpec(memory_space=pl.ANY),
        scratch_shapes=[pltpu.VMEM((tm, XE_WIDTH), F32), pltpu.VMEM((tm, XE_WIDTH), F32),
                        pltpu.VMEM((tm, D_MODEL), F32), pltpu.VMEM((tm, D_MODEL), F32),
                        pltpu.SemaphoreType.DMA((2,)), pltpu.SemaphoreType.DMA((2,))],
    )
    return pl.pallas_call(
        functools.partial(_ffn_kernel, final=final),
        grid_spec=grid_spec,
        out_shape=jax.ShapeDtypeStruct((n_tok, D_MODEL), F32),
        compiler_params=pltpu.CompilerParams(dimension_semantics=("arbitrary",),
                                             vmem_limit_bytes=VMEM_LIMIT),
        name="moe_ffn",
    )(tile_ea, tile_eb, tile_nvalid, n_used, src, src, xe, nw, fw, wg, wg, wu, wu, wd, wd)


def _pair_tables():
    lo, hi = [], []
    for a in range(EXPERTS_PER_GROUP):
        for b in range(a + 1, EXPERTS_PER_GROUP):
            lo.append(a)
            hi.append(b)
    return lo, hi


def _dispatch_plan(bucket, rank, counts, n_tok):
    tm = TM_FFN
    nt = n_tok // tm + N_BUCKETS
    i32 = jnp.int32
    tiles_b = (counts + tm - 1) // tm
    tile_end = jnp.cumsum(tiles_b)
    tile_start = tile_end - tiles_b
    n_used = tile_end[-1:]
    pos = tile_start[bucket] * tm + rank
    tile_id = jnp.arange(nt, dtype=i32)
    tile_bucket = jnp.sum((tile_end[None, :] <= tile_id[:, None]).astype(i32), axis=1)
    tile_bucket = jnp.minimum(tile_bucket, N_BUCKETS - 1)
    tile_nvalid = jnp.clip(counts[tile_bucket] - (tile_id - tile_start[tile_bucket]) * tm, 0, tm)
    tile_nvalid = jnp.where(tile_id < n_used[0], tile_nvalid, 0)
    src = jnp.zeros((nt * tm,), i32).at[pos].set(jnp.arange(n_tok, dtype=i32), unique_indices=True)
    lo, hi = _pair_tables()
    grp = tile_bucket // N_PAIRS
    pr = tile_bucket % N_PAIRS
    tile_ea = grp * EXPERTS_PER_GROUP + jnp.asarray(lo, i32)[pr]
    tile_eb = grp * EXPERTS_PER_GROUP + jnp.asarray(hi, i32)[pr]
    return (tile_ea, tile_eb, tile_nvalid.astype(i32), n_used.astype(i32), src.reshape(nt, 1, tm))


def kernel(x, norm_mix_w, w_in, conv_w, conv_b, dt_bias, a_log, d_skip, ssd_norm_w, hgrn_lb_logits,
           hgrn_norm_w, w_out, norm_ffn_w, router_group, router_expert, w_gate, w_up, w_down,
           final_norm_w):
    bsz, seq, _ = x.shape
    depth = w_in.shape[0]
    n_tok = bsz * seq
    assert n_tok % TM_PROJ == 0 and seq % TS_MIX == 0 and n_tok % TM_FFN == 0

    p = jax.nn.softmax(hgrn_lb_logits.astype(F32), axis=0)
    lower_bounds = jnp.cumsum(p, axis=0) - p[0:1]

    o_xbc = SSD_WIDTH
    o_dt = o_xbc + XBC_WIDTH
    o_q = o_dt + SSD_HEADS
    pad_h = lambda v: jnp.pad(v.astype(F32), (0, LANES - SSD_HEADS)).reshape(1, LANES)
    row = lambda v: v.astype(F32).reshape(1, -1)

    xcur = x.reshape(n_tok, D_MODEL)
    for l in range(depth):
        wl = w_in[l].astype(BF16)
        wq, wf, wi, wg = (wl[:, o_q + k * HGRN_WIDTH:o_q + (k + 1) * HGRN_WIDTH] for k in range(4))
        wdt = jnp.pad(wl[:, o_dt:o_q], ((0, 0), (0, LANES - SSD_HEADS)))
        z, xbc, q, f, i, g, dt = _inproj(xcur, n_tok, row(norm_mix_w[l]), wl[:, :o_xbc],
                                         wl[:, o_xbc:o_dt], wq, wf, wi, wg, wdt)
        y_ssd = _ssd(z, xbc, dt, bsz, seq, conv_w[l].astype(F32), row(conv_b[l]), pad_h(dt_bias[l]),
                     pad_h(a_log[l]), row(jnp.repeat(d_skip[l], SSD_HEAD_DIM)), row(ssd_norm_w[l]))
        y_hgrn = _hgrn(q, f, i, g, bsz, seq, row(lower_bounds[l]), row(hgrn_norm_w[l]))
        wo = w_out[l].astype(BF16)
        wr_t = jnp.concatenate([router_group[l], router_expert[l]], axis=1).T
        wr_t = jnp.pad(wr_t, ((0, LANES - wr_t.shape[0]), (0, 0))).astype(BF16)
        xe, route, cnt = _outproj(y_ssd, y_hgrn, xcur, n_tok, wo[:SSD_WIDTH], wo[SSD_WIDTH:],
                                  row(norm_ffn_w[l]), wr_t)
        bucket = route[:, 0, :].reshape(n_tok).astype(jnp.int32)
        rank = route[:, 3, :].reshape(n_tok).astype(jnp.int32)
        counts = cnt[:N_BUCKETS, 0].astype(jnp.int32)
        tile_ea, tile_eb, tile_nvalid, n_used, src = _dispatch_plan(bucket, rank, counts, n_tok)
        xcur = _ffn(xe, n_tok, tile_ea, tile_eb, tile_nvalid, n_used, src, row(norm_ffn_w[l]),
                    row(final_norm_w), w_gate[l].astype(BF16), w_up[l].astype(BF16),
                    w_down[l].astype(BF16), final=(l == depth - 1))
    return xcur.reshape(bsz, seq, D_MODEL)
```

```python
import functools

import jax
import jax.numpy as jnp
from jax import lax
from jax.experimental import pallas as pl
from jax.experimental.pallas import tpu as pltpu

F32 = jnp.float32
BF16 = jnp.bfloat16

D_MODEL = 1024
SSD_HEADS = 8
SSD_HEAD_DIM = 64
SSD_WIDTH = SSD_HEADS * SSD_HEAD_DIM
SSD_GROUPS = 2
SSD_STATE = 128
CONV_WIDTH = 4
XBC_WIDTH = SSD_WIDTH + 2 * SSD_GROUPS * SSD_STATE
HGRN_HEADS = 4
HGRN_HEAD_DIM = 128
HGRN_WIDTH = HGRN_HEADS * HGRN_HEAD_DIM
HGRN_BLOCK = 32
N_EXPERT_GROUPS = 4
EXPERTS_PER_GROUP = 4
N_EXPERTS = N_EXPERT_GROUPS * EXPERTS_PER_GROUP
EXPERT_DIM = 256
EPS = 1e-6

LANES = 128
N_PAIRS = 6
N_BUCKETS = N_EXPERT_GROUPS * N_PAIRS
ROUTE_LANES = LANES
SLAB = D_MODEL // LANES + 1
NEG_BIG = -1e30
LOG2E = 1.4426950408889634

TM_PROJ = 512
TS_MIX = 256
SSD_SUB = 128
HGRN_SUB = 128
CONV_HALO = 16
TM_FFN = 256
DMA_UNROLL = 8
VMEM_LIMIT = 56 * 1024 * 1024


def _mm(a, b):
    return jnp.dot(a.astype(BF16), b.astype(BF16), preferred_element_type=F32)


def _mm_nt(a, b):
    return lax.dot_general(a.astype(BF16), b.astype(BF16), (((1,), (1,)), ((), ())),
                           preferred_element_type=F32)


def _mm_tn(a, b):
    return lax.dot_general(a.astype(BF16), b.astype(BF16), (((0,), (0,)), ((), ())),
                           preferred_element_type=F32)


def _split3(x):
    p1 = x.astype(BF16)
    r1 = x - p1.astype(F32)
    p2 = r1.astype(BF16)
    p3 = (r1 - p2.astype(F32)).astype(BF16)
    return p1, p2, p3


def _cumsum_mm(tri, x):
    p1, p2, p3 = _split3(x)
    acc = jnp.dot(tri, p1, preferred_element_type=F32)
    acc = acc + jnp.dot(tri, p2, preferred_element_type=F32)
    return acc + jnp.dot(tri, p3, preferred_element_type=F32)


def _sigmoid(x):
    return 0.5 * jnp.tanh(0.5 * x) + 0.5


def _silu(x):
    hx = 0.5 * x
    return hx + hx * jnp.tanh(hx)


def _rms(x, w):
    return x * lax.rsqrt(jnp.mean(x * x, axis=-1, keepdims=True) + EPS) * w


def _slab_load(ref, rows):
    return jnp.concatenate([ref[pl.ds(c, rows, stride=SLAB), :] for c in range(D_MODEL // LANES)],
                           axis=1)


def _x_spec(rows, slab, index):
    if slab:
        return pl.BlockSpec((rows * SLAB, LANES), index)
    return pl.BlockSpec((rows, D_MODEL), index)


def _inproj_kernel(x_ref, nw_ref, wz_ref, wxbc_ref, wq_ref, wf_ref, wi_ref, wg_ref, wdt_ref,
                   z_ref, xbc_ref, q_ref, f_ref, i_ref, g_ref, dt_ref, *, slab):
    x = _slab_load(x_ref, z_ref.shape[0]) if slab else x_ref[...]
    h = _rms(x, nw_ref[...]).astype(BF16)
    for w_ref, o_ref in ((wz_ref, z_ref), (wxbc_ref, xbc_ref), (wq_ref, q_ref),
                         (wf_ref, f_ref), (wi_ref, i_ref), (wg_ref, g_ref), (wdt_ref, dt_ref)):
        o_ref[...] = jnp.dot(h, w_ref[...], preferred_element_type=F32).astype(o_ref.dtype)


def _inproj(x, n_tok, slab, nw, wz, wxbc, wq, wf, wi, wg, wdt):
    tm = TM_PROJ
    row = lambda i: (i, 0)
    fixed = lambda i: (0, 0)
    widths = (SSD_WIDTH, XBC_WIDTH, HGRN_WIDTH, HGRN_WIDTH, HGRN_WIDTH, HGRN_WIDTH, LANES)
    dtypes = (BF16,) * 6 + (F32,)
    return pl.pallas_call(
        functools.partial(_inproj_kernel, slab=slab),
        grid=(n_tok // tm,),
        in_specs=[_x_spec(tm, slab, row), pl.BlockSpec((1, D_MODEL), fixed)]
        + [pl.BlockSpec((D_MODEL, w), fixed) for w in widths],
        out_specs=[pl.BlockSpec((tm, w), row) for w in widths],
        out_shape=[jax.ShapeDtypeStruct((n_tok, w), dt) for w, dt in zip(widths, dtypes)],
        compiler_params=pltpu.CompilerParams(dimension_semantics=("arbitrary",),
                                             vmem_limit_bytes=VMEM_LIMIT),
        name="inproj",
    )(x, nw, wz, wxbc, wq, wf, wi, wg, wdt)


def _ssd_kernel(z_ref, xbc_ref, dt_ref, cw_ref, cb_ref, dtb_ref, alog_ref, dsk_ref, nw_ref, ex_ref,
                y_ref, ext_ref, h_ref):
    ts = z_ref.shape[0]
    L = SSD_SUB
    hpg = SSD_HEADS // SSD_GROUPS
    gw = SSD_WIDTH // SSD_GROUPS
    nh = SSD_HEADS

    pre = CONV_HALO

    @pl.when(pl.program_id(1) == 0)
    def _():
        ext_ref[0:pre, :] = jnp.zeros((pre, XBC_WIDTH), BF16)
        h_ref[...] = jnp.zeros_like(h_ref)

    ext_ref[pre:pre + ts, :] = xbc_ref[...]

    r_i = lax.broadcasted_iota(jnp.int32, (L, L), 0)
    c_i = lax.broadcasted_iota(jnp.int32, (L, L), 1)
    causal = r_i >= c_i
    triu = (r_i <= c_i).astype(BF16)
    even = lax.broadcasted_iota(jnp.int32, (L, LANES), 1) < SSD_HEAD_DIM
    sr_i = lax.broadcasted_iota(jnp.int32, (L, L + pre), 0)
    sc_i = lax.broadcasted_iota(jnp.int32, (L, L + pre), 1)
    shifts = [(sc_i == sr_i + (pre - (CONV_WIDTH - 1 - k))).astype(BF16)
              for k in range(CONV_WIDTH - 1)]

    lane_tile = lambda v: jnp.concatenate([v] * (ts // LANES), axis=1)
    dt_raw = dt_ref[...].T[0:nh, :] + lane_tile(dtb_ref[...])
    dt_all = jnp.maximum(dt_raw, 0.0) + jnp.log(1.0 + jnp.exp(-jnp.abs(dt_raw)))
    da_all = dt_all * lane_tile(-jnp.exp(alog_ref[...]) * LOG2E)
    ldt_all = jnp.log2(dt_all)

    for c in range(ts // L):
        r0 = c * L
        window = ext_ref[r0:r0 + L + pre, :]
        conv = cb_ref[...] + cw_ref[CONV_WIDTH - 1:CONV_WIDTH, :] * ext_ref[r0 + pre:r0 + pre + L, :].astype(F32)
        for k in range(CONV_WIDTH - 1):
            conv = conv + cw_ref[k:k + 1, :] * jnp.dot(shifts[k], window, preferred_element_type=F32)
        conv = _silu(conv)
        xs = conv[:, :SSD_WIDTH]
        bm = conv[:, SSD_WIDTH:SSD_WIDTH + SSD_GROUPS * SSD_STATE]
        cm = conv[:, SSD_WIDTH + SSD_GROUPS * SSD_STATE:]

        dt = dt_all[:, r0:r0 + L]
        p1, p2, p3 = _split3(da_all[:, r0:r0 + L])
        cum = (jnp.dot(p1, triu, preferred_element_type=F32) + jnp.dot(p2, triu, preferred_element_type=F32)
               + jnp.dot(p3, triu, preferred_element_type=F32))
        cum_last = cum[:, L - 1:L]
        key_row = cum - ldt_all[:, r0:r0 + L]
        rows = jnp.concatenate([cum, jnp.exp2(cum), jnp.exp2(cum_last - cum) * dt,
                                jnp.zeros((LANES - 3 * nh, L), F32)], axis=0)
        cols = rows.T
        hi = cols.astype(BF16)
        lo = (cols - hi.astype(F32)).astype(BF16)
        spread = (jnp.dot(hi, ex_ref[...], preferred_element_type=F32)
                  + jnp.dot(lo, ex_ref[...], preferred_element_type=F32))
        e_cum = spread[:, :SSD_WIDTH]
        xsc = xs * spread[:, SSD_WIDTH:]
        e_last = e_cum[L - 1:L, :]

        zc = z_ref[r0:r0 + L, :].astype(F32)
        for g in range(SSD_GROUPS):
            gs = slice(g * gw, (g + 1) * gw)
            bm_g = bm[:, g * SSD_STATE:(g + 1) * SSD_STATE].astype(BF16)
            cm_g = cm[:, g * SSD_STATE:(g + 1) * SSD_STATE].astype(BF16)
            cb = _mm_nt(cm_g, bm_g)
            h_t = h_ref[g]
            y_parts = []
            for p in range(hpg // 2):
                h0 = g * hpg + 2 * p
                xp_b = xs[:, h0 * SSD_HEAD_DIM:(h0 + 2) * SSD_HEAD_DIM].astype(BF16)
                y_heads = []
                for hh in (h0, h0 + 1):
                    seg = cols[:, hh:hh + 1] - key_row[hh:hh + 1, :]
                    sc = cb * jnp.exp2(jnp.where(causal, seg, NEG_BIG))
                    y_heads.append(jnp.dot(sc.astype(BF16), xp_b, preferred_element_type=F32))
                y_parts.append(jnp.where(even, y_heads[0], y_heads[1]))
            yg = (jnp.concatenate(y_parts, axis=1) + _mm(cm_g, h_t) * e_cum[:, gs]
                  + dsk_ref[:, gs] * xs[:, gs])
            h_ref[g] = h_t * e_last[:, gs] + _mm_tn(bm_g, xsc[:, gs])
            yg = _rms(yg * _silu(zc[:, gs]), nw_ref[:, gs])
            y_ref[r0:r0 + L, gs] = yg.astype(y_ref.dtype)

    ext_ref[0:pre, :] = ext_ref[ts:ts + pre, :]


def _head_spread_matrix():
    col = jnp.arange(LANES)[:, None]
    out = jnp.arange(2 * SSD_WIDTH)[None, :]
    want = SSD_HEADS * (1 + out // SSD_WIDTH) + (out % SSD_WIDTH) // SSD_HEAD_DIM
    return (col == want).astype(BF16)


def _ssd(z, xbc, dt, bsz, seq, cw, cb, dtb, alog, dsk, nw):
    ts = TS_MIX
    nst = seq // ts
    row = lambda b, s: (b * nst + s, 0)
    fixed = lambda b, s: (0, 0)
    per_head_rows = lambda v: jnp.broadcast_to(v.astype(F32)[:, None], (SSD_HEADS, LANES))
    return pl.pallas_call(
        _ssd_kernel,
        grid=(bsz, nst),
        in_specs=[pl.BlockSpec((ts, SSD_WIDTH), row), pl.BlockSpec((ts, XBC_WIDTH), row),
                  pl.BlockSpec((ts, LANES), row),
                  pl.BlockSpec((CONV_WIDTH, XBC_WIDTH), fixed), pl.BlockSpec((1, XBC_WIDTH), fixed),
                  pl.BlockSpec((SSD_HEADS, LANES), fixed), pl.BlockSpec((SSD_HEADS, LANES), fixed),
                  pl.BlockSpec((1, SSD_WIDTH), fixed), pl.BlockSpec((1, SSD_WIDTH), fixed),
                  pl.BlockSpec((LANES, 2 * SSD_WIDTH), fixed)],
        out_specs=pl.BlockSpec((ts, SSD_WIDTH), row),
        out_shape=jax.ShapeDtypeStruct((bsz * seq, SSD_WIDTH), BF16),
        scratch_shapes=[pltpu.VMEM((ts + CONV_HALO, XBC_WIDTH), BF16),
                        pltpu.VMEM((SSD_GROUPS, SSD_STATE, SSD_WIDTH // SSD_GROUPS), F32)],
        compiler_params=pltpu.CompilerParams(dimension_semantics=("arbitrary", "arbitrary"),
                                             vmem_limit_bytes=VMEM_LIMIT),
        name="ssd",
    )(z, xbc, dt, cw, cb, per_head_rows(dtb), per_head_rows(alog), dsk, nw, _head_spread_matrix())


def _hgrn_kernel(q_ref, f_ref, i_ref, g_ref, lb_ref, nw_ref, o_ref, st_ref):
    ts = q_ref.shape[0]
    blk = HGRN_BLOCK
    nb = ts // blk
    hd = HGRN_HEAD_DIM

    @pl.when(pl.program_id(1) == 0)
    def _():
        st_ref[...] = jnp.zeros_like(st_ref)

    sub = HGRN_SUB
    r_i = lax.broadcasted_iota(jnp.int32, (sub, sub), 0)
    c_i = lax.broadcasted_iota(jnp.int32, (sub, sub), 1)
    blk_causal = (r_i >= c_i) & ((r_i // blk) == (c_i // blk))
    row_in_blk = lax.broadcasted_iota(jnp.int32, (ts, hd), 0) % blk

    heads = []
    for h in range(HGRN_HEADS):
        sl = slice(h * hd, (h + 1) * hd)
        lb = lb_ref[:, sl]
        forget = lb + (1.0 - lb) * _sigmoid(f_ref[:, sl].astype(F32))
        kk = 1.0 - forget
        cum = jnp.log(forget)
        shift = 1
        while shift < blk:
            cum = cum + jnp.where(row_in_blk >= shift, pltpu.roll(cum, shift, axis=0), 0.0)
            shift *= 2
        cum = cum * LOG2E
        cum3 = cum.reshape(nb, blk, hd)
        b_mid = cum3[:, blk // 2:blk // 2 + 1, :]
        b_end = cum3[:, blk - 1:blk, :]
        rel = (cum3 - b_mid).reshape(ts, hd)
        to_end = (b_end - cum3).reshape(ts, hd)
        q = q_ref[:, sl].astype(F32)
        v = i_ref[:, sl].astype(BF16)
        qs = (q * jnp.exp2(rel)).astype(BF16)
        ks = (kk * jnp.exp2(-rel)).astype(BF16)
        o_parts = []
        for c in range(ts // sub):
            cs = slice(c * sub, (c + 1) * sub)
            sc = jnp.where(blk_causal, _mm_nt(qs[cs], ks[cs]), 0.0)
            o_parts.append(jnp.dot(sc.astype(BF16), v[cs], preferred_element_type=F32))
        heads.append(dict(sl=sl, v=v, o_intra=o_parts,
                          q_dec=(q * jnp.exp2(cum)).astype(BF16),
                          k_end=(kk * jnp.exp2(to_end)).astype(BF16),
                          dec=jnp.exp2(b_end), st=st_ref[h], o_inter=[]))

    for n in range(nb):
        rs = slice(n * blk, (n + 1) * blk)
        for hh in heads:
            hh["o_inter"].append(_mm_nt(hh["q_dec"][rs], hh["st"]))
            hh["st"] = hh["st"] * hh["dec"][n] + _mm_tn(hh["v"][rs], hh["k_end"][rs])

    for h, hh in enumerate(heads):
        sl = hh["sl"]
        st_ref[h] = hh["st"]
        o = jnp.concatenate(hh["o_intra"], axis=0) + jnp.concatenate(hh["o_inter"], axis=0)
        o = _rms(o, nw_ref[:, sl]) * _silu(g_ref[:, sl].astype(F32))
        o_ref[:, sl] = o.astype(o_ref.dtype)


def _hgrn(q, f, i, g, bsz, seq, lb, nw):
    ts = TS_MIX
    nst = seq // ts
    row = lambda b, s: (b * nst + s, 0)
    fixed = lambda b, s: (0, 0)
    return pl.pallas_call(
        _hgrn_kernel,
        grid=(bsz, nst),
        in_specs=[pl.BlockSpec((ts, HGRN_WIDTH), row)] * 4
        + [pl.BlockSpec((1, HGRN_WIDTH), fixed)] * 2,
        out_specs=pl.BlockSpec((ts, HGRN_WIDTH), row),
        out_shape=jax.ShapeDtypeStruct((bsz * seq, HGRN_WIDTH), BF16),
        scratch_shapes=[pltpu.VMEM((HGRN_HEADS, HGRN_HEAD_DIM, HGRN_HEAD_DIM), F32)],
        compiler_params=pltpu.CompilerParams(dimension_semantics=("arbitrary", "arbitrary"),
                                             vmem_limit_bytes=VMEM_LIMIT),
        name="hgrn",
    )(q, f, i, g, lb, nw)


def _first_argmax4(v):
    m = jnp.maximum(jnp.maximum(v[0], v[1]), jnp.maximum(v[2], v[3]))
    idx = jnp.where(v[0] == m, 0.0, jnp.where(v[1] == m, 1.0, jnp.where(v[2] == m, 2.0, 3.0)))
    return m, idx


def _outproj_kernel(ys_ref, yh_ref, x_ref, wo1_ref, wo2_ref, nw_ref, wr_ref, xe_ref, rt_ref, cnt_ref,
                    base_ref, *, slab):
    tm = ys_ref.shape[0]

    @pl.when(pl.program_id(0) == 0)
    def _():
        base_ref[...] = jnp.zeros_like(base_ref)

    x = _slab_load(x_ref, tm) if slab else x_ref[...]
    xm = (x + jnp.dot(ys_ref[...], wo1_ref[...], preferred_element_type=F32)
          + jnp.dot(yh_ref[...], wo2_ref[...], preferred_element_type=F32))
    h = _rms(xm, nw_ref[...])
    lt = _mm_nt(wr_ref[...], h)
    row = lambda r: lt[r:r + 1, :]
    gl = [row(j) for j in range(N_EXPERT_GROUPS)]
    g_max, g_idx = _first_argmax4(gl)
    g_w = 1.0 / (jnp.exp(gl[0] - g_max) + jnp.exp(gl[1] - g_max)
                 + jnp.exp(gl[2] - g_max) + jnp.exp(gl[3] - g_max))
    el = []
    for j in range(EXPERTS_PER_GROUP):
        base = N_EXPERT_GROUPS + j
        el.append(jnp.where(g_idx == 0.0, row(base),
                  jnp.where(g_idx == 1.0, row(base + EXPERTS_PER_GROUP),
                  jnp.where(g_idx == 2.0, row(base + 2 * EXPERTS_PER_GROUP),
                            row(base + 3 * EXPERTS_PER_GROUP)))))
    e1, i1 = _first_argmax4(el)
    el2 = [jnp.where(i1 == float(j), NEG_BIG, el[j]) for j in range(EXPERTS_PER_GROUP)]
    e2, i2 = _first_argmax4(el2)
    w1 = 1.0 / (1.0 + jnp.exp(e2 - e1))
    w2 = jnp.exp(e2 - e1) * w1
    lo = jnp.minimum(i1, i2)
    hi = jnp.maximum(i1, i2)
    w_lo = g_w * jnp.where(i1 < i2, w1, w2)
    w_hi = g_w * jnp.where(i1 < i2, w2, w1)
    pair = lo * (7.0 - lo) * 0.5 + (hi - lo - 1.0)
    bucket = g_idx * float(N_PAIRS) + pair
    sub_l = lax.broadcasted_iota(jnp.int32, (ROUTE_LANES, tm), 0)
    onehot = jnp.where(sub_l.astype(F32) == bucket, 1.0, 0.0)
    r_i = lax.broadcasted_iota(jnp.int32, (tm, tm), 0)
    c_i = lax.broadcasted_iota(jnp.int32, (tm, tm), 1)
    before = jnp.dot(onehot.astype(BF16), (r_i < c_i).astype(BF16), preferred_element_type=F32)
    base = base_ref[...]
    before = before + jnp.concatenate([base] * (tm // LANES), axis=1)
    rank = jnp.sum(onehot * before, axis=0, keepdims=True)
    base = base + jnp.sum(onehot, axis=1, keepdims=True)
    base_ref[...] = base
    cnt_ref[...] = base
    sub = lax.broadcasted_iota(jnp.int32, (8, tm), 0)
    slab = jnp.where(sub == 0, bucket, jnp.where(sub == 1, w_lo, jnp.where(sub == 2, w_hi,
                     jnp.where(sub == 3, rank, 0.0))))
    rt_ref[0] = slab
    wide = jnp.where(sub_l == 0, bucket, jnp.where(sub_l == 1, w_lo, jnp.where(sub_l == 2, w_hi, 0.0)))
    for c in range(D_MODEL // LANES):
        xe_ref[pl.ds(c, tm, stride=SLAB), :] = xm[:, c * LANES:(c + 1) * LANES]
    xe_ref[pl.ds(D_MODEL // LANES, tm, stride=SLAB), :] = wide.T


def _outproj(ys, yh, x, n_tok, slab, wo1, wo2, nw, wr_t):
    tm = TM_PROJ
    row = lambda i: (i, 0)
    fixed = lambda i: (0, 0)
    return pl.pallas_call(
        functools.partial(_outproj_kernel, slab=slab),
        grid=(n_tok // tm,),
        in_specs=[pl.BlockSpec((tm, SSD_WIDTH), row), pl.BlockSpec((tm, HGRN_WIDTH), row),
                  _x_spec(tm, slab, row),
                  pl.BlockSpec((SSD_WIDTH, D_MODEL), fixed), pl.BlockSpec((HGRN_WIDTH, D_MODEL), fixed),
                  pl.BlockSpec((1, D_MODEL), fixed), pl.BlockSpec((LANES, D_MODEL), fixed)],
        out_specs=[pl.BlockSpec((tm * SLAB, LANES), row), pl.BlockSpec((1, 8, tm), lambda i: (i, 0, 0)),
                   pl.BlockSpec((ROUTE_LANES, LANES), fixed)],
        out_shape=[jax.ShapeDtypeStruct((n_tok * SLAB, LANES), F32),
                   jax.ShapeDtypeStruct((n_tok // tm, 8, tm), F32),
                   jax.ShapeDtypeStruct((ROUTE_LANES, LANES), F32)],
        scratch_shapes=[pltpu.VMEM((ROUTE_LANES, LANES), F32)],
        compiler_params=pltpu.CompilerParams(dimension_semantics=("arbitrary",),
                                             vmem_limit_bytes=VMEM_LIMIT),
        name="outproj_router",
    )(ys, yh, x, wo1, wo2, nw, wr_t)


def _ffn_kernel(ea_ref, eb_ref, nvalid_ref, nused_ref,
                src0_ref, src1_ref, xe_hbm, nw_ref, fw_ref,
                wga_ref, wgb_ref, wua_ref, wub_ref, wda_ref, wdb_ref,
                out_hbm, xbuf0, xbuf1, obuf0, obuf1, gsem, ssem, *, final):
    del ea_ref, eb_ref
    tm = TM_FFN
    j = pl.program_id(0)
    nt = pl.num_programs(0)
    n_used = nused_ref[0]
    xbufs = (xbuf0, xbuf1)
    obufs = (obuf0, obuf1)
    orow = 1 if final else SLAB

    def issue_gather(src_ref, p):
        for r in range(tm):
            pltpu.make_async_copy(xe_hbm.at[pl.ds(src_ref[0, 0, r] * SLAB, SLAB)],
                                  xbufs[p].at[pl.ds(r * SLAB, SLAB)], gsem.at[p]).start()

    def wait_gather(p):
        pltpu.make_async_copy(xe_hbm.at[pl.ds(0, tm * SLAB)], xbufs[p], gsem.at[p]).wait()

    def scatter_copy(r, p):
        return pltpu.make_async_copy(obufs[p].at[pl.ds(r * orow, orow)],
                                     out_hbm.at[pl.ds(src0_ref[0, 0, r] * orow, orow)], ssem.at[p])

    def wait_scatter(p, tile):
        n = nvalid_ref[tile]
        n8 = pl.multiple_of((n // 8) * 8, 8)

        @pl.when(n8 > 0)
        def _():
            rows = pl.multiple_of(n8 * orow, 8)
            pltpu.make_async_copy(obufs[p].at[pl.ds(0, rows)], out_hbm.at[pl.ds(0, rows)],
                                  ssem.at[p]).wait()

        def body(r, carry):
            pltpu.make_async_copy(obufs[p].at[pl.ds(0, orow)], out_hbm.at[pl.ds(0, orow)],
                                  ssem.at[p]).wait()
            return carry
        lax.fori_loop(0, n - n8, body, 0)

    @pl.when((j == 0) & (n_used > 0))
    def _():
        issue_gather(src0_ref, 0)

    def step(p):
        @pl.when(j >= 2)
        def _():
            wait_scatter(p, j - 2)

        wait_gather(p)
        issue_gather(src1_ref, 1 - p)
        xm = _slab_load(xbufs[p], tm)
        route = xbufs[p][pl.ds(D_MODEL // LANES, tm, stride=SLAB), :]
        w_lo = route[:, 1:2]
        w_hi = route[:, 2:3]
        h = _rms(xm, nw_ref[...]).astype(BF16)
        hid_a = _silu(jnp.dot(h, wga_ref[0], preferred_element_type=F32)) \
            * jnp.dot(h, wua_ref[0], preferred_element_type=F32)
        hid_b = _silu(jnp.dot(h, wgb_ref[0], preferred_element_type=F32)) \
            * jnp.dot(h, wub_ref[0], preferred_element_type=F32)
        y = (jnp.dot((hid_a * w_lo).astype(BF16), wda_ref[0], preferred_element_type=F32)
             + jnp.dot((hid_b * w_hi).astype(BF16), wdb_ref[0], preferred_element_type=F32))
        out = xm + y
        if final:
            obufs[p][...] = _rms(out, fw_ref[...])
        else:
            for c in range(D_MODEL // LANES):
                obufs[p][pl.ds(c, tm, stride=SLAB), :] = out[:, c * LANES:(c + 1) * LANES]
            obufs[p][pl.ds(D_MODEL // LANES, tm, stride=SLAB), :] = route

        n_valid = nvalid_ref[j]

        @pl.when(n_valid == tm)
        def _():
            for r in range(tm):
                scatter_copy(r, p).start()

        @pl.when(n_valid < tm)
        def _():
            def single(r, carry):
                scatter_copy(r, p).start()
                return carry
            lax.fori_loop(0, n_valid, single, 0)

    for p in (0, 1):
        @pl.when((j % 2 == p) & (j < n_used))
        def _():
            step(p)

    @pl.when(j == nt - 1)
    def _():
        for p in (0, 1):
            @pl.when((n_used >= 1) & (n_used % 2 == p))
            def _():
                wait_gather(p)

            @pl.when((n_used >= 1) & ((n_used - 1) % 2 == p))
            def _():
                wait_scatter(p, n_used - 1)

            @pl.when((n_used >= 2) & ((n_used - 2) % 2 == p))
            def _():
                wait_scatter(p, n_used - 2)


def _ffn(xe, n_tok, tile_ea, tile_eb, tile_nvalid, n_used, src, nw, fw, wg, wu, wd, final):
    tm = TM_FFN
    nt = src.shape[0]
    obuf_shape = (tm, D_MODEL) if final else (tm * SLAB, LANES)
    out_rows = (n_tok, D_MODEL) if final else (n_tok * SLAB, LANES)
    clamp = lambda j, nu: jnp.minimum(j, jnp.maximum(nu[0] - 1, 0))
    wa = lambda j, ea, eb, nv, nu: (ea[clamp(j, nu)], 0, 0)
    wb = lambda j, ea, eb, nv, nu: (eb[clamp(j, nu)], 0, 0)
    fixed = lambda j, ea, eb, nv, nu: (0, 0)
    smem_blk = lambda f: pl.BlockSpec((1, 1, tm), f, memory_space=pltpu.SMEM)
    grid_spec = pltpu.PrefetchScalarGridSpec(
        num_scalar_prefetch=4,
        grid=(nt,),
        in_specs=[smem_blk(lambda j, ea, eb, nv, nu: (j, 0, 0)),
                  smem_blk(lambda j, ea, eb, nv, nu: (jnp.minimum(j + 1, nt - 1), 0, 0)),
                  pl.BlockSpec(memory_space=pl.ANY),
                  pl.BlockSpec((1, D_MODEL), fixed), pl.BlockSpec((1, D_MODEL), fixed),
                  pl.BlockSpec((1, D_MODEL, EXPERT_DIM), wa), pl.BlockSpec((1, D_MODEL, EXPERT_DIM), wb),
                  pl.BlockSpec((1, D_MODEL, EXPERT_DIM), wa), pl.BlockSpec((1, D_MODEL, EXPERT_DIM), wb),
                  pl.BlockSpec((1, EXPERT_DIM, D_MODEL), wa), pl.BlockSpec((1, EXPERT_DIM, D_MODEL), wb)],
        out_specs=pl.BlockSpec(memory_space=pl.ANY),
        scratch_shapes=[pltpu.VMEM((tm * SLAB, LANES), F32), pltpu.VMEM((tm * SLAB, LANES), F32),
                        pltpu.VMEM(obuf_shape, F32), pltpu.VMEM(obuf_shape, F32),
                        pltpu.SemaphoreType.DMA((2,)), pltpu.SemaphoreType.DMA((2,))],
    )
    return pl.pallas_call(
        functools.partial(_ffn_kernel, final=final),
        grid_spec=grid_spec,
        out_shape=jax.ShapeDtypeStruct(out_rows, F32),
        compiler_params=pltpu.CompilerParams(dimension_semantics=("arbitrary",),
                                             vmem_limit_bytes=VMEM_LIMIT),
        name="moe_ffn",
    )(tile_ea, tile_eb, tile_nvalid, n_used, src, src, xe, nw, fw, wg, wg, wu, wu, wd, wd)


def _pair_tables():
    lo, hi = [], []
    for a in range(EXPERTS_PER_GROUP):
        for b in range(a + 1, EXPERTS_PER_GROUP):
            lo.append(a)
            hi.append(b)
    return lo, hi


def _dispatch_plan(bucket, rank, counts, n_tok):
    tm = TM_FFN
    nt = n_tok // tm + N_BUCKETS
    i32 = jnp.int32
    tiles_b = (counts + tm - 1) // tm
    tile_end = jnp.cumsum(tiles_b)
    tile_start = tile_end - tiles_b
    n_used = tile_end[-1:]
    pos = tile_start[bucket] * tm + rank
    tile_id = jnp.arange(nt, dtype=i32)
    tile_bucket = jnp.sum((tile_end[None, :] <= tile_id[:, None]).astype(i32), axis=1)
    tile_bucket = jnp.minimum(tile_bucket, N_BUCKETS - 1)
    tile_nvalid = jnp.clip(counts[tile_bucket] - (tile_id - tile_start[tile_bucket]) * tm, 0, tm)
    tile_nvalid = jnp.where(tile_id < n_used[0], tile_nvalid, 0)
    src = jnp.zeros((nt * tm,), i32).at[pos].set(jnp.arange(n_tok, dtype=i32), unique_indices=True)
    lo, hi = _pair_tables()
    grp = tile_bucket // N_PAIRS
    pr = tile_bucket % N_PAIRS
    tile_ea = grp * EXPERTS_PER_GROUP + jnp.asarray(lo, i32)[pr]
    tile_eb = grp * EXPERTS_PER_GROUP + jnp.asarray(hi, i32)[pr]
    return (tile_ea, tile_eb, tile_nvalid.astype(i32), n_used.astype(i32), src.reshape(nt, 1, tm))


def kernel(x, norm_mix_w, w_in, conv_w, conv_b, dt_bias, a_log, d_skip, ssd_norm_w, hgrn_lb_logits,
           hgrn_norm_w, w_out, norm_ffn_w, router_group, router_expert, w_gate, w_up, w_down,
           final_norm_w):
    bsz, seq, _ = x.shape
    depth = w_in.shape[0]
    n_tok = bsz * seq
    assert n_tok % TM_PROJ == 0 and seq % TS_MIX == 0 and n_tok % TM_FFN == 0

    p = jax.nn.softmax(hgrn_lb_logits.astype(F32), axis=0)
    lower_bounds = jnp.cumsum(p, axis=0) - p[0:1]

    o_xbc = SSD_WIDTH
    o_dt = o_xbc + XBC_WIDTH
    o_q = o_dt + SSD_HEADS
    pad_h = lambda v: jnp.pad(v.astype(F32), (0, LANES - SSD_HEADS)).reshape(1, LANES)
    row = lambda v: v.astype(F32).reshape(1, -1)

    xcur = x.reshape(n_tok, D_MODEL)
    for l in range(depth):
        wl = w_in[l].astype(BF16)
        wq, wf, wi, wg = (wl[:, o_q + k * HGRN_WIDTH:o_q + (k + 1) * HGRN_WIDTH] for k in range(4))
        wdt = jnp.pad(wl[:, o_dt:o_q], ((0, 0), (0, LANES - SSD_HEADS)))
        z, xbc, q, f, i, g, dt = _inproj(xcur, n_tok, l > 0, row(norm_mix_w[l]), wl[:, :o_xbc],
                                         wl[:, o_xbc:o_dt], wq, wf, wi, wg, wdt)
        y_ssd = _ssd(z, xbc, dt, bsz, seq, conv_w[l].astype(F32), row(conv_b[l]), dt_bias[l],
                     a_log[l], row(jnp.repeat(d_skip[l], SSD_HEAD_DIM)), row(ssd_norm_w[l]))
        y_hgrn = _hgrn(q, f, i, g, bsz, seq, row(lower_bounds[l]), row(hgrn_norm_w[l]))
        wo = w_out[l].astype(BF16)
        wr_t = jnp.concatenate([router_group[l], router_expert[l]], axis=1).T
        wr_t = jnp.pad(wr_t, ((0, LANES - wr_t.shape[0]), (0, 0))).astype(BF16)
        xe, route, cnt = _outproj(y_ssd, y_hgrn, xcur, n_tok, l > 0, wo[:SSD_WIDTH], wo[SSD_WIDTH:],
                                  row(norm_ffn_w[l]), wr_t)
        bucket = route[:, 0, :].reshape(n_tok).astype(jnp.int32)
        rank = route[:, 3, :].reshape(n_tok).astype(jnp.int32)
        counts = cnt[:N_BUCKETS, 0].astype(jnp.int32)
        tile_ea, tile_eb, tile_nvalid, n_used, src = _dispatch_plan(bucket, rank, counts, n_tok)
        xcur = _ffn(xe, n_tok, tile_ea, tile_eb, tile_nvalid, n_used, src, row(norm_ffn_w[l]),
                    row(final_norm_w), w_gate[l].astype(BF16), w_up[l].astype(BF16),
                    w_down[l].astype(BF16), final=(l == depth - 1))
    return xcur.reshape(bsz, seq, D_MODEL)
```

```python
import functools

import jax
import jax.numpy as jnp
from jax import lax
from jax.experimental import pallas as pl
from jax.experimental.pallas import tpu as pltpu

F32 = jnp.float32
BF16 = jnp.bfloat16

D_MODEL = 1024
SSD_HEADS = 8
SSD_HEAD_DIM = 64
SSD_WIDTH = SSD_HEADS * SSD_HEAD_DIM
SSD_GROUPS = 2
SSD_STATE = 128
CONV_WIDTH = 4
XBC_WIDTH = SSD_WIDTH + 2 * SSD_GROUPS * SSD_STATE
HGRN_HEADS = 4
HGRN_HEAD_DIM = 128
HGRN_WIDTH = HGRN_HEADS * HGRN_HEAD_DIM
HGRN_BLOCK = 32
N_EXPERT_GROUPS = 4
EXPERTS_PER_GROUP = 4
N_EXPERTS = N_EXPERT_GROUPS * EXPERTS_PER_GROUP
EXPERT_DIM = 256
EPS = 1e-6

LANES = 128
N_PAIRS = 6
N_BUCKETS = N_EXPERT_GROUPS * N_PAIRS
ROUTE_LANES = LANES
SLAB = D_MODEL // LANES + 1
NEG_BIG = -1e30
LOG2E = 1.4426950408889634

TM_PROJ = 512
TS_MIX = 256
MIX_BATCH = 2
SSD_SUB = 128
HGRN_SUB = 128
CONV_HALO = 8
TM_FFN = 256
DMA_UNROLL = 8
VMEM_LIMIT = 56 * 1024 * 1024


def _mm(a, b):
    return jnp.dot(a.astype(BF16), b.astype(BF16), preferred_element_type=F32)


def _mm_nt(a, b):
    return lax.dot_general(a.astype(BF16), b.astype(BF16), (((1,), (1,)), ((), ())),
                           preferred_element_type=F32)


def _mm_tn(a, b):
    return lax.dot_general(a.astype(BF16), b.astype(BF16), (((0,), (0,)), ((), ())),
                           preferred_element_type=F32)


def _split3(x):
    p1 = x.astype(BF16)
    r1 = x - p1.astype(F32)
    p2 = r1.astype(BF16)
    p3 = (r1 - p2.astype(F32)).astype(BF16)
    return p1, p2, p3


def _cumsum_mm(tri, x):
    p1, p2, p3 = _split3(x)
    acc = jnp.dot(tri, p1, preferred_element_type=F32)
    acc = acc + jnp.dot(tri, p2, preferred_element_type=F32)
    return acc + jnp.dot(tri, p3, preferred_element_type=F32)


def _sigmoid(x):
    return 0.5 * jnp.tanh(0.5 * x) + 0.5


def _silu(x):
    hx = 0.5 * x
    return hx + hx * jnp.tanh(hx)


def _rms(x, w):
    return x * lax.rsqrt(jnp.mean(x * x, axis=-1, keepdims=True) + EPS) * w


def _slab_load(ref, rows):
    return jnp.concatenate([ref[pl.ds(c, rows, stride=SLAB), :] for c in range(D_MODEL // LANES)],
                           axis=1)


def _x_spec(rows, slab, index):
    if slab:
        return pl.BlockSpec((rows * SLAB, LANES), index)
    return pl.BlockSpec((rows, D_MODEL), index)


def _inproj_kernel(x_ref, nw_ref, wz_ref, wxbc_ref, wq_ref, wf_ref, wi_ref, wg_ref, wdt_ref,
                   z_ref, xbc_ref, q_ref, f_ref, i_ref, g_ref, dt_ref, *, slab):
    x = _slab_load(x_ref, z_ref.shape[0]) if slab else x_ref[...]
    h = _rms(x, nw_ref[...]).astype(BF16)
    for w_ref, o_ref in ((wz_ref, z_ref), (wxbc_ref, xbc_ref), (wq_ref, q_ref),
                         (wf_ref, f_ref), (wi_ref, i_ref), (wg_ref, g_ref), (wdt_ref, dt_ref)):
        o_ref[...] = jnp.dot(h, w_ref[...], preferred_element_type=F32).astype(o_ref.dtype)


def _inproj(x, n_tok, slab, nw, wz, wxbc, wq, wf, wi, wg, wdt):
    tm = TM_PROJ
    row = lambda i: (i, 0)
    fixed = lambda i: (0, 0)
    widths = (SSD_WIDTH, XBC_WIDTH, HGRN_WIDTH, HGRN_WIDTH, HGRN_WIDTH, HGRN_WIDTH, LANES)
    dtypes = (BF16,) * 6 + (F32,)
    return pl.pallas_call(
        functools.partial(_inproj_kernel, slab=slab),
        grid=(n_tok // tm,),
        in_specs=[_x_spec(tm, slab, row), pl.BlockSpec((1, D_MODEL), fixed)]
        + [pl.BlockSpec((D_MODEL, w), fixed) for w in widths],
        out_specs=[pl.BlockSpec((tm, w), row) for w in widths],
        out_shape=[jax.ShapeDtypeStruct((n_tok, w), dt) for w, dt in zip(widths, dtypes)],
        compiler_params=pltpu.CompilerParams(dimension_semantics=("arbitrary",),
                                             vmem_limit_bytes=VMEM_LIMIT),
        name="inproj",
    )(x, nw, wz, wxbc, wq, wf, wi, wg, wdt)


def _ssd_kernel(z_ref, xbc_ref, dt_ref, cw_ref, cb_ref, dtb_ref, alog_ref, dsk_ref, nw_ref, ex_ref,
                y_ref, ext_ref, h_ref):
    nbat, ts, _ = z_ref.shape
    L = SSD_SUB
    hpg = SSD_HEADS // SSD_GROUPS
    gw = SSD_WIDTH // SSD_GROUPS
    nh = SSD_HEADS
    pre = CONV_HALO
    streams = [(bb, c) for bb in range(nbat) for c in range(ts // L)]

    nlc = XBC_WIDTH // LANES

    @pl.when(pl.program_id(1) == 0)
    def _():
        ext_ref[...] = jnp.zeros_like(ext_ref)
        h_ref[...] = jnp.zeros_like(h_ref)

    for bb in range(nbat):
        for j in range(nlc):
            ext_ref[bb, j, pl.ds(pre, ts, stride=2), :] = xbc_ref[bb, :, j * LANES:(j + 1) * LANES].astype(F32)

    r_i = lax.broadcasted_iota(jnp.int32, (L, L), 0)
    c_i = lax.broadcasted_iota(jnp.int32, (L, L), 1)
    causal = r_i >= c_i
    triu = (r_i <= c_i).astype(BF16)
    even = lax.broadcasted_iota(jnp.int32, (L, LANES), 1) < SSD_HEAD_DIM

    lane_tile = lambda v: jnp.concatenate([v] * (ts // LANES), axis=1)
    dt_all, da_all, ldt_all = {}, {}, {}
    for bb in range(nbat):
        dt_raw = dt_ref[bb].T[0:nh, :] + lane_tile(dtb_ref[...])
        dt_all[bb] = jnp.maximum(dt_raw, 0.0) + jnp.log(1.0 + jnp.exp(-jnp.abs(dt_raw)))
        da_all[bb] = dt_all[bb] * lane_tile(-jnp.exp(alog_ref[...]) * LOG2E)
        ldt_all[bb] = jnp.log2(dt_all[bb])

    conv = {}
    for bb, c in streams:
        r0 = c * L
        pieces = []
        for j in range(nlc):
            ls = slice(j * LANES, (j + 1) * LANES)
            acc = cb_ref[:, ls]
            for k in range(CONV_WIDTH):
                tap = ext_ref[bb, j, pl.ds(pre + 2 * (r0 - (CONV_WIDTH - 1 - k)), L, stride=2), :]
                acc = acc + cw_ref[k:k + 1, ls] * tap
            pieces.append(_silu(acc))
        conv[bb, c] = jnp.concatenate(pieces, axis=1)

    cols, key_row, e_cum, xsc = {}, {}, {}, {}
    for bb, c in streams:
        r0 = c * L
        dt = dt_all[bb][:, r0:r0 + L]
        p1, p2, p3 = _split3(da_all[bb][:, r0:r0 + L])
        cum = (jnp.dot(p1, triu, preferred_element_type=F32) + jnp.dot(p2, triu, preferred_element_type=F32)
               + jnp.dot(p3, triu, preferred_element_type=F32))
        cum_last = cum[:, L - 1:L]
        key_row[bb, c] = cum - ldt_all[bb][:, r0:r0 + L]
        rows = jnp.concatenate([cum, jnp.exp2(cum), jnp.exp2(cum_last - cum) * dt,
                                jnp.zeros((LANES - 3 * nh, L), F32)], axis=0)
        cols[bb, c] = rows.T
        hi = cols[bb, c].astype(BF16)
        lo = (cols[bb, c] - hi.astype(F32)).astype(BF16)
        spread = jnp.dot(jnp.concatenate([hi, lo], axis=1), ex_ref[...],
                         preferred_element_type=F32)
        e_cum[bb, c] = spread[:, :SSD_WIDTH]
        xsc[bb, c] = conv[bb, c][:, :SSD_WIDTH] * spread[:, SSD_WIDTH:]

    y_intra = {}
    for bb, c in streams:
        xs = conv[bb, c][:, :SSD_WIDTH]
        for g in range(SSD_GROUPS):
            lo_b = SSD_WIDTH + g * SSD_STATE
            lo_c = SSD_WIDTH + (SSD_GROUPS + g) * SSD_STATE
            cb = _mm_nt(conv[bb, c][:, lo_c:lo_c + SSD_STATE], conv[bb, c][:, lo_b:lo_b + SSD_STATE])
            y_parts = []
            for p in range(hpg // 2):
                h0 = g * hpg + 2 * p
                xp = xs[:, h0 * SSD_HEAD_DIM:(h0 + 2) * SSD_HEAD_DIM]
                scores = []
                for hh in (h0, h0 + 1):
                    seg = cols[bb, c][:, hh:hh + 1] - key_row[bb, c][hh:hh + 1, :]
                    scores.append((cb * jnp.exp2(jnp.where(causal, seg, NEG_BIG))).astype(BF16))
                rhs = jnp.concatenate([jnp.where(even, xp, 0.0), jnp.where(even, 0.0, xp)],
                                      axis=0).astype(BF16)
                y_parts.append(jnp.dot(jnp.concatenate(scores, axis=1), rhs,
                                       preferred_element_type=F32))
            y_intra[bb, c, g] = jnp.concatenate(y_parts, axis=1)

    for c in range(ts // L):
        r0 = c * L
        for bb in range(nbat):
            xs = conv[bb, c][:, :SSD_WIDTH]
            zc = z_ref[bb, r0:r0 + L, :].astype(F32)
            e_last = e_cum[bb, c][L - 1:L, :]
            for g in range(SSD_GROUPS):
                gs = slice(g * gw, (g + 1) * gw)
                lo_b = SSD_WIDTH + g * SSD_STATE
                lo_c = SSD_WIDTH + (SSD_GROUPS + g) * SSD_STATE
                h_t = h_ref[bb, g]
                yg = (y_intra[bb, c, g] + _mm(conv[bb, c][:, lo_c:lo_c + SSD_STATE], h_t) * e_cum[bb, c][:, gs]
                      + dsk_ref[:, gs] * xs[:, gs])
                h_ref[bb, g] = h_t * e_last[:, gs] + _mm_tn(conv[bb, c][:, lo_b:lo_b + SSD_STATE],
                                                            xsc[bb, c][:, gs])
                yg = _rms(yg * _silu(zc[:, gs]), nw_ref[:, gs])
                y_ref[bb, r0:r0 + L, gs] = yg.astype(y_ref.dtype)

    ext_ref[:, :, 0:pre, :] = ext_ref[:, :, 2 * ts:2 * ts + pre, :]


def _head_spread_matrix():
    col = jnp.arange(LANES)[:, None]
    out = jnp.arange(2 * SSD_WIDTH)[None, :]
    want = SSD_HEADS * (1 + out // SSD_WIDTH) + (out % SSD_WIDTH) // SSD_HEAD_DIM
    once = (col == want).astype(BF16)
    return jnp.concatenate([once, once], axis=0)


def _ssd(z, xbc, dt, bsz, seq, cw, cb, dtb, alog, dsk, nw):
    ts = TS_MIX
    nbat = MIX_BATCH
    tile = lambda b, s: (b, s, 0)
    fixed = lambda b, s: (0, 0)
    as3d = lambda v: v.reshape(bsz, seq, v.shape[-1])
    per_head_rows = lambda v: jnp.broadcast_to(v.astype(F32)[:, None], (SSD_HEADS, LANES))
    out = pl.pallas_call(
        _ssd_kernel,
        grid=(bsz // nbat, seq // ts),
        in_specs=[pl.BlockSpec((nbat, ts, SSD_WIDTH), tile), pl.BlockSpec((nbat, ts, XBC_WIDTH), tile),
                  pl.BlockSpec((nbat, ts, LANES), tile),
                  pl.BlockSpec((CONV_WIDTH, XBC_WIDTH), fixed), pl.BlockSpec((1, XBC_WIDTH), fixed),
                  pl.BlockSpec((SSD_HEADS, LANES), fixed), pl.BlockSpec((SSD_HEADS, LANES), fixed),
                  pl.BlockSpec((1, SSD_WIDTH), fixed), pl.BlockSpec((1, SSD_WIDTH), fixed),
                  pl.BlockSpec((2 * LANES, 2 * SSD_WIDTH), fixed)],
        out_specs=pl.BlockSpec((nbat, ts, SSD_WIDTH), tile),
        out_shape=jax.ShapeDtypeStruct((bsz, seq, SSD_WIDTH), BF16),
        scratch_shapes=[pltpu.VMEM((nbat, XBC_WIDTH // LANES, 2 * ts + CONV_HALO, LANES), F32),
                        pltpu.VMEM((nbat, SSD_GROUPS, SSD_STATE, SSD_WIDTH // SSD_GROUPS), F32)],
        compiler_params=pltpu.CompilerParams(dimension_semantics=("arbitrary", "arbitrary"),
                                             vmem_limit_bytes=VMEM_LIMIT),
        name="ssd",
    )(as3d(z), as3d(xbc), as3d(dt), cw, cb, per_head_rows(dtb), per_head_rows(alog), dsk, nw,
      _head_spread_matrix())
    return out.reshape(bsz * seq, SSD_WIDTH)


def _hgrn_body(q_ref, f_ref, i_ref, g_ref, lb_ref, nw_ref, o_ref, st_ref):
    ts = q_ref.shape[0]
    blk = HGRN_BLOCK
    nb = ts // blk
    hd = HGRN_HEAD_DIM

    @pl.when(pl.program_id(1) == 0)
    def _():
        st_ref[...] = jnp.zeros_like(st_ref)

    sub = HGRN_SUB
    r_i = lax.broadcasted_iota(jnp.int32, (sub, sub), 0)
    c_i = lax.broadcasted_iota(jnp.int32, (sub, sub), 1)
    blk_causal = (r_i >= c_i) & ((r_i // blk) == (c_i // blk))
    row_in_blk = lax.broadcasted_iota(jnp.int32, (ts, hd), 0) % blk

    heads = []
    for h in range(HGRN_HEADS):
        sl = slice(h * hd, (h + 1) * hd)
        lb = lb_ref[:, sl]
        forget = lb + (1.0 - lb) * _sigmoid(f_ref[:, sl].astype(F32))
        kk = 1.0 - forget
        cum = jnp.log(forget)
        shift = 1
        while shift < blk:
            cum = cum + jnp.where(row_in_blk >= shift, pltpu.roll(cum, shift, axis=0), 0.0)
            shift *= 2
        cum = cum * LOG2E
        cum3 = cum.reshape(nb, blk, hd)
        b_mid = cum3[:, blk // 2:blk // 2 + 1, :]
        b_end = cum3[:, blk - 1:blk, :]
        rel = (cum3 - b_mid).reshape(ts, hd)
        to_end = (b_end - cum3).reshape(ts, hd)
        q = q_ref[:, sl].astype(F32)
        v = i_ref[:, sl].astype(BF16)
        qs = (q * jnp.exp2(rel)).astype(BF16)
        ks = (kk * jnp.exp2(-rel)).astype(BF16)
        o_parts = []
        for c in range(ts // sub):
            cs = slice(c * sub, (c + 1) * sub)
            sc = jnp.where(blk_causal, _mm_nt(qs[cs], ks[cs]), 0.0)
            o_parts.append(jnp.dot(sc.astype(BF16), v[cs], preferred_element_type=F32))
        heads.append(dict(sl=sl, v=v, o_intra=o_parts,
                          q_dec=(q * jnp.exp2(cum)).astype(BF16),
                          k_end=(kk * jnp.exp2(to_end)).astype(BF16),
                          dec=jnp.exp2(b_end), st=st_ref[h], o_inter=[]))

    for n in range(nb):
        rs = slice(n * blk, (n + 1) * blk)
        for hh in heads:
            hh["o_inter"].append(_mm_nt(hh["q_dec"][rs], hh["st"]))
            hh["st"] = hh["st"] * hh["dec"][n] + _mm_tn(hh["v"][rs], hh["k_end"][rs])

    for h, hh in enumerate(heads):
        sl = hh["sl"]
        st_ref[h] = hh["st"]
        o = jnp.concatenate(hh["o_intra"], axis=0) + jnp.concatenate(hh["o_inter"], axis=0)
        o = _rms(o, nw_ref[:, sl]) * _silu(g_ref[:, sl].astype(F32))
        o_ref[:, sl] = o.astype(o_ref.dtype)


def _hgrn_kernel(q_ref, f_ref, i_ref, g_ref, lb_ref, nw_ref, o_ref, st_ref):
    for bb in range(q_ref.shape[0]):
        _hgrn_body(q_ref.at[bb], f_ref.at[bb], i_ref.at[bb], g_ref.at[bb], lb_ref, nw_ref,
                   o_ref.at[bb], st_ref.at[bb])


def _hgrn(q, f, i, g, bsz, seq, lb, nw):
    ts = TS_MIX
    nbat = MIX_BATCH
    tile = lambda b, s: (b, s, 0)
    fixed = lambda b, s: (0, 0)
    as3d = lambda v: v.reshape(bsz, seq, v.shape[-1])
    out = pl.pallas_call(
        _hgrn_kernel,
        grid=(bsz // nbat, seq // ts),
        in_specs=[pl.BlockSpec((nbat, ts, HGRN_WIDTH), tile)] * 4
        + [pl.BlockSpec((1, HGRN_WIDTH), fixed)] * 2,
        out_specs=pl.BlockSpec((nbat, ts, HGRN_WIDTH), tile),
        out_shape=jax.ShapeDtypeStruct((bsz, seq, HGRN_WIDTH), BF16),
        scratch_shapes=[pltpu.VMEM((nbat, HGRN_HEADS, HGRN_HEAD_DIM, HGRN_HEAD_DIM), F32)],
        compiler_params=pltpu.CompilerParams(dimension_semantics=("arbitrary", "arbitrary"),
                                             vmem_limit_bytes=VMEM_LIMIT),
        name="hgrn",
    )(as3d(q), as3d(f), as3d(i), as3d(g), lb, nw)
    return out.reshape(bsz * seq, HGRN_WIDTH)


def _first_argmax4(v):
    m = jnp.maximum(jnp.maximum(v[0], v[1]), jnp.maximum(v[2], v[3]))
    idx = jnp.where(v[0] == m, 0.0, jnp.where(v[1] == m, 1.0, jnp.where(v[2] == m, 2.0, 3.0)))
    return m, idx


def _outproj_kernel(ys_ref, yh_ref, x_ref, wo1_ref, wo2_ref, nw_ref, wr_ref, xe_ref, rt_ref, cnt_ref,
                    base_ref, *, slab):
    tm = ys_ref.shape[0]

    @pl.when(pl.program_id(0) == 0)
    def _():
        base_ref[...] = jnp.zeros_like(base_ref)

    x = _slab_load(x_ref, tm) if slab else x_ref[...]
    xm = (x + jnp.dot(ys_ref[...], wo1_ref[...], preferred_element_type=F32)
          + jnp.dot(yh_ref[...], wo2_ref[...], preferred_element_type=F32))
    h = _rms(xm, nw_ref[...])
    lt = _mm_nt(wr_ref[...], h)
    row = lambda r: lt[r:r + 1, :]
    gl = [row(j) for j in range(N_EXPERT_GROUPS)]
    g_max, g_idx = _first_argmax4(gl)
    g_w = 1.0 / (jnp.exp(gl[0] - g_max) + jnp.exp(gl[1] - g_max)
                 + jnp.exp(gl[2] - g_max) + jnp.exp(gl[3] - g_max))
    el = []
    for j in range(EXPERTS_PER_GROUP):
        base = N_EXPERT_GROUPS + j
        el.append(jnp.where(g_idx == 0.0, row(base),
                  jnp.where(g_idx == 1.0, row(base + EXPERTS_PER_GROUP),
                  jnp.where(g_idx == 2.0, row(base + 2 * EXPERTS_PER_GROUP),
                            row(base + 3 * EXPERTS_PER_GROUP)))))
    e1, i1 = _first_argmax4(el)
    el2 = [jnp.where(i1 == float(j), NEG_BIG, el[j]) for j in range(EXPERTS_PER_GROUP)]
    e2, i2 = _first_argmax4(el2)
    w1 = 1.0 / (1.0 + jnp.exp(e2 - e1))
    w2 = jnp.exp(e2 - e1) * w1
    lo = jnp.minimum(i1, i2)
    hi = jnp.maximum(i1, i2)
    w_lo = g_w * jnp.where(i1 < i2, w1, w2)
    w_hi = g_w * jnp.where(i1 < i2, w2, w1)
    pair = lo * (7.0 - lo) * 0.5 + (hi - lo - 1.0)
    bucket = g_idx * float(N_PAIRS) + pair
    sub_l = lax.broadcasted_iota(jnp.int32, (ROUTE_LANES, tm), 0)
    onehot = jnp.where(sub_l.astype(F32) == bucket, 1.0, 0.0)
    r_i = lax.broadcasted_iota(jnp.int32, (tm, tm), 0)
    c_i = lax.broadcasted_iota(jnp.int32, (tm, tm), 1)
    before = jnp.dot(onehot.astype(BF16), (r_i < c_i).astype(BF16), preferred_element_type=F32)
    base = base_ref[...]
    before = before + jnp.concatenate([base] * (tm // LANES), axis=1)
    rank = jnp.sum(onehot * before, axis=0, keepdims=True)
    base = base + jnp.sum(onehot, axis=1, keepdims=True)
    base_ref[...] = base
    cnt_ref[...] = base
    sub = lax.broadcasted_iota(jnp.int32, (8, tm), 0)
    slab = jnp.where(sub == 0, bucket, jnp.where(sub == 1, w_lo, jnp.where(sub == 2, w_hi,
                     jnp.where(sub == 3, rank, 0.0))))
    rt_ref[0] = slab
    wide = jnp.where(sub_l == 0, bucket, jnp.where(sub_l == 1, w_lo, jnp.where(sub_l == 2, w_hi, 0.0)))
    for c in range(D_MODEL // LANES):
        xe_ref[pl.ds(c, tm, stride=SLAB), :] = xm[:, c * LANES:(c + 1) * LANES]
    xe_ref[pl.ds(D_MODEL // LANES, tm, stride=SLAB), :] = wide.T


def _outproj(ys, yh, x, n_tok, slab, wo1, wo2, nw, wr_t):
    tm = TM_PROJ
    row = lambda i: (i, 0)
    fixed = lambda i: (0, 0)
    return pl.pallas_call(
        functools.partial(_outproj_kernel, slab=slab),
        grid=(n_tok // tm,),
        in_specs=[pl.BlockSpec((tm, SSD_WIDTH), row), pl.BlockSpec((tm, HGRN_WIDTH), row),
                  _x_spec(tm, slab, row),
                  pl.BlockSpec((SSD_WIDTH, D_MODEL), fixed), pl.BlockSpec((HGRN_WIDTH, D_MODEL), fixed),
                  pl.BlockSpec((1, D_MODEL), fixed), pl.BlockSpec((LANES, D_MODEL), fixed)],
        out_specs=[pl.BlockSpec((tm * SLAB, LANES), row), pl.BlockSpec((1, 8, tm), lambda i: (i, 0, 0)),
                   pl.BlockSpec((ROUTE_LANES, LANES), fixed)],
        out_shape=[jax.ShapeDtypeStruct((n_tok * SLAB, LANES), F32),
                   jax.ShapeDtypeStruct((n_tok // tm, 8, tm), F32),
                   jax.ShapeDtypeStruct((ROUTE_LANES, LANES), F32)],
        scratch_shapes=[pltpu.VMEM((ROUTE_LANES, LANES), F32)],
        compiler_params=pltpu.CompilerParams(dimension_semantics=("arbitrary",),
                                             vmem_limit_bytes=VMEM_LIMIT),
        name="outproj_router",
    )(ys, yh, x, wo1, wo2, nw, wr_t)


def _ffn_kernel(ea_ref, eb_ref, nvalid_ref, nused_ref,
                src0_ref, src1_ref, xe_hbm, nw_ref, fw_ref,
                wga_ref, wgb_ref, wua_ref, wub_ref, wda_ref, wdb_ref,
                out_hbm, xbuf0, xbuf1, obuf0, obuf1, gsem, ssem, *, final):
    del ea_ref, eb_ref
    tm = TM_FFN
    j = pl.program_id(0)
    nt = pl.num_programs(0)
    n_used = nused_ref[0]
    xbufs = (xbuf0, xbuf1)
    obufs = (obuf0, obuf1)
    orow = 1 if final else SLAB

    def issue_gather(src_ref, p):
        for r in range(tm):
            pltpu.make_async_copy(xe_hbm.at[pl.ds(src_ref[0, 0, r] * SLAB, SLAB)],
                                  xbufs[p].at[pl.ds(r * SLAB, SLAB)], gsem.at[p]).start(priority=r % 2)

    def wait_gather(p):
        pltpu.make_async_copy(xe_hbm.at[pl.ds(0, tm * SLAB)], xbufs[p], gsem.at[p]).wait()

    def scatter_copy(r, p):
        return pltpu.make_async_copy(obufs[p].at[pl.ds(r * orow, orow)],
                                     out_hbm.at[pl.ds(src0_ref[0, 0, r] * orow, orow)], ssem.at[p])

    def wait_scatter(p, tile):
        n = nvalid_ref[tile]
        n8 = pl.multiple_of((n // 8) * 8, 8)

        @pl.when(n8 > 0)
        def _():
            rows = pl.multiple_of(n8 * orow, 8)
            pltpu.make_async_copy(obufs[p].at[pl.ds(0, rows)], out_hbm.at[pl.ds(0, rows)],
                                  ssem.at[p]).wait()

        def body(r, carry):
            pltpu.make_async_copy(obufs[p].at[pl.ds(0, orow)], out_hbm.at[pl.ds(0, orow)],
                                  ssem.at[p]).wait()
            return carry
        lax.fori_loop(0, n - n8, body, 0)

    @pl.when((j == 0) & (n_used > 0))
    def _():
        issue_gather(src0_ref, 0)

    def step(p):
        @pl.when(j >= 2)
        def _():
            wait_scatter(p, j - 2)

        wait_gather(p)
        issue_gather(src1_ref, 1 - p)
        xm = _slab_load(xbufs[p], tm)
        route = xbufs[p][pl.ds(D_MODEL // LANES, tm, stride=SLAB), :]
        w_lo = route[:, 1:2]
        w_hi = route[:, 2:3]
        h = _rms(xm, nw_ref[...]).astype(BF16)
        hid_a = _silu(jnp.dot(h, wga_ref[0], preferred_element_type=F32)) \
            * jnp.dot(h, wua_ref[0], preferred_element_type=F32)
        hid_b = _silu(jnp.dot(h, wgb_ref[0], preferred_element_type=F32)) \
            * jnp.dot(h, wub_ref[0], preferred_element_type=F32)
        y = (jnp.dot((hid_a * w_lo).astype(BF16), wda_ref[0], preferred_element_type=F32)
             + jnp.dot((hid_b * w_hi).astype(BF16), wdb_ref[0], preferred_element_type=F32))
        out = xm + y
        if final:
            obufs[p][...] = _rms(out, fw_ref[...])
        else:
            for c in range(D_MODEL // LANES):
                obufs[p][pl.ds(c, tm, stride=SLAB), :] = out[:, c * LANES:(c + 1) * LANES]
            obufs[p][pl.ds(D_MODEL // LANES, tm, stride=SLAB), :] = route

        n_valid = nvalid_ref[j]

        @pl.when(n_valid == tm)
        def _():
            for r in range(tm):
                scatter_copy(r, p).start(priority=r % 2)

        @pl.when(n_valid < tm)
        def _():
            def single(r, carry):
                scatter_copy(r, p).start()
                return carry
            lax.fori_loop(0, n_valid, single, 0)

    for p in (0, 1):
        @pl.when((j % 2 == p) & (j < n_used))
        def _():
            step(p)

    @pl.when(j == nt - 1)
    def _():
        for p in (0, 1):
            @pl.when((n_used >= 1) & (n_used % 2 == p))
            def _():
                wait_gather(p)

            @pl.when((n_used >= 1) & ((n_used - 1) % 2 == p))
            def _():
                wait_scatter(p, n_used - 1)

            @pl.when((n_used >= 2) & ((n_used - 2) % 2 == p))
            def _():
                wait_scatter(p, n_used - 2)


def _ffn(xe, n_tok, tile_ea, tile_eb, tile_nvalid, n_used, src, nw, fw, wg, wu, wd, final):
    tm = TM_FFN
    nt = src.shape[0]
    obuf_shape = (tm, D_MODEL) if final else (tm * SLAB, LANES)
    out_rows = (n_tok, D_MODEL) if final else (n_tok * SLAB, LANES)
    clamp = lambda j, nu: jnp.minimum(j, jnp.maximum(nu[0] - 1, 0))
    wa = lambda j, ea, eb, nv, nu: (ea[clamp(j, nu)], 0, 0)
    wb = lambda j, ea, eb, nv, nu: (eb[clamp(j, nu)], 0, 0)
    fixed = lambda j, ea, eb, nv, nu: (0, 0)
    smem_blk = lambda f: pl.BlockSpec((1, 1, tm), f, memory_space=pltpu.SMEM)
    grid_spec = pltpu.PrefetchScalarGridSpec(
        num_scalar_prefetch=4,
        grid=(nt,),
        in_specs=[smem_blk(lambda j, ea, eb, nv, nu: (j, 0, 0)),
                  smem_blk(lambda j, ea, eb, nv, nu: (jnp.minimum(j + 1, nt - 1), 0, 0)),
                  pl.BlockSpec(memory_space=pl.ANY),
                  pl.BlockSpec((1, D_MODEL), fixed), pl.BlockSpec((1, D_MODEL), fixed),
                  pl.BlockSpec((1, D_MODEL, EXPERT_DIM), wa), pl.BlockSpec((1, D_MODEL, EXPERT_DIM), wb),
                  pl.BlockSpec((1, D_MODEL, EXPERT_DIM), wa), pl.BlockSpec((1, D_MODEL, EXPERT_DIM), wb),
                  pl.BlockSpec((1, EXPERT_DIM, D_MODEL), wa), pl.BlockSpec((1, EXPERT_DIM, D_MODEL), wb)],
        out_specs=pl.BlockSpec(memory_space=pl.ANY),
        scratch_shapes=[pltpu.VMEM((tm * SLAB, LANES), F32), pltpu.VMEM((tm * SLAB, LANES), F32),
                        pltpu.VMEM(obuf_shape, F32), pltpu.VMEM(obuf_shape, F32),
                        pltpu.SemaphoreType.DMA((2,)), pltpu.SemaphoreType.DMA((2,))],
    )
    return pl.pallas_call(
        functools.partial(_ffn_kernel, final=final),
        grid_spec=grid_spec,
        out_shape=jax.ShapeDtypeStruct(out_rows, F32),
        compiler_params=pltpu.CompilerParams(dimension_semantics=("arbitrary",),
                                             vmem_limit_bytes=VMEM_LIMIT),
        name="moe_ffn",
    )(tile_ea, tile_eb, tile_nvalid, n_used, src, src, xe, nw, fw, wg, wg, wu, wu, wd, wd)


def _pair_tables():
    lo, hi = [], []
    for a in range(EXPERTS_PER_GROUP):
        for b in range(a + 1, EXPERTS_PER_GROUP):
            lo.append(a)
            hi.append(b)
    return lo, hi


def _dispatch_plan(bucket, rank, counts, n_tok):
    tm = TM_FFN
    nt = n_tok // tm + N_BUCKETS
    i32 = jnp.int32
    tiles_b = (counts + tm - 1) // tm
    tile_end = jnp.cumsum(tiles_b)
    tile_start = tile_end - tiles_b
    n_used = tile_end[-1:]
    pos = tile_start[bucket] * tm + rank
    tile_id = jnp.arange(nt, dtype=i32)
    tile_bucket = jnp.sum((tile_end[None, :] <= tile_id[:, None]).astype(i32), axis=1)
    tile_bucket = jnp.minimum(tile_bucket, N_BUCKETS - 1)
    tile_nvalid = jnp.clip(counts[tile_bucket] - (tile_id - tile_start[tile_bucket]) * tm, 0, tm)
    tile_nvalid = jnp.where(tile_id < n_used[0], tile_nvalid, 0)
    src = jnp.zeros((nt * tm,), i32).at[pos].set(jnp.arange(n_tok, dtype=i32), unique_indices=True)
    lo, hi = _pair_tables()
    grp = tile_bucket // N_PAIRS
    pr = tile_bucket % N_PAIRS
    tile_ea = grp * EXPERTS_PER_GROUP + jnp.asarray(lo, i32)[pr]
    tile_eb = grp * EXPERTS_PER_GROUP + jnp.asarray(hi, i32)[pr]
    return (tile_ea, tile_eb, tile_nvalid.astype(i32), n_used.astype(i32), src.reshape(nt, 1, tm))


def kernel(x, norm_mix_w, w_in, conv_w, conv_b, dt_bias, a_log, d_skip, ssd_norm_w, hgrn_lb_logits,
           hgrn_norm_w, w_out, norm_ffn_w, router_group, router_expert, w_gate, w_up, w_down,
           final_norm_w):
    bsz, seq, _ = x.shape
    depth = w_in.shape[0]
    n_tok = bsz * seq
    assert n_tok % TM_PROJ == 0 and seq % TS_MIX == 0 and n_tok % TM_FFN == 0

    p = jax.nn.softmax(hgrn_lb_logits.astype(F32), axis=0)
    lower_bounds = jnp.cumsum(p, axis=0) - p[0:1]

    o_xbc = SSD_WIDTH
    o_dt = o_xbc + XBC_WIDTH
    o_q = o_dt + SSD_HEADS
    pad_h = lambda v: jnp.pad(v.astype(F32), (0, LANES - SSD_HEADS)).reshape(1, LANES)
    row = lambda v: v.astype(F32).reshape(1, -1)

    xcur = x.reshape(n_tok, D_MODEL)
    for l in range(depth):
        wl = w_in[l].astype(BF16)
        wq, wf, wi, wg = (wl[:, o_q + k * HGRN_WIDTH:o_q + (k + 1) * HGRN_WIDTH] for k in range(4))
        wdt = jnp.pad(wl[:, o_dt:o_q], ((0, 0), (0, LANES - SSD_HEADS)))
        z, xbc, q, f, i, g, dt = _inproj(xcur, n_tok, l > 0, row(norm_mix_w[l]), wl[:, :o_xbc],
                                         wl[:, o_xbc:o_dt], wq, wf, wi, wg, wdt)
        y_ssd = _ssd(z, xbc, dt, bsz, seq, conv_w[l].astype(F32), row(conv_b[l]), dt_bias[l],
                     a_log[l], row(jnp.repeat(d_skip[l], SSD_HEAD_DIM)), row(ssd_norm_w[l]))
        y_hgrn = _hgrn(q, f, i, g, bsz, seq, row(lower_bounds[l]), row(hgrn_norm_w[l]))
        wo = w_out[l].astype(BF16)
        wr_t = jnp.concatenate([router_group[l], router_expert[l]], axis=1).T
        wr_t = jnp.pad(wr_t, ((0, LANES - wr_t.shape[0]), (0, 0))).astype(BF16)
        xe, route, cnt = _outproj(y_ssd, y_hgrn, xcur, n_tok, l > 0, wo[:SSD_WIDTH], wo[SSD_WIDTH:],
                                  row(norm_ffn_w[l]), wr_t)
        bucket = route[:, 0, :].reshape(n_tok).astype(jnp.int32)
        rank = route[:, 3, :].reshape(n_tok).astype(jnp.int32)
        counts = cnt[:N_BUCKETS, 0].astype(jnp.int32)
        tile_ea, tile_eb, tile_nvalid, n_used, src = _dispatch_plan(bucket, rank, counts, n_tok)
        xcur = _ffn(xe, n_tok, tile_ea, tile_eb, tile_nvalid, n_used, src, row(norm_ffn_w[l]),
                    row(final_norm_w), w_gate[l].astype(BF16), w_up[l].astype(BF16),
                    w_down[l].astype(BF16), final=(l == depth - 1))
    return xcur.reshape(bsz, seq, D_MODEL)
```

```python
import functools

import jax
import jax.numpy as jnp
from jax import lax
from jax.experimental import pallas as pl
from jax.experimental.pallas import tpu as pltpu

F32 = jnp.float32
BF16 = jnp.bfloat16

D_MODEL = 1024
SSD_HEADS = 8
SSD_HEAD_DIM = 64
SSD_WIDTH = SSD_HEADS * SSD_HEAD_DIM
SSD_GROUPS = 2
SSD_STATE = 128
CONV_WIDTH = 4
XBC_WIDTH = SSD_WIDTH + 2 * SSD_GROUPS * SSD_STATE
HGRN_HEADS = 4
HGRN_HEAD_DIM = 128
HGRN_WIDTH = HGRN_HEADS * HGRN_HEAD_DIM
HGRN_BLOCK = 32
N_EXPERT_GROUPS = 4
EXPERTS_PER_GROUP = 4
N_EXPERTS = N_EXPERT_GROUPS * EXPERTS_PER_GROUP
EXPERT_DIM = 256
EPS = 1e-6

LANES = 128
N_PAIRS = 6
N_BUCKETS = N_EXPERT_GROUPS * N_PAIRS
ROUTE_LANES = LANES
SLAB = D_MODEL // LANES + 1
NEG_BIG = -1e30
LOG2E = 1.4426950408889634

TM_PROJ = 512
TS_MIX = 256
MIX_BATCH = 2
SSD_SUB = 128
HGRN_SUB = 128
CONV_HALO = 8
TM_FFN = 256
DMA_UNROLL = 8
VMEM_LIMIT = 56 * 1024 * 1024


def _mm(a, b):
    return jnp.dot(a.astype(BF16), b.astype(BF16), preferred_element_type=F32)


def _mm_nt(a, b):
    return lax.dot_general(a.astype(BF16), b.astype(BF16), (((1,), (1,)), ((), ())),
                           preferred_element_type=F32)


def _mm_tn(a, b):
    return lax.dot_general(a.astype(BF16), b.astype(BF16), (((0,), (0,)), ((), ())),
                           preferred_element_type=F32)


def _split3(x):
    p1 = x.astype(BF16)
    r1 = x - p1.astype(F32)
    p2 = r1.astype(BF16)
    p3 = (r1 - p2.astype(F32)).astype(BF16)
    return p1, p2, p3


def _cumsum_mm(tri, x):
    p1, p2, p3 = _split3(x)
    acc = jnp.dot(tri, p1, preferred_element_type=F32)
    acc = acc + jnp.dot(tri, p2, preferred_element_type=F32)
    return acc + jnp.dot(tri, p3, preferred_element_type=F32)


def _sigmoid(x):
    return 0.5 * jnp.tanh(0.5 * x) + 0.5


def _silu(x):
    hx = 0.5 * x
    return hx + hx * jnp.tanh(hx)


def _rms(x, w):
    return x * lax.rsqrt(jnp.mean(x * x, axis=-1, keepdims=True) + EPS) * w


def _slab_load(ref, rows):
    return jnp.concatenate([ref[pl.ds(c, rows, stride=SLAB), :] for c in range(D_MODEL // LANES)],
                           axis=1)


def _x_spec(rows, slab, index):
    if slab:
        return pl.BlockSpec((rows * SLAB, LANES), index)
    return pl.BlockSpec((rows, D_MODEL), index)


def _inproj_kernel(x_ref, nw_ref, wz_ref, wxbc_ref, wq_ref, wf_ref, wi_ref, wg_ref, wdt_ref,
                   z_ref, xbc_ref, q_ref, f_ref, i_ref, g_ref, dt_ref, *, slab):
    x = _slab_load(x_ref, z_ref.shape[0]) if slab else x_ref[...]
    h = _rms(x, nw_ref[...]).astype(BF16)
    for w_ref, o_ref in ((wz_ref, z_ref), (wxbc_ref, xbc_ref), (wq_ref, q_ref),
                         (wf_ref, f_ref), (wi_ref, i_ref), (wg_ref, g_ref), (wdt_ref, dt_ref)):
        o_ref[...] = jnp.dot(h, w_ref[...], preferred_element_type=F32).astype(o_ref.dtype)


def _inproj(x, n_tok, slab, nw, wz, wxbc, wq, wf, wi, wg, wdt):
    tm = TM_PROJ
    row = lambda i: (i, 0)
    fixed = lambda i: (0, 0)
    widths = (SSD_WIDTH, XBC_WIDTH, HGRN_WIDTH, HGRN_WIDTH, HGRN_WIDTH, HGRN_WIDTH, LANES)
    dtypes = (BF16,) * 6 + (F32,)
    return pl.pallas_call(
        functools.partial(_inproj_kernel, slab=slab),
        grid=(n_tok // tm,),
        in_specs=[_x_spec(tm, slab, row), pl.BlockSpec((1, D_MODEL), fixed)]
        + [pl.BlockSpec((D_MODEL, w), fixed) for w in widths],
        out_specs=[pl.BlockSpec((tm, w), row) for w in widths],
        out_shape=[jax.ShapeDtypeStruct((n_tok, w), dt) for w, dt in zip(widths, dtypes)],
        compiler_params=pltpu.CompilerParams(dimension_semantics=("arbitrary",),
                                             vmem_limit_bytes=VMEM_LIMIT),
        name="inproj",
    )(x, nw, wz, wxbc, wq, wf, wi, wg, wdt)


def _ssd_kernel(z_ref, xbc_ref, dt_ref, cw_ref, cb_ref, dtb_ref, alog_ref, dsk_ref, nw_ref, ex_ref,
                y_ref, ext_ref, h_ref):
    nbat, ts, _ = z_ref.shape
    L = SSD_SUB
    hpg = SSD_HEADS // SSD_GROUPS
    gw = SSD_WIDTH // SSD_GROUPS
    nh = SSD_HEADS
    pre = CONV_HALO
    streams = [(bb, c) for bb in range(nbat) for c in range(ts // L)]

    nlc = XBC_WIDTH // LANES

    @pl.when(pl.program_id(1) == 0)
    def _():
        ext_ref[...] = jnp.zeros_like(ext_ref)
        h_ref[...] = jnp.zeros_like(h_ref)

    for bb in range(nbat):
        for j in range(nlc):
            ext_ref[bb, j, pl.ds(pre, ts, stride=2), :] = xbc_ref[bb, :, j * LANES:(j + 1) * LANES].astype(F32)

    r_i = lax.broadcasted_iota(jnp.int32, (L, L), 0)
    c_i = lax.broadcasted_iota(jnp.int32, (L, L), 1)
    causal = r_i >= c_i
    triu = (r_i <= c_i).astype(BF16)
    even = lax.broadcasted_iota(jnp.int32, (L, LANES), 1) < SSD_HEAD_DIM

    lane_tile = lambda v: jnp.concatenate([v] * (ts // LANES), axis=1)
    dt_all, da_all, ldt_all = {}, {}, {}
    for bb in range(nbat):
        dt_raw = dt_ref[bb].T[0:nh, :] + lane_tile(dtb_ref[...])
        dt_all[bb] = jnp.maximum(dt_raw, 0.0) + jnp.log(1.0 + jnp.exp(-jnp.abs(dt_raw)))
        da_all[bb] = dt_all[bb] * lane_tile(-jnp.exp(alog_ref[...]) * LOG2E)
        ldt_all[bb] = jnp.log2(dt_all[bb])

    conv = {}
    for bb, c in streams:
        r0 = c * L
        pieces = []
        for j in range(nlc):
            ls = slice(j * LANES, (j + 1) * LANES)
            acc = cb_ref[:, ls]
            for k in range(CONV_WIDTH):
                tap = ext_ref[bb, j, pl.ds(pre + 2 * (r0 - (CONV_WIDTH - 1 - k)), L, stride=2), :]
                acc = acc + cw_ref[k:k + 1, ls] * tap
            pieces.append(_silu(acc))
        conv[bb, c] = jnp.concatenate(pieces, axis=1)

    cols, key_row, e_cum, xsc = {}, {}, {}, {}
    for bb, c in streams:
        r0 = c * L
        dt = dt_all[bb][:, r0:r0 + L]
        p1, p2, p3 = _split3(da_all[bb][:, r0:r0 + L])
        cum = (jnp.dot(p1, triu, preferred_element_type=F32) + jnp.dot(p2, triu, preferred_element_type=F32)
               + jnp.dot(p3, triu, preferred_element_type=F32))
        cum_last = cum[:, L - 1:L]
        key_row[bb, c] = cum - ldt_all[bb][:, r0:r0 + L]
        rows = jnp.concatenate([cum, jnp.exp2(cum), jnp.exp2(cum_last - cum) * dt,
                                jnp.zeros((LANES - 3 * nh, L), F32)], axis=0)
        cols[bb, c] = rows.T
        hi = cols[bb, c].astype(BF16)
        lo = (cols[bb, c] - hi.astype(F32)).astype(BF16)
        spread = jnp.dot(jnp.concatenate([hi, lo], axis=1), ex_ref[...],
                         preferred_element_type=F32)
        e_cum[bb, c] = spread[:, :SSD_WIDTH]
        xsc[bb, c] = conv[bb, c][:, :SSD_WIDTH] * spread[:, SSD_WIDTH:]

    y_intra = {}
    for bb, c in streams:
        xs = conv[bb, c][:, :SSD_WIDTH]
        for g in range(SSD_GROUPS):
            lo_b = SSD_WIDTH + g * SSD_STATE
            lo_c = SSD_WIDTH + (SSD_GROUPS + g) * SSD_STATE
            cb = _mm_nt(conv[bb, c][:, lo_c:lo_c + SSD_STATE], conv[bb, c][:, lo_b:lo_b + SSD_STATE])
            y_parts = []
            for p in range(hpg // 2):
                h0 = g * hpg + 2 * p
                xp = xs[:, h0 * SSD_HEAD_DIM:(h0 + 2) * SSD_HEAD_DIM]
                scores = []
                for hh in (h0, h0 + 1):
                    seg = cols[bb, c][:, hh:hh + 1] - key_row[bb, c][hh:hh + 1, :]
                    scores.append((cb * jnp.exp2(jnp.where(causal, seg, NEG_BIG))).astype(BF16))
                rhs = jnp.concatenate([jnp.where(even, xp, 0.0), jnp.where(even, 0.0, xp)],
                                      axis=0).astype(BF16)
                y_parts.append(jnp.dot(jnp.concatenate(scores, axis=1), rhs,
                                       preferred_element_type=F32))
            y_intra[bb, c, g] = jnp.concatenate(y_parts, axis=1)

    for c in range(ts // L):
        r0 = c * L
        for bb in range(nbat):
            xs = conv[bb, c][:, :SSD_WIDTH]
            zc = z_ref[bb, r0:r0 + L, :].astype(F32)
            e_last = e_cum[bb, c][L - 1:L, :]
            for g in range(SSD_GROUPS):
                gs = slice(g * gw, (g + 1) * gw)
                lo_b = SSD_WIDTH + g * SSD_STATE
                lo_c = SSD_WIDTH + (SSD_GROUPS + g) * SSD_STATE
                h_t = h_ref[bb, g]
                yg = (y_intra[bb, c, g] + _mm(conv[bb, c][:, lo_c:lo_c + SSD_STATE], h_t) * e_cum[bb, c][:, gs]
                      + dsk_ref[:, gs] * xs[:, gs])
                h_ref[bb, g] = h_t * e_last[:, gs] + _mm_tn(conv[bb, c][:, lo_b:lo_b + SSD_STATE],
                                                            xsc[bb, c][:, gs])
                yg = _rms(yg * _silu(zc[:, gs]), nw_ref[:, gs])
                y_ref[bb, r0:r0 + L, gs] = yg.astype(y_ref.dtype)

    ext_ref[:, :, 0:pre, :] = ext_ref[:, :, 2 * ts:2 * ts + pre, :]


def _head_spread_matrix():
    col = jnp.arange(LANES)[:, None]
    out = jnp.arange(2 * SSD_WIDTH)[None, :]
    want = SSD_HEADS * (1 + out // SSD_WIDTH) + (out % SSD_WIDTH) // SSD_HEAD_DIM
    once = (col == want).astype(BF16)
    return jnp.concatenate([once, once], axis=0)


def _ssd(z, xbc, dt, bsz, seq, cw, cb, dtb, alog, dsk, nw):
    ts = TS_MIX
    nbat = MIX_BATCH
    tile = lambda b, s: (b, s, 0)
    fixed = lambda b, s: (0, 0)
    as3d = lambda v: v.reshape(bsz, seq, v.shape[-1])
    per_head_rows = lambda v: jnp.broadcast_to(v.astype(F32)[:, None], (SSD_HEADS, LANES))
    out = pl.pallas_call(
        _ssd_kernel,
        grid=(bsz // nbat, seq // ts),
        in_specs=[pl.BlockSpec((nbat, ts, SSD_WIDTH), tile), pl.BlockSpec((nbat, ts, XBC_WIDTH), tile),
                  pl.BlockSpec((nbat, ts, LANES), tile),
                  pl.BlockSpec((CONV_WIDTH, XBC_WIDTH), fixed), pl.BlockSpec((1, XBC_WIDTH), fixed),
                  pl.BlockSpec((SSD_HEADS, LANES), fixed), pl.BlockSpec((SSD_HEADS, LANES), fixed),
                  pl.BlockSpec((1, SSD_WIDTH), fixed), pl.BlockSpec((1, SSD_WIDTH), fixed),
                  pl.BlockSpec((2 * LANES, 2 * SSD_WIDTH), fixed)],
        out_specs=pl.BlockSpec((nbat, ts, SSD_WIDTH), tile),
        out_shape=jax.ShapeDtypeStruct((bsz, seq, SSD_WIDTH), BF16),
        scratch_shapes=[pltpu.VMEM((nbat, XBC_WIDTH // LANES, 2 * ts + CONV_HALO, LANES), F32),
                        pltpu.VMEM((nbat, SSD_GROUPS, SSD_STATE, SSD_WIDTH // SSD_GROUPS), F32)],
        compiler_params=pltpu.CompilerParams(dimension_semantics=("arbitrary", "arbitrary"),
                                             vmem_limit_bytes=VMEM_LIMIT),
        name="ssd",
    )(as3d(z), as3d(xbc), as3d(dt), cw, cb, per_head_rows(dtb), per_head_rows(alog), dsk, nw,
      _head_spread_matrix())
    return out.reshape(bsz * seq, SSD_WIDTH)


def _hgrn_body(q_ref, f_ref, i_ref, g_ref, lb_ref, nw_ref, o_ref, st_ref):
    ts = q_ref.shape[0]
    blk = HGRN_BLOCK
    nb = ts // blk
    hd = HGRN_HEAD_DIM

    @pl.when(pl.program_id(1) == 0)
    def _():
        st_ref[...] = jnp.zeros_like(st_ref)

    sub = HGRN_SUB
    r_i = lax.broadcasted_iota(jnp.int32, (sub, sub), 0)
    c_i = lax.broadcasted_iota(jnp.int32, (sub, sub), 1)
    blk_causal = (r_i >= c_i) & ((r_i // blk) == (c_i // blk))
    row_in_blk = lax.broadcasted_iota(jnp.int32, (ts, hd), 0) % blk

    heads = []
    for h in range(HGRN_HEADS):
        sl = slice(h * hd, (h + 1) * hd)
        lb = lb_ref[:, sl]
        forget = lb + (1.0 - lb) * _sigmoid(f_ref[:, sl].astype(F32))
        kk = 1.0 - forget
        cum = jnp.log(forget)
        shift = 1
        while shift < blk:
            cum = cum + jnp.where(row_in_blk >= shift, pltpu.roll(cum, shift, axis=0), 0.0)
            shift *= 2
        cum = cum * LOG2E
        cum3 = cum.reshape(nb, blk, hd)
        b_mid = cum3[:, blk // 2:blk // 2 + 1, :]
        b_end = cum3[:, blk - 1:blk, :]
        rel = (cum3 - b_mid).reshape(ts, hd)
        to_end = (b_end - cum3).reshape(ts, hd)
        q = q_ref[:, sl].astype(F32)
        v = i_ref[:, sl].astype(BF16)
        heads.append(dict(sl=sl, v=v, o_intra=[], o_inter=[],
                          qs=(q * jnp.exp2(rel)).astype(BF16),
                          ks=(kk * jnp.exp2(-rel)).astype(BF16),
                          q_dec=(q * jnp.exp2(cum)).astype(BF16),
                          k_end=(kk * jnp.exp2(to_end)).astype(BF16),
                          dec=jnp.exp2(b_end), st=st_ref[h]))

    def pair_diag(x0, x1):
        z = jnp.zeros_like(x0)
        return jnp.concatenate([jnp.concatenate([x0, z], axis=1),
                                jnp.concatenate([z, x1], axis=1)], axis=0)

    pairs = [(heads[i], heads[i + 1]) for i in range(0, HGRN_HEADS, 2)]
    mask2 = jnp.concatenate([blk_causal, blk_causal], axis=1)

    for a, b in pairs:
        for c in range(ts // sub):
            cs = slice(c * sub, (c + 1) * sub)
            sc = _mm_nt(jnp.concatenate([a["qs"][cs], b["qs"][cs]], axis=1),
                        pair_diag(a["ks"][cs], b["ks"][cs]))
            sc = jnp.where(mask2, sc, 0.0).astype(BF16)
            o = jnp.dot(sc, pair_diag(a["v"][cs], b["v"][cs]), preferred_element_type=F32)
            a["o_intra"].append(o[:, :hd])
            b["o_intra"].append(o[:, hd:])

    for n in range(nb):
        rs = slice(n * blk, (n + 1) * blk)
        for a, b in pairs:
            o = _mm_nt(jnp.concatenate([a["q_dec"][rs], b["q_dec"][rs]], axis=1),
                       pair_diag(a["st"].astype(BF16), b["st"].astype(BF16)))
            a["o_inter"].append(o[:, :hd])
            b["o_inter"].append(o[:, hd:])
            for hh in (a, b):
                hh["st"] = hh["st"] * hh["dec"][n] + _mm_tn(hh["v"][rs], hh["k_end"][rs])

    for h, hh in enumerate(heads):
        sl = hh["sl"]
        st_ref[h] = hh["st"]
        o = jnp.concatenate(hh["o_intra"], axis=0) + jnp.concatenate(hh["o_inter"], axis=0)
        o = _rms(o, nw_ref[:, sl]) * _silu(g_ref[:, sl].astype(F32))
        o_ref[:, sl] = o.astype(o_ref.dtype)


def _hgrn_kernel(q_ref, f_ref, i_ref, g_ref, lb_ref, nw_ref, o_ref, st_ref):
    for bb in range(q_ref.shape[0]):
        _hgrn_body(q_ref.at[bb], f_ref.at[bb], i_ref.at[bb], g_ref.at[bb], lb_ref, nw_ref,
                   o_ref.at[bb], st_ref.at[bb])


def _hgrn(q, f, i, g, bsz, seq, lb, nw):
    ts = TS_MIX
    nbat = MIX_BATCH
    tile = lambda b, s: (b, s, 0)
    fixed = lambda b, s: (0, 0)
    as3d = lambda v: v.reshape(bsz, seq, v.shape[-1])
    out = pl.pallas_call(
        _hgrn_kernel,
        grid=(bsz // nbat, seq // ts),
        in_specs=[pl.BlockSpec((nbat, ts, HGRN_WIDTH), tile)] * 4
        + [pl.BlockSpec((1, HGRN_WIDTH), fixed)] * 2,
        out_specs=pl.BlockSpec((nbat, ts, HGRN_WIDTH), tile),
        out_shape=jax.ShapeDtypeStruct((bsz, seq, HGRN_WIDTH), BF16),
        scratch_shapes=[pltpu.VMEM((nbat, HGRN_HEADS, HGRN_HEAD_DIM, HGRN_HEAD_DIM), F32)],
        compiler_params=pltpu.CompilerParams(dimension_semantics=("arbitrary", "arbitrary"),
                                             vmem_limit_bytes=VMEM_LIMIT),
        name="hgrn",
    )(as3d(q), as3d(f), as3d(i), as3d(g), lb, nw)
    return out.reshape(bsz * seq, HGRN_WIDTH)


def _first_argmax4(v):
    m = jnp.maximum(jnp.maximum(v[0], v[1]), jnp.maximum(v[2], v[3]))
    idx = jnp.where(v[0] == m, 0.0, jnp.where(v[1] == m, 1.0, jnp.where(v[2] == m, 2.0, 3.0)))
    return m, idx


def _outproj_kernel(ys_ref, yh_ref, x_ref, wo1_ref, wo2_ref, nw_ref, wr_ref, xe_ref, rt_ref, cnt_ref,
                    base_ref, *, slab):
    tm = ys_ref.shape[0]

    @pl.when(pl.program_id(0) == 0)
    def _():
        base_ref[...] = jnp.zeros_like(base_ref)

    x = _slab_load(x_ref, tm) if slab else x_ref[...]
    xm = (x + jnp.dot(ys_ref[...], wo1_ref[...], preferred_element_type=F32)
          + jnp.dot(yh_ref[...], wo2_ref[...], preferred_element_type=F32))
    h = _rms(xm, nw_ref[...])
    lt = _mm_nt(wr_ref[...], h)
    row = lambda r: lt[r:r + 1, :]
    gl = [row(j) for j in range(N_EXPERT_GROUPS)]
    g_max, g_idx = _first_argmax4(gl)
    g_w = 1.0 / (jnp.exp(gl[0] - g_max) + jnp.exp(gl[1] - g_max)
                 + jnp.exp(gl[2] - g_max) + jnp.exp(gl[3] - g_max))
    el = []
    for j in range(EXPERTS_PER_GROUP):
        base = N_EXPERT_GROUPS + j
        el.append(jnp.where(g_idx == 0.0, row(base),
                  jnp.where(g_idx == 1.0, row(base + EXPERTS_PER_GROUP),
                  jnp.where(g_idx == 2.0, row(base + 2 * EXPERTS_PER_GROUP),
                            row(base + 3 * EXPERTS_PER_GROUP)))))
    e1, i1 = _first_argmax4(el)
    el2 = [jnp.where(i1 == float(j), NEG_BIG, el[j]) for j in range(EXPERTS_PER_GROUP)]
    e2, i2 = _first_argmax4(el2)
    w1 = 1.0 / (1.0 + jnp.exp(e2 - e1))
    w2 = jnp.exp(e2 - e1) * w1
    lo = jnp.minimum(i1, i2)
    hi = jnp.maximum(i1, i2)
    w_lo = g_w * jnp.where(i1 < i2, w1, w2)
    w_hi = g_w * jnp.where(i1 < i2, w2, w1)
    pair = lo * (7.0 - lo) * 0.5 + (hi - lo - 1.0)
    bucket = g_idx * float(N_PAIRS) + pair
    sub_l = lax.broadcasted_iota(jnp.int32, (ROUTE_LANES, tm), 0)
    onehot = jnp.where(sub_l.astype(F32) == bucket, 1.0, 0.0)
    r_i = lax.broadcasted_iota(jnp.int32, (tm, tm), 0)
    c_i = lax.broadcasted_iota(jnp.int32, (tm, tm), 1)
    before = jnp.dot(onehot.astype(BF16), (r_i < c_i).astype(BF16), preferred_element_type=F32)
    base = base_ref[...]
    before = before + jnp.concatenate([base] * (tm // LANES), axis=1)
    rank = jnp.sum(onehot * before, axis=0, keepdims=True)
    base = base + jnp.sum(onehot, axis=1, keepdims=True)
    base_ref[...] = base
    cnt_ref[...] = base
    sub = lax.broadcasted_iota(jnp.int32, (8, tm), 0)
    slab = jnp.where(sub == 0, bucket, jnp.where(sub == 1, w_lo, jnp.where(sub == 2, w_hi,
                     jnp.where(sub == 3, rank, 0.0))))
    rt_ref[0] = slab
    wide = jnp.where(sub_l == 0, bucket, jnp.where(sub_l == 1, w_lo, jnp.where(sub_l == 2, w_hi, 0.0)))
    for c in range(D_MODEL // LANES):
        xe_ref[pl.ds(c, tm, stride=SLAB), :] = xm[:, c * LANES:(c + 1) * LANES]
    xe_ref[pl.ds(D_MODEL // LANES, tm, stride=SLAB), :] = wide.T


def _outproj(ys, yh, x, n_tok, slab, wo1, wo2, nw, wr_t):
    tm = TM_PROJ
    row = lambda i: (i, 0)
    fixed = lambda i: (0, 0)
    return pl.pallas_call(
        functools.partial(_outproj_kernel, slab=slab),
        grid=(n_tok // tm,),
        in_specs=[pl.BlockSpec((tm, SSD_WIDTH), row), pl.BlockSpec((tm, HGRN_WIDTH), row),
                  _x_spec(tm, slab, row),
                  pl.BlockSpec((SSD_WIDTH, D_MODEL), fixed), pl.BlockSpec((HGRN_WIDTH, D_MODEL), fixed),
                  pl.BlockSpec((1, D_MODEL), fixed), pl.BlockSpec((LANES, D_MODEL), fixed)],
        out_specs=[pl.BlockSpec((tm * SLAB, LANES), row), pl.BlockSpec((1, 8, tm), lambda i: (i, 0, 0)),
                   pl.BlockSpec((ROUTE_LANES, LANES), fixed)],
        out_shape=[jax.ShapeDtypeStruct((n_tok * SLAB, LANES), F32),
                   jax.ShapeDtypeStruct((n_tok // tm, 8, tm), F32),
                   jax.ShapeDtypeStruct((ROUTE_LANES, LANES), F32)],
        scratch_shapes=[pltpu.VMEM((ROUTE_LANES, LANES), F32)],
        compiler_params=pltpu.CompilerParams(dimension_semantics=("arbitrary",),
                                             vmem_limit_bytes=VMEM_LIMIT),
        name="outproj_router",
    )(ys, yh, x, wo1, wo2, nw, wr_t)


def _ffn_kernel(ea_ref, eb_ref, nvalid_ref, nused_ref,
                src0_ref, src1_ref, src2_ref, xe_hbm, nw_ref, fw_ref,
                wga_ref, wgb_ref, wua_ref, wub_ref, wda_ref, wdb_ref,
                out_hbm, xbuf0, xbuf1, xbuf2, obuf, gsem, ssem, *, final):
    del ea_ref, eb_ref
    tm = TM_FFN
    j = pl.program_id(0)
    nt = pl.num_programs(0)
    n_used = nused_ref[0]
    xbufs = (xbuf0, xbuf1, xbuf2)
    nx = len(xbufs)
    orow = 1 if final else SLAB

    def issue_gather(src_ref, q):
        for r in range(tm):
            pltpu.make_async_copy(xe_hbm.at[pl.ds(src_ref[0, 0, r] * SLAB, SLAB)],
                                  xbufs[q].at[pl.ds(r * SLAB, SLAB)], gsem.at[q]).start(priority=r % 2)

    def wait_gather(q):
        pltpu.make_async_copy(xe_hbm.at[pl.ds(0, tm * SLAB)], xbufs[q], gsem.at[q]).wait()

    def scatter_copy(r, p):
        return pltpu.make_async_copy(obuf.at[p, pl.ds(r * orow, orow)],
                                     out_hbm.at[pl.ds(src0_ref[0, 0, r] * orow, orow)], ssem.at[p])

    def wait_scatter(p, tile):
        n = nvalid_ref[tile]
        n8 = pl.multiple_of((n // 8) * 8, 8)

        @pl.when(n8 > 0)
        def _():
            rows = pl.multiple_of(n8 * orow, 8)
            pltpu.make_async_copy(obuf.at[p, pl.ds(0, rows)], out_hbm.at[pl.ds(0, rows)],
                                  ssem.at[p]).wait()

        def body(r, carry):
            pltpu.make_async_copy(obuf.at[p, pl.ds(0, orow)], out_hbm.at[pl.ds(0, orow)],
                                  ssem.at[p]).wait()
            return carry
        lax.fori_loop(0, n - n8, body, 0)

    @pl.when((j == 0) & (n_used > 0))
    def _():
        issue_gather(src0_ref, 0)
        issue_gather(src1_ref, 1)

    def step(q):
        p = j % 2

        @pl.when(j >= 2)
        def _():
            wait_scatter(p, j - 2)

        wait_gather(q)
        issue_gather(src2_ref, (q + 2) % nx)
        xm = _slab_load(xbufs[q], tm)
        route = xbufs[q][pl.ds(D_MODEL // LANES, tm, stride=SLAB), :]
        w_lo = route[:, 1:2]
        w_hi = route[:, 2:3]
        h = _rms(xm, nw_ref[...]).astype(BF16)
        hid_a = _silu(jnp.dot(h, wga_ref[0], preferred_element_type=F32)) \
            * jnp.dot(h, wua_ref[0], preferred_element_type=F32)
        hid_b = _silu(jnp.dot(h, wgb_ref[0], preferred_element_type=F32)) \
            * jnp.dot(h, wub_ref[0], preferred_element_type=F32)
        y = (jnp.dot((hid_a * w_lo).astype(BF16), wda_ref[0], preferred_element_type=F32)
             + jnp.dot((hid_b * w_hi).astype(BF16), wdb_ref[0], preferred_element_type=F32))
        out = xm + y
        if final:
            obuf[p] = _rms(out, fw_ref[...])
        else:
            for c in range(D_MODEL // LANES):
                obuf[p, pl.ds(c, tm, stride=SLAB), :] = out[:, c * LANES:(c + 1) * LANES]
            obuf[p, pl.ds(D_MODEL // LANES, tm, stride=SLAB), :] = route

        n_valid = nvalid_ref[j]

        @pl.when(n_valid == tm)
        def _():
            for r in range(tm):
                scatter_copy(r, p).start(priority=r % 2)

        @pl.when(n_valid < tm)
        def _():
            def single(r, carry):
                scatter_copy(r, p).start()
                return carry
            lax.fori_loop(0, n_valid, single, 0)

    for q in range(nx):
        @pl.when((j % nx == q) & (j < n_used))
        def _():
            step(q)

    @pl.when((j == nt - 1) & (n_used >= 1))
    def _():
        for q in range(nx):
            @pl.when((n_used % nx == q) | ((n_used + 1) % nx == q))
            def _():
                wait_gather(q)

        wait_scatter((n_used - 1) % 2, n_used - 1)

        @pl.when(n_used >= 2)
        def _():
            wait_scatter(n_used % 2, n_used - 2)


def _ffn(xe, n_tok, tile_ea, tile_eb, tile_nvalid, n_used, src, nw, fw, wg, wu, wd, final):
    tm = TM_FFN
    nt = src.shape[0]
    obuf_shape = (tm, D_MODEL) if final else (tm * SLAB, LANES)
    out_rows = (n_tok, D_MODEL) if final else (n_tok * SLAB, LANES)
    clamp = lambda j, nu: jnp.minimum(j, jnp.maximum(nu[0] - 1, 0))
    wa = lambda j, ea, eb, nv, nu: (ea[clamp(j, nu)], 0, 0)
    wb = lambda j, ea, eb, nv, nu: (eb[clamp(j, nu)], 0, 0)
    fixed = lambda j, ea, eb, nv, nu: (0, 0)
    smem_blk = lambda f: pl.BlockSpec((1, 1, tm), f, memory_space=pltpu.SMEM)
    grid_spec = pltpu.PrefetchScalarGridSpec(
        num_scalar_prefetch=4,
        grid=(nt,),
        in_specs=[smem_blk(lambda j, ea, eb, nv, nu: (j, 0, 0)),
                  smem_blk(lambda j, ea, eb, nv, nu: (jnp.minimum(j + 1, nt - 1), 0, 0)),
                  smem_blk(lambda j, ea, eb, nv, nu: (jnp.minimum(j + 2, nt - 1), 0, 0)),
                  pl.BlockSpec(memory_space=pl.ANY),
                  pl.BlockSpec((1, D_MODEL), fixed), pl.BlockSpec((1, D_MODEL), fixed),
                  pl.BlockSpec((1, D_MODEL, EXPERT_DIM), wa), pl.BlockSpec((1, D_MODEL, EXPERT_DIM), wb),
                  pl.BlockSpec((1, D_MODEL, EXPERT_DIM), wa), pl.BlockSpec((1, D_MODEL, EXPERT_DIM), wb),
                  pl.BlockSpec((1, EXPERT_DIM, D_MODEL), wa), pl.BlockSpec((1, EXPERT_DIM, D_MODEL), wb)],
        out_specs=pl.BlockSpec(memory_space=pl.ANY),
        scratch_shapes=[pltpu.VMEM((tm * SLAB, LANES), F32)] * 3
        + [pltpu.VMEM((2,) + obuf_shape, F32),
           pltpu.SemaphoreType.DMA((3,)), pltpu.SemaphoreType.DMA((2,))],
    )
    return pl.pallas_call(
        functools.partial(_ffn_kernel, final=final),
        grid_spec=grid_spec,
        out_shape=jax.ShapeDtypeStruct(out_rows, F32),
        compiler_params=pltpu.CompilerParams(dimension_semantics=("arbitrary",),
                                             vmem_limit_bytes=VMEM_LIMIT),
        name="moe_ffn",
    )(tile_ea, tile_eb, tile_nvalid, n_used, src, src, src, xe, nw, fw, wg, wg, wu, wu, wd, wd)


def _pair_tables():
    lo, hi = [], []
    for a in range(EXPERTS_PER_GROUP):
        for b in range(a + 1, EXPERTS_PER_GROUP):
            lo.append(a)
            hi.append(b)
    return lo, hi


def _dispatch_plan(bucket, rank, counts, n_tok):
    tm = TM_FFN
    nt = n_tok // tm + N_BUCKETS
    i32 = jnp.int32
    tiles_b = (counts + tm - 1) // tm
    tile_end = jnp.cumsum(tiles_b)
    tile_start = tile_end - tiles_b
    n_used = tile_end[-1:]
    pos = tile_start[bucket] * tm + rank
    tile_id = jnp.arange(nt, dtype=i32)
    tile_bucket = jnp.sum((tile_end[None, :] <= tile_id[:, None]).astype(i32), axis=1)
    tile_bucket = jnp.minimum(tile_bucket, N_BUCKETS - 1)
    tile_nvalid = jnp.clip(counts[tile_bucket] - (tile_id - tile_start[tile_bucket]) * tm, 0, tm)
    tile_nvalid = jnp.where(tile_id < n_used[0], tile_nvalid, 0)
    src = jnp.zeros((nt * tm,), i32).at[pos].set(jnp.arange(n_tok, dtype=i32), unique_indices=True)
    lo, hi = _pair_tables()
    grp = tile_bucket // N_PAIRS
    pr = tile_bucket % N_PAIRS
    tile_ea = grp * EXPERTS_PER_GROUP + jnp.asarray(lo, i32)[pr]
    tile_eb = grp * EXPERTS_PER_GROUP + jnp.asarray(hi, i32)[pr]
    return (tile_ea, tile_eb, tile_nvalid.astype(i32), n_used.astype(i32), src.reshape(nt, 1, tm))


def kernel(x, norm_mix_w, w_in, conv_w, conv_b, dt_bias, a_log, d_skip, ssd_norm_w, hgrn_lb_logits,
           hgrn_norm_w, w_out, norm_ffn_w, router_group, router_expert, w_gate, w_up, w_down,
           final_norm_w):
    bsz, seq, _ = x.shape
    depth = w_in.shape[0]
    n_tok = bsz * seq
    assert n_tok % TM_PROJ == 0 and seq % TS_MIX == 0 and n_tok % TM_FFN == 0

    p = jax.nn.softmax(hgrn_lb_logits.astype(F32), axis=0)
    lower_bounds = jnp.cumsum(p, axis=0) - p[0:1]

    o_xbc = SSD_WIDTH
    o_dt = o_xbc + XBC_WIDTH
    o_q = o_dt + SSD_HEADS
    pad_h = lambda v: jnp.pad(v.astype(F32), (0, LANES - SSD_HEADS)).reshape(1, LANES)
    row = lambda v: v.astype(F32).reshape(1, -1)

    xcur = x.reshape(n_tok, D_MODEL)
    for l in range(depth):
        wl = w_in[l].astype(BF16)
        wq, wf, wi, wg = (wl[:, o_q + k * HGRN_WIDTH:o_q + (k + 1) * HGRN_WIDTH] for k in range(4))
        wdt = jnp.pad(wl[:, o_dt:o_q], ((0, 0), (0, LANES - SSD_HEADS)))
        z, xbc, q, f, i, g, dt = _inproj(xcur, n_tok, l > 0, row(norm_mix_w[l]), wl[:, :o_xbc],
                                         wl[:, o_xbc:o_dt], wq, wf, wi, wg, wdt)
        y_ssd = _ssd(z, xbc, dt, bsz, seq, conv_w[l].astype(F32), row(conv_b[l]), dt_bias[l],
                     a_log[l], row(jnp.repeat(d_skip[l], SSD_HEAD_DIM)), row(ssd_norm_w[l]))
        y_hgrn = _hgrn(q, f, i, g, bsz, seq, row(lower_bounds[l]), row(hgrn_norm_w[l]))
        wo = w_out[l].astype(BF16)
        wr_t = jnp.concatenate([router_group[l], router_expert[l]], axis=1).T
        wr_t = jnp.pad(wr_t, ((0, LANES - wr_t.shape[0]), (0, 0))).astype(BF16)
        xe, route, cnt = _outproj(y_ssd, y_hgrn, xcur, n_tok, l > 0, wo[:SSD_WIDTH], wo[SSD_WIDTH:],
                                  row(norm_ffn_w[l]), wr_t)
        bucket = route[:, 0, :].reshape(n_tok).astype(jnp.int32)
        rank = route[:, 3, :].reshape(n_tok).astype(jnp.int32)
        counts = cnt[:N_BUCKETS, 0].astype(jnp.int32)
        tile_ea, tile_eb, tile_nvalid, n_used, src = _dispatch_plan(bucket, rank, counts, n_tok)
        xcur = _ffn(xe, n_tok, tile_ea, tile_eb, tile_nvalid, n_used, src, row(norm_ffn_w[l]),
                    row(final_norm_w), w_gate[l].astype(BF16), w_up[l].astype(BF16),
                    w_down[l].astype(BF16), final=(l == depth - 1))
    return xcur.reshape(bsz, seq, D_MODEL)
```

```python
import functools

import jax
import jax.numpy as jnp
from jax import lax
from jax.experimental import pallas as pl
from jax.experimental.pallas import tpu as pltpu

F32 = jnp.float32
BF16 = jnp.bfloat16

D_MODEL = 1024
SSD_HEADS = 8
SSD_HEAD_DIM = 64
SSD_WIDTH = SSD_HEADS * SSD_HEAD_DIM
SSD_GROUPS = 2
SSD_STATE = 128
CONV_WIDTH = 4
XBC_WIDTH = SSD_WIDTH + 2 * SSD_GROUPS * SSD_STATE
HGRN_HEADS = 4
HGRN_HEAD_DIM = 128
HGRN_WIDTH = HGRN_HEADS * HGRN_HEAD_DIM
HGRN_BLOCK = 32
N_EXPERT_GROUPS = 4
EXPERTS_PER_GROUP = 4
N_EXPERTS = N_EXPERT_GROUPS * EXPERTS_PER_GROUP
EXPERT_DIM = 256
EPS = 1e-6

LANES = 128
N_PAIRS = 6
N_BUCKETS = N_EXPERT_GROUPS * N_PAIRS
ROUTE_LANES = LANES
SLAB = D_MODEL // LANES + 1
NEG_BIG = -1e30
LOG2E = 1.4426950408889634

TM_PROJ = 512
TS_MIX = 256
MIX_BATCH = 4
SSD_SUB = 128
HGRN_SUB = 128
CONV_HALO = 8
TM_FFN = 256
PLAN_UNROLL = 8
PLAN_CHUNK = 2048
VMEM_LIMIT = 56 * 1024 * 1024


def _mm(a, b):
    return jnp.dot(a.astype(BF16), b.astype(BF16), preferred_element_type=F32)


def _mm_nt(a, b):
    return lax.dot_general(a.astype(BF16), b.astype(BF16), (((1,), (1,)), ((), ())),
                           preferred_element_type=F32)


def _mm_tn(a, b):
    return lax.dot_general(a.astype(BF16), b.astype(BF16), (((0,), (0,)), ((), ())),
                           preferred_element_type=F32)


def _split3(x):
    p1 = x.astype(BF16)
    r1 = x - p1.astype(F32)
    p2 = r1.astype(BF16)
    p3 = (r1 - p2.astype(F32)).astype(BF16)
    return p1, p2, p3


def _cumsum_mm(tri, x):
    p1, p2, p3 = _split3(x)
    acc = jnp.dot(tri, p1, preferred_element_type=F32)
    acc = acc + jnp.dot(tri, p2, preferred_element_type=F32)
    return acc + jnp.dot(tri, p3, preferred_element_type=F32)


def _sigmoid(x):
    return 0.5 * jnp.tanh(0.5 * x) + 0.5


def _silu(x):
    hx = 0.5 * x
    return hx + hx * jnp.tanh(hx)


def _rms(x, w):
    return x * lax.rsqrt(jnp.mean(x * x, axis=-1, keepdims=True) + EPS) * w


def _slab_load(ref, rows):
    return jnp.concatenate([ref[pl.ds(c, rows, stride=SLAB), :] for c in range(D_MODEL // LANES)],
                           axis=1)


def _x_spec(rows, slab, index):
    if slab:
        return pl.BlockSpec((rows * SLAB, LANES), index)
    return pl.BlockSpec((rows, D_MODEL), index)


def _inproj_kernel(x_ref, nw_ref, wz_ref, wxbc_ref, wq_ref, wf_ref, wi_ref, wg_ref, wdt_ref,
                   z_ref, xbc_ref, q_ref, f_ref, i_ref, g_ref, dt_ref, *, slab):
    x = _slab_load(x_ref, z_ref.shape[0]) if slab else x_ref[...]
    h = _rms(x, nw_ref[...]).astype(BF16)
    for w_ref, o_ref in ((wz_ref, z_ref), (wxbc_ref, xbc_ref), (wq_ref, q_ref),
                         (wf_ref, f_ref), (wi_ref, i_ref), (wg_ref, g_ref), (wdt_ref, dt_ref)):
        o_ref[...] = jnp.dot(h, w_ref[...], preferred_element_type=F32).astype(o_ref.dtype)


def _inproj(x, n_tok, slab, nw, wz, wxbc, wq, wf, wi, wg, wdt):
    tm = TM_PROJ
    row = lambda i: (i, 0)
    fixed = lambda i: (0, 0)
    widths = (SSD_WIDTH, XBC_WIDTH, HGRN_WIDTH, HGRN_WIDTH, HGRN_WIDTH, HGRN_WIDTH, LANES)
    dtypes = (BF16,) * 6 + (F32,)
    return pl.pallas_call(
        functools.partial(_inproj_kernel, slab=slab),
        grid=(n_tok // tm,),
        in_specs=[_x_spec(tm, slab, row), pl.BlockSpec((1, D_MODEL), fixed)]
        + [pl.BlockSpec((D_MODEL, w), fixed) for w in widths],
        out_specs=[pl.BlockSpec((tm, w), row) for w in widths],
        out_shape=[jax.ShapeDtypeStruct((n_tok, w), dt) for w, dt in zip(widths, dtypes)],
        compiler_params=pltpu.CompilerParams(dimension_semantics=("arbitrary",),
                                             vmem_limit_bytes=VMEM_LIMIT),
        name="inproj",
    )(x, nw, wz, wxbc, wq, wf, wi, wg, wdt)


def _ssd_kernel(z_ref, xbc_ref, dt_ref, cw_ref, cb_ref, dtb_ref, alog_ref, dsk_ref, nw_ref, ex_ref,
                y_ref, ext_ref, h_ref):
    nbat, ts, _ = z_ref.shape
    L = SSD_SUB
    hpg = SSD_HEADS // SSD_GROUPS
    gw = SSD_WIDTH // SSD_GROUPS
    nh = SSD_HEADS
    pre = CONV_HALO
    streams = [(bb, c) for bb in range(nbat) for c in range(ts // L)]

    nlc = XBC_WIDTH // LANES

    @pl.when(pl.program_id(1) == 0)
    def _():
        ext_ref[...] = jnp.zeros_like(ext_ref)
        h_ref[...] = jnp.zeros_like(h_ref)

    for bb in range(nbat):
        for j in range(nlc):
            ext_ref[bb, j, pl.ds(pre, ts, stride=2), :] = xbc_ref[bb, :, j * LANES:(j + 1) * LANES].astype(F32)

    r_i = lax.broadcasted_iota(jnp.int32, (L, L), 0)
    c_i = lax.broadcasted_iota(jnp.int32, (L, L), 1)
    causal = r_i >= c_i
    triu = (r_i <= c_i).astype(BF16)
    even = lax.broadcasted_iota(jnp.int32, (L, LANES), 1) < SSD_HEAD_DIM

    lane_tile = lambda v: jnp.concatenate([v] * (ts // LANES), axis=1)
    dt_all, da_all, ldt_all = {}, {}, {}
    for bb in range(nbat):
        dt_raw = dt_ref[bb].T[0:nh, :] + lane_tile(dtb_ref[...])
        dt_all[bb] = jnp.maximum(dt_raw, 0.0) + jnp.log(1.0 + jnp.exp(-jnp.abs(dt_raw)))
        da_all[bb] = dt_all[bb] * lane_tile(-jnp.exp(alog_ref[...]) * LOG2E)
        ldt_all[bb] = jnp.log2(dt_all[bb])

    conv = {}
    for bb, c in streams:
        r0 = c * L
        pieces = []
        for j in range(nlc):
            ls = slice(j * LANES, (j + 1) * LANES)
            acc = cb_ref[:, ls]
            for k in range(CONV_WIDTH):
                tap = ext_ref[bb, j, pl.ds(pre + 2 * (r0 - (CONV_WIDTH - 1 - k)), L, stride=2), :]
                acc = acc + cw_ref[k:k + 1, ls] * tap
            pieces.append(_silu(acc))
        conv[bb, c] = jnp.concatenate(pieces, axis=1)

    cols, key_row, e_cum, xsc = {}, {}, {}, {}
    for bb, c in streams:
        r0 = c * L
        dt = dt_all[bb][:, r0:r0 + L]
        p1, p2, p3 = _split3(da_all[bb][:, r0:r0 + L])
        cum = (jnp.dot(p1, triu, preferred_element_type=F32) + jnp.dot(p2, triu, preferred_element_type=F32)
               + jnp.dot(p3, triu, preferred_element_type=F32))
        cum_last = cum[:, L - 1:L]
        key_row[bb, c] = cum - ldt_all[bb][:, r0:r0 + L]
        rows = jnp.concatenate([cum, jnp.exp2(cum), jnp.exp2(cum_last - cum) * dt,
                                jnp.zeros((LANES - 3 * nh, L), F32)], axis=0)
        cols[bb, c] = rows.T
        hi = cols[bb, c].astype(BF16)
        lo = (cols[bb, c] - hi.astype(F32)).astype(BF16)
        spread = jnp.dot(jnp.concatenate([hi, lo], axis=1), ex_ref[...],
                         preferred_element_type=F32)
        e_cum[bb, c] = spread[:, :SSD_WIDTH]
        xsc[bb, c] = conv[bb, c][:, :SSD_WIDTH] * spread[:, SSD_WIDTH:]

    y_intra = {}
    for bb, c in streams:
        xs = conv[bb, c][:, :SSD_WIDTH]
        for g in range(SSD_GROUPS):
            lo_b = SSD_WIDTH + g * SSD_STATE
            lo_c = SSD_WIDTH + (SSD_GROUPS + g) * SSD_STATE
            cb = _mm_nt(conv[bb, c][:, lo_c:lo_c + SSD_STATE], conv[bb, c][:, lo_b:lo_b + SSD_STATE])
            y_parts = []
            for p in range(hpg // 2):
                h0 = g * hpg + 2 * p
                xp = xs[:, h0 * SSD_HEAD_DIM:(h0 + 2) * SSD_HEAD_DIM]
                scores = []
                for hh in (h0, h0 + 1):
                    seg = cols[bb, c][:, hh:hh + 1] - key_row[bb, c][hh:hh + 1, :]
                    scores.append((cb * jnp.exp2(jnp.where(causal, seg, NEG_BIG))).astype(BF16))
                rhs = jnp.concatenate([jnp.where(even, xp, 0.0), jnp.where(even, 0.0, xp)],
                                      axis=0).astype(BF16)
                y_parts.append(jnp.dot(jnp.concatenate(scores, axis=1), rhs,
                                       preferred_element_type=F32))
            y_intra[bb, c, g] = jnp.concatenate(y_parts, axis=1)

    for c in range(ts // L):
        r0 = c * L
        for bb in range(nbat):
            xs = conv[bb, c][:, :SSD_WIDTH]
            zc = z_ref[bb, r0:r0 + L, :].astype(F32)
            e_last = e_cum[bb, c][L - 1:L, :]
            for g in range(SSD_GROUPS):
                gs = slice(g * gw, (g + 1) * gw)
                lo_b = SSD_WIDTH + g * SSD_STATE
                lo_c = SSD_WIDTH + (SSD_GROUPS + g) * SSD_STATE
                h_t = h_ref[bb, g]
                yg = (y_intra[bb, c, g] + _mm(conv[bb, c][:, lo_c:lo_c + SSD_STATE], h_t) * e_cum[bb, c][:, gs]
                      + dsk_ref[:, gs] * xs[:, gs])
                h_ref[bb, g] = h_t * e_last[:, gs] + _mm_tn(conv[bb, c][:, lo_b:lo_b + SSD_STATE],
                                                            xsc[bb, c][:, gs])
                yg = _rms(yg * _silu(zc[:, gs]), nw_ref[:, gs])
                y_ref[bb, r0:r0 + L, gs] = yg.astype(y_ref.dtype)

    ext_ref[:, :, 0:pre, :] = ext_ref[:, :, 2 * ts:2 * ts + pre, :]


def _head_spread_matrix():
    col = jnp.arange(LANES)[:, None]
    out = jnp.arange(2 * SSD_WIDTH)[None, :]
    want = SSD_HEADS * (1 + out // SSD_WIDTH) + (out % SSD_WIDTH) // SSD_HEAD_DIM
    once = (col == want).astype(BF16)
    return jnp.concatenate([once, once], axis=0)


def _ssd(z, xbc, dt, bsz, seq, cw, cb, dtb, alog, dsk, nw):
    ts = TS_MIX
    nbat = MIX_BATCH
    tile = lambda b, s: (b, s, 0)
    fixed = lambda b, s: (0, 0)
    as3d = lambda v: v.reshape(bsz, seq, v.shape[-1])
    per_head_rows = lambda v: jnp.broadcast_to(v.astype(F32)[:, None], (SSD_HEADS, LANES))
    out = pl.pallas_call(
        _ssd_kernel,
        grid=(bsz // nbat, seq // ts),
        in_specs=[pl.BlockSpec((nbat, ts, SSD_WIDTH), tile), pl.BlockSpec((nbat, ts, XBC_WIDTH), tile),
                  pl.BlockSpec((nbat, ts, LANES), tile),
                  pl.BlockSpec((CONV_WIDTH, XBC_WIDTH), fixed), pl.BlockSpec((1, XBC_WIDTH), fixed),
                  pl.BlockSpec((SSD_HEADS, LANES), fixed), pl.BlockSpec((SSD_HEADS, LANES), fixed),
                  pl.BlockSpec((1, SSD_WIDTH), fixed), pl.BlockSpec((1, SSD_WIDTH), fixed),
                  pl.BlockSpec((2 * LANES, 2 * SSD_WIDTH), fixed)],
        out_specs=pl.BlockSpec((nbat, ts, SSD_WIDTH), tile),
        out_shape=jax.ShapeDtypeStruct((bsz, seq, SSD_WIDTH), BF16),
        scratch_shapes=[pltpu.VMEM((nbat, XBC_WIDTH // LANES, 2 * ts + CONV_HALO, LANES), F32),
                        pltpu.VMEM((nbat, SSD_GROUPS, SSD_STATE, SSD_WIDTH // SSD_GROUPS), F32)],
        compiler_params=pltpu.CompilerParams(dimension_semantics=("arbitrary", "arbitrary"),
                                             vmem_limit_bytes=VMEM_LIMIT),
        name="ssd",
    )(as3d(z), as3d(xbc), as3d(dt), cw, cb, per_head_rows(dtb), per_head_rows(alog), dsk, nw,
      _head_spread_matrix())
    return out.reshape(bsz * seq, SSD_WIDTH)


def _hgrn_body(q_ref, f_ref, i_ref, g_ref, lb_ref, nw_ref, o_ref, st_ref):
    ts = q_ref.shape[0]
    blk = HGRN_BLOCK
    nb = ts // blk
    hd = HGRN_HEAD_DIM

    @pl.when(pl.program_id(1) == 0)
    def _():
        st_ref[...] = jnp.zeros_like(st_ref)

    sub = HGRN_SUB
    r_i = lax.broadcasted_iota(jnp.int32, (sub, sub), 0)
    c_i = lax.broadcasted_iota(jnp.int32, (sub, sub), 1)
    blk_causal = (r_i >= c_i) & ((r_i // blk) == (c_i // blk))
    row_in_blk = lax.broadcasted_iota(jnp.int32, (ts, hd), 0) % blk

    heads = []
    for h in range(HGRN_HEADS):
        sl = slice(h * hd, (h + 1) * hd)
        lb = lb_ref[:, sl]
        forget = lb + (1.0 - lb) * _sigmoid(f_ref[:, sl].astype(F32))
        kk = 1.0 - forget
        cum = jnp.log(forget)
        shift = 1
        while shift < blk:
            cum = cum + jnp.where(row_in_blk >= shift, pltpu.roll(cum, shift, axis=0), 0.0)
            shift *= 2
        cum = cum * LOG2E
        cum3 = cum.reshape(nb, blk, hd)
        b_mid = cum3[:, blk // 2:blk // 2 + 1, :]
        b_end = cum3[:, blk - 1:blk, :]
        rel = (cum3 - b_mid).reshape(ts, hd)
        to_end = (b_end - cum3).reshape(ts, hd)
        q = q_ref[:, sl].astype(F32)
        v = i_ref[:, sl].astype(BF16)
        heads.append(dict(sl=sl, v=v, o_intra=[], o_inter=[],
                          qs=(q * jnp.exp2(rel)).astype(BF16),
                          ks=(kk * jnp.exp2(-rel)).astype(BF16),
                          q_dec=(q * jnp.exp2(cum)).astype(BF16),
                          k_end=(kk * jnp.exp2(to_end)).astype(BF16),
                          dec=jnp.exp2(b_end), st=st_ref[h]))

    def pair_diag(x0, x1):
        z = jnp.zeros_like(x0)
        return jnp.concatenate([jnp.concatenate([x0, z], axis=1),
                                jnp.concatenate([z, x1], axis=1)], axis=0)

    pairs = [(heads[i], heads[i + 1]) for i in range(0, HGRN_HEADS, 2)]
    mask2 = jnp.concatenate([blk_causal, blk_causal], axis=1)

    for a, b in pairs:
        for c in range(ts // sub):
            cs = slice(c * sub, (c + 1) * sub)
            sc = _mm_nt(jnp.concatenate([a["qs"][cs], b["qs"][cs]], axis=1),
                        pair_diag(a["ks"][cs], b["ks"][cs]))
            sc = jnp.where(mask2, sc, 0.0).astype(BF16)
            o = jnp.dot(sc, pair_diag(a["v"][cs], b["v"][cs]), preferred_element_type=F32)
            a["o_intra"].append(o[:, :hd])
            b["o_intra"].append(o[:, hd:])

    for n in range(nb):
        rs = slice(n * blk, (n + 1) * blk)
        for a, b in pairs:
            o = _mm_nt(jnp.concatenate([a["q_dec"][rs], b["q_dec"][rs]], axis=1),
                       pair_diag(a["st"].astype(BF16), b["st"].astype(BF16)))
            a["o_inter"].append(o[:, :hd])
            b["o_inter"].append(o[:, hd:])
            for hh in (a, b):
                hh["st"] = hh["st"] * hh["dec"][n] + _mm_tn(hh["v"][rs], hh["k_end"][rs])

    for h, hh in enumerate(heads):
        sl = hh["sl"]
        st_ref[h] = hh["st"]
        o = jnp.concatenate(hh["o_intra"], axis=0) + jnp.concatenate(hh["o_inter"], axis=0)
        o = _rms(o, nw_ref[:, sl]) * _silu(g_ref[:, sl].astype(F32))
        o_ref[:, sl] = o.astype(o_ref.dtype)


def _hgrn_kernel(q_ref, f_ref, i_ref, g_ref, lb_ref, nw_ref, o_ref, st_ref):
    for bb in range(q_ref.shape[0]):
        _hgrn_body(q_ref.at[bb], f_ref.at[bb], i_ref.at[bb], g_ref.at[bb], lb_ref, nw_ref,
                   o_ref.at[bb], st_ref.at[bb])


def _hgrn(q, f, i, g, bsz, seq, lb, nw):
    ts = TS_MIX
    nbat = MIX_BATCH
    tile = lambda b, s: (b, s, 0)
    fixed = lambda b, s: (0, 0)
    as3d = lambda v: v.reshape(bsz, seq, v.shape[-1])
    out = pl.pallas_call(
        _hgrn_kernel,
        grid=(bsz // nbat, seq // ts),
        in_specs=[pl.BlockSpec((nbat, ts, HGRN_WIDTH), tile)] * 4
        + [pl.BlockSpec((1, HGRN_WIDTH), fixed)] * 2,
        out_specs=pl.BlockSpec((nbat, ts, HGRN_WIDTH), tile),
        out_shape=jax.ShapeDtypeStruct((bsz, seq, HGRN_WIDTH), BF16),
        scratch_shapes=[pltpu.VMEM((nbat, HGRN_HEADS, HGRN_HEAD_DIM, HGRN_HEAD_DIM), F32)],
        compiler_params=pltpu.CompilerParams(dimension_semantics=("arbitrary", "arbitrary"),
                                             vmem_limit_bytes=VMEM_LIMIT),
        name="hgrn",
    )(as3d(q), as3d(f), as3d(i), as3d(g), lb, nw)
    return out.reshape(bsz * seq, HGRN_WIDTH)


def _first_argmax4(v):
    m = jnp.maximum(jnp.maximum(v[0], v[1]), jnp.maximum(v[2], v[3]))
    idx = jnp.where(v[0] == m, 0.0, jnp.where(v[1] == m, 1.0, jnp.where(v[2] == m, 2.0, 3.0)))
    return m, idx


def _outproj_kernel(ys_ref, yh_ref, x_ref, wo1_ref, wo2_ref, nw_ref, wr_ref, xe_ref, rt_ref, cnt_ref,
                    base_ref, *, slab):
    tm = ys_ref.shape[0]

    @pl.when(pl.program_id(0) == 0)
    def _():
        base_ref[...] = jnp.zeros_like(base_ref)

    x = _slab_load(x_ref, tm) if slab else x_ref[...]
    xm = (x + jnp.dot(ys_ref[...], wo1_ref[...], preferred_element_type=F32)
          + jnp.dot(yh_ref[...], wo2_ref[...], preferred_element_type=F32))
    h = _rms(xm, nw_ref[...])
    lt = _mm_nt(wr_ref[...], h)
    row = lambda r: lt[r:r + 1, :]
    gl = [row(j) for j in range(N_EXPERT_GROUPS)]
    g_max, g_idx = _first_argmax4(gl)
    g_w = 1.0 / (jnp.exp(gl[0] - g_max) + jnp.exp(gl[1] - g_max)
                 + jnp.exp(gl[2] - g_max) + jnp.exp(gl[3] - g_max))
    el = []
    for j in range(EXPERTS_PER_GROUP):
        base = N_EXPERT_GROUPS + j
        el.append(jnp.where(g_idx == 0.0, row(base),
                  jnp.where(g_idx == 1.0, row(base + EXPERTS_PER_GROUP),
                  jnp.where(g_idx == 2.0, row(base + 2 * EXPERTS_PER_GROUP),
                            row(base + 3 * EXPERTS_PER_GROUP)))))
    e1, i1 = _first_argmax4(el)
    el2 = [jnp.where(i1 == float(j), NEG_BIG, el[j]) for j in range(EXPERTS_PER_GROUP)]
    e2, i2 = _first_argmax4(el2)
    w1 = 1.0 / (1.0 + jnp.exp(e2 - e1))
    w2 = jnp.exp(e2 - e1) * w1
    lo = jnp.minimum(i1, i2)
    hi = jnp.maximum(i1, i2)
    w_lo = g_w * jnp.where(i1 < i2, w1, w2)
    w_hi = g_w * jnp.where(i1 < i2, w2, w1)
    pair = lo * (7.0 - lo) * 0.5 + (hi - lo - 1.0)
    bucket = g_idx * float(N_PAIRS) + pair
    sub_l = lax.broadcasted_iota(jnp.int32, (ROUTE_LANES, tm), 0)
    onehot = jnp.where(sub_l.astype(F32) == bucket, 1.0, 0.0)
    r_i = lax.broadcasted_iota(jnp.int32, (tm, tm), 0)
    c_i = lax.broadcasted_iota(jnp.int32, (tm, tm), 1)
    before = jnp.dot(onehot.astype(BF16), (r_i < c_i).astype(BF16), preferred_element_type=F32)
    base = base_ref[...]
    before = before + jnp.concatenate([base] * (tm // LANES), axis=1)
    rank = jnp.sum(onehot * before, axis=0, keepdims=True)
    base = base + jnp.sum(onehot, axis=1, keepdims=True)
    base_ref[...] = base
    cnt_ref[...] = base
    sub = lax.broadcasted_iota(jnp.int32, (8, tm), 0)
    rt_ref[0] = jnp.where(sub == 0, bucket, jnp.where(sub == 1, rank, 0.0)).astype(jnp.int32)
    wide = jnp.where(sub_l == 0, bucket, jnp.where(sub_l == 1, w_lo, jnp.where(sub_l == 2, w_hi, 0.0)))
    for c in range(D_MODEL // LANES):
        xe_ref[pl.ds(c, tm, stride=SLAB), :] = xm[:, c * LANES:(c + 1) * LANES]
    xe_ref[pl.ds(D_MODEL // LANES, tm, stride=SLAB), :] = wide.T


def _outproj(ys, yh, x, n_tok, slab, wo1, wo2, nw, wr_t):
    tm = TM_PROJ
    row = lambda i: (i, 0)
    fixed = lambda i: (0, 0)
    return pl.pallas_call(
        functools.partial(_outproj_kernel, slab=slab),
        grid=(n_tok // tm,),
        in_specs=[pl.BlockSpec((tm, SSD_WIDTH), row), pl.BlockSpec((tm, HGRN_WIDTH), row),
                  _x_spec(tm, slab, row),
                  pl.BlockSpec((SSD_WIDTH, D_MODEL), fixed), pl.BlockSpec((HGRN_WIDTH, D_MODEL), fixed),
                  pl.BlockSpec((1, D_MODEL), fixed), pl.BlockSpec((LANES, D_MODEL), fixed)],
        out_specs=[pl.BlockSpec((tm * SLAB, LANES), row), pl.BlockSpec((1, 8, tm), lambda i: (i, 0, 0)),
                   pl.BlockSpec((ROUTE_LANES, LANES), fixed)],
        out_shape=[jax.ShapeDtypeStruct((n_tok * SLAB, LANES), F32),
                   jax.ShapeDtypeStruct((n_tok // tm, 8, tm), jnp.int32),
                   jax.ShapeDtypeStruct((ROUTE_LANES, LANES), F32)],
        scratch_shapes=[pltpu.VMEM((ROUTE_LANES, LANES), F32)],
        compiler_params=pltpu.CompilerParams(dimension_semantics=("arbitrary",),
                                             vmem_limit_bytes=VMEM_LIMIT),
        name="outproj_router",
    )(ys, yh, x, wo1, wo2, nw, wr_t)


def _ffn_kernel(ea_ref, eb_ref, nvalid_ref, nused_ref,
                src0_ref, src1_ref, src2_ref, xe_hbm, nw_ref, fw_ref,
                wga_ref, wgb_ref, wua_ref, wub_ref, wda_ref, wdb_ref,
                out_hbm, xbuf0, xbuf1, xbuf2, obuf, gsem, ssem, *, final):
    del ea_ref, eb_ref
    tm = TM_FFN
    j = pl.program_id(0)
    nt = pl.num_programs(0)
    n_used = nused_ref[0]
    xbufs = (xbuf0, xbuf1, xbuf2)
    nx = len(xbufs)
    orow = 1 if final else SLAB

    def issue_gather(src_ref, q):
        for r in range(tm):
            pltpu.make_async_copy(xe_hbm.at[pl.ds(src_ref[0, 0, r] * SLAB, SLAB)],
                                  xbufs[q].at[pl.ds(r * SLAB, SLAB)], gsem.at[q]).start(priority=r % 2)

    def wait_gather(q):
        pltpu.make_async_copy(xe_hbm.at[pl.ds(0, tm * SLAB)], xbufs[q], gsem.at[q]).wait()

    def scatter_copy(r, p):
        return pltpu.make_async_copy(obuf.at[p, pl.ds(r * orow, orow)],
                                     out_hbm.at[pl.ds(src0_ref[0, 0, r] * orow, orow)], ssem.at[p])

    def wait_scatter(p, tile):
        n = nvalid_ref[tile]
        n8 = pl.multiple_of((n // 8) * 8, 8)

        @pl.when(n8 > 0)
        def _():
            rows = pl.multiple_of(n8 * orow, 8)
            pltpu.make_async_copy(obuf.at[p, pl.ds(0, rows)], out_hbm.at[pl.ds(0, rows)],
                                  ssem.at[p]).wait()

        def body(r, carry):
            pltpu.make_async_copy(obuf.at[p, pl.ds(0, orow)], out_hbm.at[pl.ds(0, orow)],
                                  ssem.at[p]).wait()
            return carry
        lax.fori_loop(0, n - n8, body, 0)

    @pl.when((j == 0) & (n_used > 0))
    def _():
        issue_gather(src0_ref, 0)
        issue_gather(src1_ref, 1)

    def step(q):
        p = j % 2

        @pl.when(j >= 2)
        def _():
            wait_scatter(p, j - 2)

        wait_gather(q)
        issue_gather(src2_ref, (q + 2) % nx)
        xm = _slab_load(xbufs[q], tm)
        route = xbufs[q][pl.ds(D_MODEL // LANES, tm, stride=SLAB), :]
        w_lo = route[:, 1:2]
        w_hi = route[:, 2:3]
        h = _rms(xm, nw_ref[...]).astype(BF16)
        wt = lambda w_ref: w_ref[0, 0].astype(BF16)
        hid_a = _silu(jnp.dot(h, wt(wga_ref), preferred_element_type=F32)) \
            * jnp.dot(h, wt(wua_ref), preferred_element_type=F32)
        hid_b = _silu(jnp.dot(h, wt(wgb_ref), preferred_element_type=F32)) \
            * jnp.dot(h, wt(wub_ref), preferred_element_type=F32)
        y = (jnp.dot((hid_a * w_lo).astype(BF16), wt(wda_ref), preferred_element_type=F32)
             + jnp.dot((hid_b * w_hi).astype(BF16), wt(wdb_ref), preferred_element_type=F32))
        out = xm + y
        if final:
            obuf[p] = _rms(out, fw_ref[...])
        else:
            for c in range(D_MODEL // LANES):
                obuf[p, pl.ds(c, tm, stride=SLAB), :] = out[:, c * LANES:(c + 1) * LANES]
            obuf[p, pl.ds(D_MODEL // LANES, tm, stride=SLAB), :] = route

        n_valid = nvalid_ref[j]

        @pl.when(n_valid == tm)
        def _():
            for r in range(tm):
                scatter_copy(r, p).start(priority=r % 2)

        @pl.when(n_valid < tm)
        def _():
            def single(r, carry):
                scatter_copy(r, p).start()
                return carry
            lax.fori_loop(0, n_valid, single, 0)

    for q in range(nx):
        @pl.when((j % nx == q) & (j < n_used))
        def _():
            step(q)

    @pl.when((j == nt - 1) & (n_used >= 1))
    def _():
        for q in range(nx):
            @pl.when((n_used % nx == q) | ((n_used + 1) % nx == q))
            def _():
                wait_gather(q)

        wait_scatter((n_used - 1) % 2, n_used - 1)

        @pl.when(n_used >= 2)
        def _():
            wait_scatter(n_used % 2, n_used - 2)


def _ffn(xe, n_tok, tile_ea, tile_eb, tile_nvalid, n_used, src, nw, fw, layer, wg, wu, wd, final):
    tm = TM_FFN
    nt = src.shape[0]
    obuf_shape = (tm, D_MODEL) if final else (tm * SLAB, LANES)
    out_rows = (n_tok, D_MODEL) if final else (n_tok * SLAB, LANES)
    clamp = lambda j, nu: jnp.minimum(j, jnp.maximum(nu[0] - 1, 0))
    wa = lambda j, ea, eb, nv, nu: (layer, ea[clamp(j, nu)], 0, 0)
    wb = lambda j, ea, eb, nv, nu: (layer, eb[clamp(j, nu)], 0, 0)
    fixed = lambda j, ea, eb, nv, nu: (0, 0)
    smem_blk = lambda f: pl.BlockSpec((1, 1, tm), f, memory_space=pltpu.SMEM)
    grid_spec = pltpu.PrefetchScalarGridSpec(
        num_scalar_prefetch=4,
        grid=(nt,),
        in_specs=[smem_blk(lambda j, ea, eb, nv, nu: (j, 0, 0)),
                  smem_blk(lambda j, ea, eb, nv, nu: (jnp.minimum(j + 1, nt - 1), 0, 0)),
                  smem_blk(lambda j, ea, eb, nv, nu: (jnp.minimum(j + 2, nt - 1), 0, 0)),
                  pl.BlockSpec(memory_space=pl.ANY),
                  pl.BlockSpec((1, D_MODEL), fixed), pl.BlockSpec((1, D_MODEL), fixed),
                  pl.BlockSpec((1, 1, D_MODEL, EXPERT_DIM), wa), pl.BlockSpec((1, 1, D_MODEL, EXPERT_DIM), wb),
                  pl.BlockSpec((1, 1, D_MODEL, EXPERT_DIM), wa), pl.BlockSpec((1, 1, D_MODEL, EXPERT_DIM), wb),
                  pl.BlockSpec((1, 1, EXPERT_DIM, D_MODEL), wa), pl.BlockSpec((1, 1, EXPERT_DIM, D_MODEL), wb)],
        out_specs=pl.BlockSpec(memory_space=pl.ANY),
        scratch_shapes=[pltpu.VMEM((tm * SLAB, LANES), F32)] * 3
        + [pltpu.VMEM((2,) + obuf_shape, F32),
           pltpu.SemaphoreType.DMA((3,)), pltpu.SemaphoreType.DMA((2,))],
    )
    return pl.pallas_call(
        functools.partial(_ffn_kernel, final=final),
        grid_spec=grid_spec,
        out_shape=jax.ShapeDtypeStruct(out_rows, F32),
        compiler_params=pltpu.CompilerParams(dimension_semantics=("arbitrary",),
                                             vmem_limit_bytes=VMEM_LIMIT),
        name="moe_ffn",
    )(tile_ea, tile_eb, tile_nvalid, n_used, src, src, src, xe, nw, fw, wg, wg, wu, wu, wd, wd)


def _pair_tables():
    lo, hi = [], []
    for a in range(EXPERTS_PER_GROUP):
        for b in range(a + 1, EXPERTS_PER_GROUP):
            lo.append(a)
            hi.append(b)
    return lo, hi


def _plan_kernel(base_ref, pad_lo_ref, pad_hi_ref, bucket_ref, rank_ref, src_ref):
    i = pl.program_id(0)
    ch = bucket_ref.shape[0]

    @pl.when(i == 0)
    def _():
        for b in range(N_BUCKETS):
            def zero(k, carry):
                src_ref[k] = 0
                return carry
            lax.fori_loop(pad_lo_ref[b], pad_hi_ref[b], zero, 0)

    def body(k, carry):
        for u in range(PLAN_UNROLL):
            t = k * PLAN_UNROLL + u
            src_ref[base_ref[bucket_ref[t]] + rank_ref[t]] = i * ch + t
        return carry
    lax.fori_loop(0, ch // PLAN_UNROLL, body, 0)


def _plan(base, pad_lo, pad_hi, bucket, rank, n_slots):
    ch = min(PLAN_CHUNK, bucket.shape[0])
    tok = lambda i, *_: (i,)
    return pl.pallas_call(
        _plan_kernel,
        grid_spec=pltpu.PrefetchScalarGridSpec(
            num_scalar_prefetch=3,
            grid=(bucket.shape[0] // ch,),
            in_specs=[pl.BlockSpec((ch,), tok, memory_space=pltpu.SMEM),
                      pl.BlockSpec((ch,), tok, memory_space=pltpu.SMEM)],
            out_specs=pl.BlockSpec(memory_space=pltpu.SMEM),
        ),
        out_shape=jax.ShapeDtypeStruct((n_slots,), jnp.int32),
        compiler_params=pltpu.CompilerParams(dimension_semantics=("arbitrary",)),
        name="dispatch_plan",
    )(base, pad_lo, pad_hi, bucket, rank)


def _dispatch_plan(route_i, counts, n_tok):
    tm = TM_FFN
    nt = n_tok // tm + N_BUCKETS
    i32 = jnp.int32
    tiles_b = (counts + tm - 1) // tm
    tile_end = jnp.cumsum(tiles_b)
    tile_start = tile_end - tiles_b
    n_used = tile_end[-1:]
    tile_id = jnp.arange(nt, dtype=i32)
    tile_bucket = jnp.sum((tile_end[None, :] <= tile_id[:, None]).astype(i32), axis=1)
    tile_bucket = jnp.minimum(tile_bucket, N_BUCKETS - 1)
    tile_nvalid = jnp.clip(counts[tile_bucket] - (tile_id - tile_start[tile_bucket]) * tm, 0, tm)
    tile_nvalid = jnp.where(tile_id < n_used[0], tile_nvalid, 0)
    base = (tile_start * tm).astype(i32)
    next_base = jnp.concatenate([base[1:], jnp.full((1,), nt * tm, i32)])
    bucket = route_i[:, 0, :].reshape(n_tok)
    rank = route_i[:, 1, :].reshape(n_tok)
    src = _plan(base, base + counts, next_base, bucket, rank, nt * tm)
    lo, hi = _pair_tables()
    grp = tile_bucket // N_PAIRS
    pr = tile_bucket % N_PAIRS
    tile_ea = grp * EXPERTS_PER_GROUP + jnp.asarray(lo, i32)[pr]
    tile_eb = grp * EXPERTS_PER_GROUP + jnp.asarray(hi, i32)[pr]
    return (tile_ea, tile_eb, tile_nvalid.astype(i32), n_used.astype(i32), src.reshape(nt, 1, tm))


def kernel(x, norm_mix_w, w_in, conv_w, conv_b, dt_bias, a_log, d_skip, ssd_norm_w, hgrn_lb_logits,
           hgrn_norm_w, w_out, norm_ffn_w, router_group, router_expert, w_gate, w_up, w_down,
           final_norm_w):
    bsz, seq, _ = x.shape
    depth = w_in.shape[0]
    n_tok = bsz * seq
    assert n_tok % TM_PROJ == 0 and seq % TS_MIX == 0 and n_tok % TM_FFN == 0

    p = jax.nn.softmax(hgrn_lb_logits.astype(F32), axis=0)
    lower_bounds = jnp.cumsum(p, axis=0) - p[0:1]

    o_xbc = SSD_WIDTH
    o_dt = o_xbc + XBC_WIDTH
    o_q = o_dt + SSD_HEADS
    pad_h = lambda v: jnp.pad(v.astype(F32), (0, LANES - SSD_HEADS)).reshape(1, LANES)
    row = lambda v: v.astype(F32).reshape(1, -1)

    xcur = x.reshape(n_tok, D_MODEL)
    for l in range(depth):
        wl = w_in[l].astype(BF16)
        wq, wf, wi, wg = (wl[:, o_q + k * HGRN_WIDTH:o_q + (k + 1) * HGRN_WIDTH] for k in range(4))
        wdt = jnp.pad(wl[:, o_dt:o_q], ((0, 0), (0, LANES - SSD_HEADS)))
        z, xbc, q, f, i, g, dt = _inproj(xcur, n_tok, l > 0, row(norm_mix_w[l]), wl[:, :o_xbc],
                                         wl[:, o_xbc:o_dt], wq, wf, wi, wg, wdt)
        y_ssd = _ssd(z, xbc, dt, bsz, seq, conv_w[l].astype(F32), row(conv_b[l]), dt_bias[l],
                     a_log[l], row(jnp.repeat(d_skip[l], SSD_HEAD_DIM)), row(ssd_norm_w[l]))
        y_hgrn = _hgrn(q, f, i, g, bsz, seq, row(lower_bounds[l]), row(hgrn_norm_w[l]))
        wo = w_out[l].astype(BF16)
        wr_t = jnp.concatenate([router_group[l], router_expert[l]], axis=1).T
        wr_t = jnp.pad(wr_t, ((0, LANES - wr_t.shape[0]), (0, 0))).astype(BF16)
        xe, route, cnt = _outproj(y_ssd, y_hgrn, xcur, n_tok, l > 0, wo[:SSD_WIDTH], wo[SSD_WIDTH:],
                                  row(norm_ffn_w[l]), wr_t)
        counts = cnt[:N_BUCKETS, 0].astype(jnp.int32)
        tile_ea, tile_eb, tile_nvalid, n_used, src = _dispatch_plan(route, counts, n_tok)
        xcur = _ffn(xe, n_tok, tile_ea, tile_eb, tile_nvalid, n_used, src, row(norm_ffn_w[l]),
                    row(final_norm_w), l, w_gate, w_up, w_down, final=(l == depth - 1))
    return xcur.reshape(bsz, seq, D_MODEL)
```

```python
import functools

import jax
import jax.numpy as jnp
from jax import lax
from jax.experimental import pallas as pl
from jax.experimental.pallas import tpu as pltpu

F32 = jnp.float32
BF16 = jnp.bfloat16

D_MODEL = 1024
SSD_HEADS = 8
SSD_HEAD_DIM = 64
SSD_WIDTH = SSD_HEADS * SSD_HEAD_DIM
SSD_GROUPS = 2
SSD_STATE = 128
CONV_WIDTH = 4
XBC_WIDTH = SSD_WIDTH + 2 * SSD_GROUPS * SSD_STATE
HGRN_HEADS = 4
HGRN_HEAD_DIM = 128
HGRN_WIDTH = HGRN_HEADS * HGRN_HEAD_DIM
HGRN_BLOCK = 32
N_EXPERT_GROUPS = 4
EXPERTS_PER_GROUP = 4
N_EXPERTS = N_EXPERT_GROUPS * EXPERTS_PER_GROUP
EXPERT_DIM = 256
EPS = 1e-6

LANES = 128
N_PAIRS = 6
N_BUCKETS = N_EXPERT_GROUPS * N_PAIRS
ROUTE_LANES = LANES
SLAB = D_MODEL // LANES + 1
NEG_BIG = -1e30
LOG2E = 1.4426950408889634

TM_PROJ = 512
TS_MIX = 256
MIX_BATCH = 4
SSD_SUB = 128
HGRN_SUB = 128
CONV_HALO = 8
TM_FFN = 256
PLAN_UNROLL = 8
PLAN_CHUNK = 2048
VMEM_LIMIT = 56 * 1024 * 1024


def _mm(a, b):
    return jnp.dot(a.astype(BF16), b.astype(BF16), preferred_element_type=F32)


def _mm_nt(a, b):
    return lax.dot_general(a.astype(BF16), b.astype(BF16), (((1,), (1,)), ((), ())),
                           preferred_element_type=F32)


def _mm_tn(a, b):
    return lax.dot_general(a.astype(BF16), b.astype(BF16), (((0,), (0,)), ((), ())),
                           preferred_element_type=F32)


def _split3(x):
    p1 = x.astype(BF16)
    r1 = x - p1.astype(F32)
    p2 = r1.astype(BF16)
    p3 = (r1 - p2.astype(F32)).astype(BF16)
    return p1, p2, p3


def _cumsum_mm(tri, x):
    p1, p2, p3 = _split3(x)
    acc = jnp.dot(tri, p1, preferred_element_type=F32)
    acc = acc + jnp.dot(tri, p2, preferred_element_type=F32)
    return acc + jnp.dot(tri, p3, preferred_element_type=F32)


def _sigmoid(x):
    return 0.5 * jnp.tanh(0.5 * x) + 0.5


def _silu(x):
    hx = 0.5 * x
    return hx + hx * jnp.tanh(hx)


def _rms(x, w):
    return x * lax.rsqrt(jnp.mean(x * x, axis=-1, keepdims=True) + EPS) * w


def _slab_load(ref, rows):
    return jnp.concatenate([ref[pl.ds(c, rows, stride=SLAB), :] for c in range(D_MODEL // LANES)],
                           axis=1)


def _x_spec(rows, slab, index):
    if slab:
        return pl.BlockSpec((rows * SLAB, LANES), index)
    return pl.BlockSpec((rows, D_MODEL), index)


def _inproj_kernel(x_ref, nw_ref, wz_ref, wxbc_ref, wq_ref, wf_ref, wi_ref, wg_ref, wdt_ref,
                   z_ref, xbc_ref, q_ref, f_ref, i_ref, g_ref, dt_ref, *, slab):
    x = _slab_load(x_ref, z_ref.shape[0]) if slab else x_ref[...]
    h = _rms(x, nw_ref[...]).astype(BF16)
    for w_ref, o_ref in ((wz_ref, z_ref), (wxbc_ref, xbc_ref), (wq_ref, q_ref),
                         (wf_ref, f_ref), (wi_ref, i_ref), (wg_ref, g_ref), (wdt_ref, dt_ref)):
        o_ref[...] = jnp.dot(h, w_ref[...], preferred_element_type=F32).astype(o_ref.dtype)


def _inproj(x, n_tok, slab, nw, wz, wxbc, wq, wf, wi, wg, wdt):
    tm = TM_PROJ
    row = lambda i: (i, 0)
    fixed = lambda i: (0, 0)
    widths = (SSD_WIDTH, XBC_WIDTH, HGRN_WIDTH, HGRN_WIDTH, HGRN_WIDTH, HGRN_WIDTH, LANES)
    dtypes = (BF16,) * 6 + (F32,)
    return pl.pallas_call(
        functools.partial(_inproj_kernel, slab=slab),
        grid=(n_tok // tm,),
        in_specs=[_x_spec(tm, slab, row), pl.BlockSpec((1, D_MODEL), fixed)]
        + [pl.BlockSpec((D_MODEL, w), fixed) for w in widths],
        out_specs=[pl.BlockSpec((tm, w), row) for w in widths],
        out_shape=[jax.ShapeDtypeStruct((n_tok, w), dt) for w, dt in zip(widths, dtypes)],
        compiler_params=pltpu.CompilerParams(dimension_semantics=("arbitrary",),
                                             vmem_limit_bytes=VMEM_LIMIT),
        name="inproj",
    )(x, nw, wz, wxbc, wq, wf, wi, wg, wdt)


def _ssd_kernel(z_ref, xbc_ref, dt_ref, cw_ref, cb_ref, dtb_ref, alog_ref, dsk_ref, nw_ref, ex_ref,
                y_ref, ext_ref, h_ref):
    nbat, ts, _ = z_ref.shape
    L = SSD_SUB
    hpg = SSD_HEADS // SSD_GROUPS
    gw = SSD_WIDTH // SSD_GROUPS
    nh = SSD_HEADS
    pre = CONV_HALO
    streams = [(bb, c) for bb in range(nbat) for c in range(ts // L)]

    nlc = XBC_WIDTH // LANES

    @pl.when(pl.program_id(1) == 0)
    def _():
        ext_ref[...] = jnp.zeros_like(ext_ref)
        h_ref[...] = jnp.zeros_like(h_ref)

    for bb in range(nbat):
        for j in range(nlc):
            ext_ref[bb, j, pl.ds(pre, ts, stride=2), :] = xbc_ref[bb, :, j * LANES:(j + 1) * LANES].astype(F32)

    r_i = lax.broadcasted_iota(jnp.int32, (L, L), 0)
    c_i = lax.broadcasted_iota(jnp.int32, (L, L), 1)
    causal = r_i >= c_i
    triu = (r_i <= c_i).astype(BF16)
    even = lax.broadcasted_iota(jnp.int32, (L, LANES), 1) < SSD_HEAD_DIM

    lane_tile = lambda v: jnp.concatenate([v] * (ts // LANES), axis=1)
    dt_all, da_all, ldt_all = {}, {}, {}
    for bb in range(nbat):
        dt_raw = dt_ref[bb].T[0:nh, :] + lane_tile(dtb_ref[...])
        dt_all[bb] = jnp.maximum(dt_raw, 0.0) + jnp.log(1.0 + jnp.exp(-jnp.abs(dt_raw)))
        da_all[bb] = dt_all[bb] * lane_tile(-jnp.exp(alog_ref[...]) * LOG2E)
        ldt_all[bb] = jnp.log2(dt_all[bb])

    conv = {}
    for bb, c in streams:
        r0 = c * L
        pieces = []
        for j in range(nlc):
            ls = slice(j * LANES, (j + 1) * LANES)
            acc = cb_ref[:, ls]
            for k in range(CONV_WIDTH):
                tap = ext_ref[bb, j, pl.ds(pre + 2 * (r0 - (CONV_WIDTH - 1 - k)), L, stride=2), :]
                acc = acc + cw_ref[k:k + 1, ls] * tap
            pieces.append(_silu(acc))
        conv[bb, c] = jnp.concatenate(pieces, axis=1)

    cols, key_row, e_cum, xsc = {}, {}, {}, {}
    for bb, c in streams:
        r0 = c * L
        dt = dt_all[bb][:, r0:r0 + L]
        p1, p2, p3 = _split3(da_all[bb][:, r0:r0 + L])
        cum = (jnp.dot(p1, triu, preferred_element_type=F32) + jnp.dot(p2, triu, preferred_element_type=F32)
               + jnp.dot(p3, triu, preferred_element_type=F32))
        cum_last = cum[:, L - 1:L]
        key_row[bb, c] = cum - ldt_all[bb][:, r0:r0 + L]
        rows = jnp.concatenate([cum, jnp.exp2(cum), jnp.exp2(cum_last - cum) * dt,
                                jnp.zeros((LANES - 3 * nh, L), F32)], axis=0)
        cols[bb, c] = rows.T
        hi = cols[bb, c].astype(BF16)
        lo = (cols[bb, c] - hi.astype(F32)).astype(BF16)
        spread = jnp.dot(jnp.concatenate([hi, lo], axis=1), ex_ref[...],
                         preferred_element_type=F32)
        e_cum[bb, c] = spread[:, :SSD_WIDTH]
        xsc[bb, c] = conv[bb, c][:, :SSD_WIDTH] * spread[:, SSD_WIDTH:]

    y_intra = {}
    for bb, c in streams:
        xs = conv[bb, c][:, :SSD_WIDTH]
        for g in range(SSD_GROUPS):
            lo_b = SSD_WIDTH + g * SSD_STATE
            lo_c = SSD_WIDTH + (SSD_GROUPS + g) * SSD_STATE
            cb = _mm_nt(conv[bb, c][:, lo_c:lo_c + SSD_STATE], conv[bb, c][:, lo_b:lo_b + SSD_STATE])
            y_parts = []
            for p in range(hpg // 2):
                h0 = g * hpg + 2 * p
                xp = xs[:, h0 * SSD_HEAD_DIM:(h0 + 2) * SSD_HEAD_DIM]
                scores = []
                for hh in (h0, h0 + 1):
                    seg = cols[bb, c][:, hh:hh + 1] - key_row[bb, c][hh:hh + 1, :]
                    scores.append((cb * jnp.exp2(jnp.where(causal, seg, NEG_BIG))).astype(BF16))
                rhs = jnp.concatenate([jnp.where(even, xp, 0.0), jnp.where(even, 0.0, xp)],
                                      axis=0).astype(BF16)
                y_parts.append(jnp.dot(jnp.concatenate(scores, axis=1), rhs,
                                       preferred_element_type=F32))
            y_intra[bb, c, g] = jnp.concatenate(y_parts, axis=1)

    for c in range(ts // L):
        r0 = c * L
        for bb in range(nbat):
            xs = conv[bb, c][:, :SSD_WIDTH]
            zc = z_ref[bb, r0:r0 + L, :].astype(F32)
            e_last = e_cum[bb, c][L - 1:L, :]
            for g in range(SSD_GROUPS):
                gs = slice(g * gw, (g + 1) * gw)
                lo_b = SSD_WIDTH + g * SSD_STATE
                lo_c = SSD_WIDTH + (SSD_GROUPS + g) * SSD_STATE
                h_t = h_ref[bb, g]
                yg = (y_intra[bb, c, g] + _mm(conv[bb, c][:, lo_c:lo_c + SSD_STATE], h_t) * e_cum[bb, c][:, gs]
                      + dsk_ref[:, gs] * xs[:, gs])
                h_ref[bb, g] = h_t * e_last[:, gs] + _mm_tn(conv[bb, c][:, lo_b:lo_b + SSD_STATE],
                                                            xsc[bb, c][:, gs])
                yg = _rms(yg * _silu(zc[:, gs]), nw_ref[:, gs])
                y_ref[bb, r0:r0 + L, gs] = yg.astype(y_ref.dtype)

    ext_ref[:, :, 0:pre, :] = ext_ref[:, :, 2 * ts:2 * ts + pre, :]


def _head_spread_matrix():
    col = jnp.arange(LANES)[:, None]
    out = jnp.arange(2 * SSD_WIDTH)[None, :]
    want = SSD_HEADS * (1 + out // SSD_WIDTH) + (out % SSD_WIDTH) // SSD_HEAD_DIM
    once = (col == want).astype(BF16)
    return jnp.concatenate([once, once], axis=0)


def _ssd(z, xbc, dt, bsz, seq, cw, cb, dtb, alog, dsk, nw):
    ts = TS_MIX
    nbat = MIX_BATCH
    tile = lambda b, s: (b, s, 0)
    fixed = lambda b, s: (0, 0)
    as3d = lambda v: v.reshape(bsz, seq, v.shape[-1])
    per_head_rows = lambda v: jnp.broadcast_to(v.astype(F32)[:, None], (SSD_HEADS, LANES))
    out = pl.pallas_call(
        _ssd_kernel,
        grid=(bsz // nbat, seq // ts),
        in_specs=[pl.BlockSpec((nbat, ts, SSD_WIDTH), tile), pl.BlockSpec((nbat, ts, XBC_WIDTH), tile),
                  pl.BlockSpec((nbat, ts, LANES), tile),
                  pl.BlockSpec((CONV_WIDTH, XBC_WIDTH), fixed), pl.BlockSpec((1, XBC_WIDTH), fixed),
                  pl.BlockSpec((SSD_HEADS, LANES), fixed), pl.BlockSpec((SSD_HEADS, LANES), fixed),
                  pl.BlockSpec((1, SSD_WIDTH), fixed), pl.BlockSpec((1, SSD_WIDTH), fixed),
                  pl.BlockSpec((2 * LANES, 2 * SSD_WIDTH), fixed)],
        out_specs=pl.BlockSpec((nbat, ts, SSD_WIDTH), tile),
        out_shape=jax.ShapeDtypeStruct((bsz, seq, SSD_WIDTH), BF16),
        scratch_shapes=[pltpu.VMEM((nbat, XBC_WIDTH // LANES, 2 * ts + CONV_HALO, LANES), F32),
                        pltpu.VMEM((nbat, SSD_GROUPS, SSD_STATE, SSD_WIDTH // SSD_GROUPS), F32)],
        compiler_params=pltpu.CompilerParams(dimension_semantics=("arbitrary", "arbitrary"),
                                             vmem_limit_bytes=VMEM_LIMIT),
        name="ssd",
    )(as3d(z), as3d(xbc), as3d(dt), cw, cb, per_head_rows(dtb), per_head_rows(alog), dsk, nw,
      _head_spread_matrix())
    return out.reshape(bsz * seq, SSD_WIDTH)


def _hgrn_body(q_ref, f_ref, i_ref, g_ref, lb_ref, nw_ref, o_ref, st_ref):
    ts = q_ref.shape[0]
    blk = HGRN_BLOCK
    nb = ts // blk
    hd = HGRN_HEAD_DIM

    @pl.when(pl.program_id(1) == 0)
    def _():
        st_ref[...] = jnp.zeros_like(st_ref)

    sub = HGRN_SUB
    r_i = lax.broadcasted_iota(jnp.int32, (sub, sub), 0)
    c_i = lax.broadcasted_iota(jnp.int32, (sub, sub), 1)
    blk_causal = (r_i >= c_i) & ((r_i // blk) == (c_i // blk))
    row_in_blk = lax.broadcasted_iota(jnp.int32, (ts, hd), 0) % blk

    heads = []
    for h in range(HGRN_HEADS):
        sl = slice(h * hd, (h + 1) * hd)
        lb = lb_ref[:, sl]
        forget = lb + (1.0 - lb) * _sigmoid(f_ref[:, sl].astype(F32))
        kk = 1.0 - forget
        cum = jnp.log(forget)
        shift = 1
        while shift < blk:
            cum = cum + jnp.where(row_in_blk >= shift, pltpu.roll(cum, shift, axis=0), 0.0)
            shift *= 2
        cum = cum * LOG2E
        cum3 = cum.reshape(nb, blk, hd)
        b_mid = cum3[:, blk // 2:blk // 2 + 1, :]
        b_end = cum3[:, blk - 1:blk, :]
        rel = (cum3 - b_mid).reshape(ts, hd)
        to_end = (b_end - cum3).reshape(ts, hd)
        q = q_ref[:, sl].astype(F32)
        v = i_ref[:, sl].astype(BF16)
        heads.append(dict(sl=sl, v=v, o_intra=[], o_inter=[],
                          qs=(q * jnp.exp2(rel)).astype(BF16),
                          ks=(kk * jnp.exp2(-rel)).astype(BF16),
                          q_dec=(q * jnp.exp2(cum)).astype(BF16),
                          k_end=(kk * jnp.exp2(to_end)).astype(BF16),
                          dec=jnp.exp2(b_end), st=st_ref[h]))

    def pair_diag(x0, x1):
        z = jnp.zeros_like(x0)
        return jnp.concatenate([jnp.concatenate([x0, z], axis=1),
                                jnp.concatenate([z, x1], axis=1)], axis=0)

    pairs = [(heads[i], heads[i + 1]) for i in range(0, HGRN_HEADS, 2)]
    mask2 = jnp.concatenate([blk_causal, blk_causal], axis=1)

    for a, b in pairs:
        for c in range(ts // sub):
            cs = slice(c * sub, (c + 1) * sub)
            sc = _mm_nt(jnp.concatenate([a["qs"][cs], b["qs"][cs]], axis=1),
                        pair_diag(a["ks"][cs], b["ks"][cs]))
            sc = jnp.where(mask2, sc, 0.0).astype(BF16)
            o = jnp.dot(sc, pair_diag(a["v"][cs], b["v"][cs]), preferred_element_type=F32)
            a["o_intra"].append(o[:, :hd])
            b["o_intra"].append(o[:, hd:])

    for n in range(nb):
        rs = slice(n * blk, (n + 1) * blk)
        for a, b in pairs:
            o = _mm_nt(jnp.concatenate([a["q_dec"][rs], b["q_dec"][rs]], axis=1),
                       pair_diag(a["st"].astype(BF16), b["st"].astype(BF16)))
            a["o_inter"].append(o[:, :hd])
            b["o_inter"].append(o[:, hd:])
            for hh in (a, b):
                hh["st"] = hh["st"] * hh["dec"][n] + _mm_tn(hh["v"][rs], hh["k_end"][rs])

    for h, hh in enumerate(heads):
        sl = hh["sl"]
        st_ref[h] = hh["st"]
        o = jnp.concatenate(hh["o_intra"], axis=0) + jnp.concatenate(hh["o_inter"], axis=0)
        o = _rms(o, nw_ref[:, sl]) * _silu(g_ref[:, sl].astype(F32))
        o_ref[:, sl] = o.astype(o_ref.dtype)


def _hgrn_kernel(q_ref, f_ref, i_ref, g_ref, lb_ref, nw_ref, o_ref, st_ref):
    for bb in range(q_ref.shape[0]):
        _hgrn_body(q_ref.at[bb], f_ref.at[bb], i_ref.at[bb], g_ref.at[bb], lb_ref, nw_ref,
                   o_ref.at[bb], st_ref.at[bb])


def _hgrn(q, f, i, g, bsz, seq, lb, nw):
    ts = TS_MIX
    nbat = MIX_BATCH
    tile = lambda b, s: (b, s, 0)
    fixed = lambda b, s: (0, 0)
    as3d = lambda v: v.reshape(bsz, seq, v.shape[-1])
    out = pl.pallas_call(
        _hgrn_kernel,
        grid=(bsz // nbat, seq // ts),
        in_specs=[pl.BlockSpec((nbat, ts, HGRN_WIDTH), tile)] * 4
        + [pl.BlockSpec((1, HGRN_WIDTH), fixed)] * 2,
        out_specs=pl.BlockSpec((nbat, ts, HGRN_WIDTH), tile),
        out_shape=jax.ShapeDtypeStruct((bsz, seq, HGRN_WIDTH), BF16),
        scratch_shapes=[pltpu.VMEM((nbat, HGRN_HEADS, HGRN_HEAD_DIM, HGRN_HEAD_DIM), F32)],
        compiler_params=pltpu.CompilerParams(dimension_semantics=("arbitrary", "arbitrary"),
                                             vmem_limit_bytes=VMEM_LIMIT),
        name="hgrn",
    )(as3d(q), as3d(f), as3d(i), as3d(g), lb, nw)
    return out.reshape(bsz * seq, HGRN_WIDTH)


def _first_argmax4(v):
    m = jnp.maximum(jnp.maximum(v[0], v[1]), jnp.maximum(v[2], v[3]))
    idx = jnp.where(v[0] == m, 0.0, jnp.where(v[1] == m, 1.0, jnp.where(v[2] == m, 2.0, 3.0)))
    return m, idx


def _outproj_kernel(ys_ref, yh_ref, x_ref, wo1_ref, wo2_ref, nw_ref, wr_ref, xe_ref, rt_ref, cnt_ref,
                    base_ref, *, slab):
    tm = ys_ref.shape[0]

    @pl.when(pl.program_id(0) == 0)
    def _():
        base_ref[...] = jnp.zeros_like(base_ref)

    x = _slab_load(x_ref, tm) if slab else x_ref[...]
    xm = (x + jnp.dot(ys_ref[...], wo1_ref[...], preferred_element_type=F32)
          + jnp.dot(yh_ref[...], wo2_ref[...], preferred_element_type=F32))
    h = _rms(xm, nw_ref[...])
    lt = _mm_nt(wr_ref[...], h)
    row = lambda r: lt[r:r + 1, :]
    gl = [row(j) for j in range(N_EXPERT_GROUPS)]
    g_max, g_idx = _first_argmax4(gl)
    g_w = 1.0 / (jnp.exp(gl[0] - g_max) + jnp.exp(gl[1] - g_max)
                 + jnp.exp(gl[2] - g_max) + jnp.exp(gl[3] - g_max))
    el = []
    for j in range(EXPERTS_PER_GROUP):
        base = N_EXPERT_GROUPS + j
        el.append(jnp.where(g_idx == 0.0, row(base),
                  jnp.where(g_idx == 1.0, row(base + EXPERTS_PER_GROUP),
                  jnp.where(g_idx == 2.0, row(base + 2 * EXPERTS_PER_GROUP),
                            row(base + 3 * EXPERTS_PER_GROUP)))))
    e1, i1 = _first_argmax4(el)
    el2 = [jnp.where(i1 == float(j), NEG_BIG, el[j]) for j in range(EXPERTS_PER_GROUP)]
    e2, i2 = _first_argmax4(el2)
    w1 = 1.0 / (1.0 + jnp.exp(e2 - e1))
    w2 = jnp.exp(e2 - e1) * w1
    lo = jnp.minimum(i1, i2)
    hi = jnp.maximum(i1, i2)
    w_lo = g_w * jnp.where(i1 < i2, w1, w2)
    w_hi = g_w * jnp.where(i1 < i2, w2, w1)
    pair = lo * (7.0 - lo) * 0.5 + (hi - lo - 1.0)
    bucket = g_idx * float(N_PAIRS) + pair
    sub_l = lax.broadcasted_iota(jnp.int32, (ROUTE_LANES, tm), 0)
    onehot = jnp.where(sub_l.astype(F32) == bucket, 1.0, 0.0)
    r_i = lax.broadcasted_iota(jnp.int32, (tm, tm), 0)
    c_i = lax.broadcasted_iota(jnp.int32, (tm, tm), 1)
    before = jnp.dot(onehot.astype(BF16), (r_i < c_i).astype(BF16), preferred_element_type=F32)
    base = base_ref[...]
    before = before + jnp.concatenate([base] * (tm // LANES), axis=1)
    rank = jnp.sum(onehot * before, axis=0, keepdims=True)
    base = base + jnp.sum(onehot, axis=1, keepdims=True)
    base_ref[...] = base
    cnt_ref[...] = base
    sub = lax.broadcasted_iota(jnp.int32, (8, tm), 0)
    rt_ref[0] = jnp.where(sub == 0, bucket, jnp.where(sub == 1, rank, 0.0)).astype(jnp.int32)
    wide = jnp.where(sub_l == 0, bucket, jnp.where(sub_l == 1, w_lo, jnp.where(sub_l == 2, w_hi, 0.0)))
    for c in range(D_MODEL // LANES):
        xe_ref[pl.ds(c, tm, stride=SLAB), :] = xm[:, c * LANES:(c + 1) * LANES]
    xe_ref[pl.ds(D_MODEL // LANES, tm, stride=SLAB), :] = wide.T


def _outproj(ys, yh, x, n_tok, slab, wo1, wo2, nw, wr_t):
    tm = TM_PROJ
    row = lambda i: (i, 0)
    fixed = lambda i: (0, 0)
    return pl.pallas_call(
        functools.partial(_outproj_kernel, slab=slab),
        grid=(n_tok // tm,),
        in_specs=[pl.BlockSpec((tm, SSD_WIDTH), row), pl.BlockSpec((tm, HGRN_WIDTH), row),
                  _x_spec(tm, slab, row),
                  pl.BlockSpec((SSD_WIDTH, D_MODEL), fixed), pl.BlockSpec((HGRN_WIDTH, D_MODEL), fixed),
                  pl.BlockSpec((1, D_MODEL), fixed), pl.BlockSpec((LANES, D_MODEL), fixed)],
        out_specs=[pl.BlockSpec((tm * SLAB, LANES), row), pl.BlockSpec((1, 8, tm), lambda i: (i, 0, 0)),
                   pl.BlockSpec((ROUTE_LANES, LANES), fixed)],
        out_shape=[jax.ShapeDtypeStruct((n_tok * SLAB, LANES), F32),
                   jax.ShapeDtypeStruct((n_tok // tm, 8, tm), jnp.int32),
                   jax.ShapeDtypeStruct((ROUTE_LANES, LANES), F32)],
        scratch_shapes=[pltpu.VMEM((ROUTE_LANES, LANES), F32)],
        compiler_params=pltpu.CompilerParams(dimension_semantics=("arbitrary",),
                                             vmem_limit_bytes=VMEM_LIMIT),
        name="outproj_router",
    )(ys, yh, x, wo1, wo2, nw, wr_t)


def _ffn_kernel(ea_ref, eb_ref, nvalid_ref, nused_ref,
                src0_ref, src1_ref, src2_ref, xe_hbm, nw_ref, fw_ref,
                wga_ref, wgb_ref, wua_ref, wub_ref, wda_ref, wdb_ref,
                out_hbm, xbuf0, xbuf1, xbuf2, obuf, gsem, ssem, *, final):
    del ea_ref, eb_ref
    tm = TM_FFN
    j = pl.program_id(0)
    nt = pl.num_programs(0)
    n_used = nused_ref[0]
    xbufs = (xbuf0, xbuf1, xbuf2)
    nx = len(xbufs)
    orow = 1 if final else SLAB

    def issue_gather(src_ref, q):
        for r in range(tm):
            pltpu.make_async_copy(xe_hbm.at[pl.ds(src_ref[0, 0, r] * SLAB, SLAB)],
                                  xbufs[q].at[pl.ds(r * SLAB, SLAB)], gsem.at[q]).start(priority=r % 2)

    def wait_gather(q):
        pltpu.make_async_copy(xe_hbm.at[pl.ds(0, tm * SLAB)], xbufs[q], gsem.at[q]).wait()

    def scatter_copy(r, p):
        return pltpu.make_async_copy(obuf.at[p, pl.ds(r * orow, orow)],
                                     out_hbm.at[pl.ds(src0_ref[0, 0, r] * orow, orow)], ssem.at[p])

    def wait_scatter(p, tile):
        n = nvalid_ref[tile]
        n8 = pl.multiple_of((n // 8) * 8, 8)

        @pl.when(n8 > 0)
        def _():
            rows = pl.multiple_of(n8 * orow, 8)
            pltpu.make_async_copy(obuf.at[p, pl.ds(0, rows)], out_hbm.at[pl.ds(0, rows)],
                                  ssem.at[p]).wait()

        def body(r, carry):
            pltpu.make_async_copy(obuf.at[p, pl.ds(0, orow)], out_hbm.at[pl.ds(0, orow)],
                                  ssem.at[p]).wait()
            return carry
        lax.fori_loop(0, n - n8, body, 0)

    @pl.when((j == 0) & (n_used > 0))
    def _():
        issue_gather(src0_ref, 0)
        issue_gather(src1_ref, 1)

    def step(q):
        p = j % 2

        @pl.when(j >= 2)
        def _():
            wait_scatter(p, j - 2)

        wait_gather(q)
        issue_gather(src2_ref, (q + 2) % nx)
        xm = _slab_load(xbufs[q], tm)
        route = xbufs[q][pl.ds(D_MODEL // LANES, tm, stride=SLAB), :]
        w_lo = route[:, 1:2]
        w_hi = route[:, 2:3]
        h = _rms(xm, nw_ref[...]).astype(BF16)
        wt = lambda w_ref: w_ref[0, 0].astype(BF16)
        hid_a = _silu(jnp.dot(h, wt(wga_ref), preferred_element_type=F32)) \
            * jnp.dot(h, wt(wua_ref), preferred_element_type=F32)
        hid_b = _silu(jnp.dot(h, wt(wgb_ref), preferred_element_type=F32)) \
            * jnp.dot(h, wt(wub_ref), preferred_element_type=F32)
        y = (jnp.dot((hid_a * w_lo).astype(BF16), wt(wda_ref), preferred_element_type=F32)
             + jnp.dot((hid_b * w_hi).astype(BF16), wt(wdb_ref), preferred_element_type=F32))
        out = xm + y
        if final:
            obuf[p] = _rms(out, fw_ref[...])
        else:
            for c in range(D_MODEL // LANES):
                obuf[p, pl.ds(c, tm, stride=SLAB), :] = out[:, c * LANES:(c + 1) * LANES]
            obuf[p, pl.ds(D_MODEL // LANES, tm, stride=SLAB), :] = route

        n_valid = nvalid_ref[j]

        @pl.when(n_valid == tm)
        def _():
            for r in range(tm):
                scatter_copy(r, p).start(priority=r % 2)

        @pl.when(n_valid < tm)
        def _():
            def single(r, carry):
                scatter_copy(r, p).start()
                return carry
            lax.fori_loop(0, n_valid, single, 0)

    for q in range(nx):
        @pl.when((j % nx == q) & (j < n_used))
        def _():
            step(q)

    @pl.when((j == nt - 1) & (n_used >= 1))
    def _():
        for q in range(nx):
            @pl.when((n_used % nx == q) | ((n_used + 1) % nx == q))
            def _():
                wait_gather(q)

        wait_scatter((n_used - 1) % 2, n_used - 1)

        @pl.when(n_used >= 2)
        def _():
            wait_scatter(n_used % 2, n_used - 2)


def _ffn(xe, n_tok, tile_ea, tile_eb, tile_nvalid, n_used, src, nw, fw, layer, wg, wu, wd, final):
    tm = TM_FFN
    nt = src.shape[0]
    obuf_shape = (tm, D_MODEL) if final else (tm * SLAB, LANES)
    out_rows = (n_tok, D_MODEL) if final else (n_tok * SLAB, LANES)
    clamp = lambda j, nu: jnp.minimum(j, jnp.maximum(nu[0] - 1, 0))
    wa = lambda j, ea, eb, nv, nu: (layer, ea[clamp(j, nu)], 0, 0)
    wb = lambda j, ea, eb, nv, nu: (layer, eb[clamp(j, nu)], 0, 0)
    fixed = lambda j, ea, eb, nv, nu: (0, 0)
    smem_blk = lambda f: pl.BlockSpec((1, 1, tm), f, memory_space=pltpu.SMEM)
    grid_spec = pltpu.PrefetchScalarGridSpec(
        num_scalar_prefetch=4,
        grid=(nt,),
        in_specs=[smem_blk(lambda j, ea, eb, nv, nu: (j, 0, 0)),
                  smem_blk(lambda j, ea, eb, nv, nu: (jnp.minimum(j + 1, nt - 1), 0, 0)),
                  smem_blk(lambda j, ea, eb, nv, nu: (jnp.minimum(j + 2, nt - 1), 0, 0)),
                  pl.BlockSpec(memory_space=pl.ANY),
                  pl.BlockSpec((1, D_MODEL), fixed), pl.BlockSpec((1, D_MODEL), fixed),
                  pl.BlockSpec((1, 1, D_MODEL, EXPERT_DIM), wa), pl.BlockSpec((1, 1, D_MODEL, EXPERT_DIM), wb),
                  pl.BlockSpec((1, 1, D_MODEL, EXPERT_DIM), wa), pl.BlockSpec((1, 1, D_MODEL, EXPERT_DIM), wb),
                  pl.BlockSpec((1, 1, EXPERT_DIM, D_MODEL), wa), pl.BlockSpec((1, 1, EXPERT_DIM, D_MODEL), wb)],
        out_specs=pl.BlockSpec(memory_space=pl.ANY),
        scratch_shapes=[pltpu.VMEM((tm * SLAB, LANES), F32)] * 3
        + [pltpu.VMEM((2,) + obuf_shape, F32),
           pltpu.SemaphoreType.DMA((3,)), pltpu.SemaphoreType.DMA((2,))],
    )
    return pl.pallas_call(
        functools.partial(_ffn_kernel, final=final),
        grid_spec=grid_spec,
        out_shape=jax.ShapeDtypeStruct(out_rows, F32),
        compiler_params=pltpu.CompilerParams(dimension_semantics=("arbitrary",),
                                             vmem_limit_bytes=VMEM_LIMIT),
        name="moe_ffn",
    )(tile_ea, tile_eb, tile_nvalid, n_used, src, src, src, xe, nw, fw, wg, wg, wu, wu, wd, wd)


def _pair_tables():
    lo, hi = [], []
    for a in range(EXPERTS_PER_GROUP):
        for b in range(a + 1, EXPERTS_PER_GROUP):
            lo.append(a)
            hi.append(b)
    return lo, hi


def _plan_kernel(counts_ref, bucket_ref, rank_ref, src_ref, tab_ref, base_ref):
    i = pl.program_id(0)
    ch = bucket_ref.shape[0]
    tm = TM_FFN
    nt = tab_ref.shape[1]
    lo, hi = _pair_tables()

    def zero_slots(first, last):
        def zero(k, carry):
            src_ref[k] = 0
            return carry
        lax.fori_loop(first, last, zero, 0)

    @pl.when(i == 0)
    def _():
        tile0 = 0
        for b in range(N_BUCKETS):
            cnt = counts_ref[b]
            n_tiles = (cnt + (tm - 1)) // tm
            base_ref[b] = tile0 * tm
            zero_slots(tile0 * tm + cnt, (tile0 + n_tiles) * tm)
            ea = (b // N_PAIRS) * EXPERTS_PER_GROUP + lo[b % N_PAIRS]
            eb = (b // N_PAIRS) * EXPERTS_PER_GROUP + hi[b % N_PAIRS]

            def tile(k, carry, tile0=tile0, cnt=cnt, ea=ea, eb=eb):
                tab_ref[0, tile0 + k] = ea
                tab_ref[1, tile0 + k] = eb
                tab_ref[2, tile0 + k] = jnp.minimum(cnt - k * tm, tm)
                return carry
            lax.fori_loop(0, n_tiles, tile, 0)
            tile0 = tile0 + n_tiles
        n_used = tile0

        def spare(k, carry):
            tab_ref[0, k] = 0
            tab_ref[1, k] = 0
            tab_ref[2, k] = 0
            return carry
        lax.fori_loop(n_used, nt, spare, 0)
        zero_slots(n_used * tm, nt * tm)

        def rest(k, carry):
            tab_ref[3, k] = n_used
            return carry
        lax.fori_loop(0, nt, rest, 0)

    def body(k, carry):
        for u in range(PLAN_UNROLL):
            t = k * PLAN_UNROLL + u
            src_ref[base_ref[bucket_ref[t]] + rank_ref[t]] = i * ch + t
        return carry
    lax.fori_loop(0, ch // PLAN_UNROLL, body, 0)


def _dispatch_plan(route_i, counts, n_tok):
    tm = TM_FFN
    nt = n_tok // tm + N_BUCKETS
    bucket = route_i[:, 0, :].reshape(n_tok)
    rank = route_i[:, 1, :].reshape(n_tok)
    ch = min(PLAN_CHUNK, n_tok)
    tok = lambda i, *_: (i,)
    src, tab = pl.pallas_call(
        _plan_kernel,
        grid_spec=pltpu.PrefetchScalarGridSpec(
            num_scalar_prefetch=1,
            grid=(n_tok // ch,),
            in_specs=[pl.BlockSpec((ch,), tok, memory_space=pltpu.SMEM),
                      pl.BlockSpec((ch,), tok, memory_space=pltpu.SMEM)],
            out_specs=[pl.BlockSpec(memory_space=pltpu.SMEM), pl.BlockSpec(memory_space=pltpu.SMEM)],
            scratch_shapes=[pltpu.SMEM((N_BUCKETS,), jnp.int32)],
        ),
        out_shape=[jax.ShapeDtypeStruct((nt * tm,), jnp.int32),
                   jax.ShapeDtypeStruct((4, nt), jnp.int32)],
        compiler_params=pltpu.CompilerParams(dimension_semantics=("arbitrary",)),
        name="dispatch_plan",
    )(counts, bucket, rank)
    return tab[0], tab[1], tab[2], tab[3, :1], src.reshape(nt, 1, tm)


def kernel(x, norm_mix_w, w_in, conv_w, conv_b, dt_bias, a_log, d_skip, ssd_norm_w, hgrn_lb_logits,
           hgrn_norm_w, w_out, norm_ffn_w, router_group, router_expert, w_gate, w_up, w_down,
           final_norm_w):
    bsz, seq, _ = x.shape
    depth = w_in.shape[0]
    n_tok = bsz * seq
    assert n_tok % TM_PROJ == 0 and seq % TS_MIX == 0 and n_tok % TM_FFN == 0

    p = jax.nn.softmax(hgrn_lb_logits.astype(F32), axis=0)
    lower_bounds = jnp.cumsum(p, axis=0) - p[0:1]

    o_xbc = SSD_WIDTH
    o_dt = o_xbc + XBC_WIDTH
    o_q = o_dt + SSD_HEADS
    pad_h = lambda v: jnp.pad(v.astype(F32), (0, LANES - SSD_HEADS)).reshape(1, LANES)
    row = lambda v: v.astype(F32).reshape(1, -1)

    xcur = x.reshape(n_tok, D_MODEL)
    for l in range(depth):
        wl = w_in[l].astype(BF16)
        wq, wf, wi, wg = (wl[:, o_q + k * HGRN_WIDTH:o_q + (k + 1) * HGRN_WIDTH] for k in range(4))
        wdt = jnp.pad(wl[:, o_dt:o_q], ((0, 0), (0, LANES - SSD_HEADS)))
        z, xbc, q, f, i, g, dt = _inproj(xcur, n_tok, l > 0, row(norm_mix_w[l]), wl[:, :o_xbc],
                                         wl[:, o_xbc:o_dt], wq, wf, wi, wg, wdt)
        y_ssd = _ssd(z, xbc, dt, bsz, seq, conv_w[l].astype(F32), row(conv_b[l]), dt_bias[l],
                     a_log[l], row(jnp.repeat(d_skip[l], SSD_HEAD_DIM)), row(ssd_norm_w[l]))
        y_hgrn = _hgrn(q, f, i, g, bsz, seq, row(lower_bounds[l]), row(hgrn_norm_w[l]))
        wo = w_out[l].astype(BF16)
        wr_t = jnp.concatenate([router_group[l], router_expert[l]], axis=1).T
        wr_t = jnp.pad(wr_t, ((0, LANES - wr_t.shape[0]), (0, 0))).astype(BF16)
        xe, route, cnt = _outproj(y_ssd, y_hgrn, xcur, n_tok, l > 0, wo[:SSD_WIDTH], wo[SSD_WIDTH:],
                                  row(norm_ffn_w[l]), wr_t)
        counts = cnt[:N_BUCKETS, 0].astype(jnp.int32)
        tile_ea, tile_eb, tile_nvalid, n_used, src = _dispatch_plan(route, counts, n_tok)
        xcur = _ffn(xe, n_tok, tile_ea, tile_eb, tile_nvalid, n_used, src, row(norm_ffn_w[l]),
                    row(final_norm_w), l, w_gate, w_up, w_down, final=(l == depth - 1))
    return xcur.reshape(bsz, seq, D_MODEL)
```

```python
import functools

import jax
import jax.numpy as jnp
from jax import lax
from jax.experimental import pallas as pl
from jax.experimental.pallas import tpu as pltpu

F32 = jnp.float32
BF16 = jnp.bfloat16

D_MODEL = 1024
SSD_HEADS = 8
SSD_HEAD_DIM = 64
SSD_WIDTH = SSD_HEADS * SSD_HEAD_DIM
SSD_GROUPS = 2
SSD_STATE = 128
CONV_WIDTH = 4
XBC_WIDTH = SSD_WIDTH + 2 * SSD_GROUPS * SSD_STATE
HGRN_HEADS = 4
HGRN_HEAD_DIM = 128
HGRN_WIDTH = HGRN_HEADS * HGRN_HEAD_DIM
HGRN_BLOCK = 32
N_EXPERT_GROUPS = 4
EXPERTS_PER_GROUP = 4
N_EXPERTS = N_EXPERT_GROUPS * EXPERTS_PER_GROUP
EXPERT_DIM = 256
EPS = 1e-6

LANES = 128
N_PAIRS = 6
N_BUCKETS = N_EXPERT_GROUPS * N_PAIRS
ROUTE_LANES = LANES
SLAB = D_MODEL // LANES + 1
NEG_BIG = -1e30
LOG2E = 1.4426950408889634

TM_PROJ = 512
TS_MIX = 256
MIX_BATCH = 4
SSD_SUB = 128
HGRN_SUB = 128
CONV_HALO = 8
TM_FFN = 256
PLAN_UNROLL = 32
RANK_BITS = 16
RANK_SPAN = 1 << RANK_BITS
PLAN_CHUNK = 2048
VMEM_LIMIT = 56 * 1024 * 1024


def _mm(a, b):
    return jnp.dot(a.astype(BF16), b.astype(BF16), preferred_element_type=F32)


def _mm_nt(a, b):
    return lax.dot_general(a.astype(BF16), b.astype(BF16), (((1,), (1,)), ((), ())),
                           preferred_element_type=F32)


def _mm_tn(a, b):
    return lax.dot_general(a.astype(BF16), b.astype(BF16), (((0,), (0,)), ((), ())),
                           preferred_element_type=F32)


def _split3(x):
    p1 = x.astype(BF16)
    r1 = x - p1.astype(F32)
    p2 = r1.astype(BF16)
    p3 = (r1 - p2.astype(F32)).astype(BF16)
    return p1, p2, p3


def _cumsum_mm(tri, x):
    p1, p2, p3 = _split3(x)
    acc = jnp.dot(tri, p1, preferred_element_type=F32)
    acc = acc + jnp.dot(tri, p2, preferred_element_type=F32)
    return acc + jnp.dot(tri, p3, preferred_element_type=F32)


def _sigmoid(x):
    return 0.5 * jnp.tanh(0.5 * x) + 0.5


def _silu(x):
    hx = 0.5 * x
    return hx + hx * jnp.tanh(hx)


def _rms(x, w):
    return x * lax.rsqrt(jnp.mean(x * x, axis=-1, keepdims=True) + EPS) * w


def _slab_load(ref, rows):
    return jnp.concatenate([ref[pl.ds(c, rows, stride=SLAB), :] for c in range(D_MODEL // LANES)],
                           axis=1)


def _x_spec(rows, slab, index):
    if slab:
        return pl.BlockSpec((rows * SLAB, LANES), index)
    return pl.BlockSpec((rows, D_MODEL), index)


def _inproj_kernel(x_ref, nw_ref, wz_ref, wxbc_ref, wq_ref, wf_ref, wi_ref, wg_ref, wdt_ref,
                   z_ref, xbc_ref, q_ref, f_ref, i_ref, g_ref, dt_ref, *, slab):
    x = _slab_load(x_ref, z_ref.shape[0]) if slab else x_ref[...]
    h = _rms(x, nw_ref[...]).astype(BF16)
    for w_ref, o_ref in ((wz_ref, z_ref), (wxbc_ref, xbc_ref), (wq_ref, q_ref),
                         (wf_ref, f_ref), (wi_ref, i_ref), (wg_ref, g_ref), (wdt_ref, dt_ref)):
        o_ref[...] = jnp.dot(h, w_ref[...], preferred_element_type=F32).astype(o_ref.dtype)


def _inproj(x, n_tok, slab, nw, wz, wxbc, wq, wf, wi, wg, wdt):
    tm = TM_PROJ
    row = lambda i: (i, 0)
    fixed = lambda i: (0, 0)
    widths = (SSD_WIDTH, XBC_WIDTH, HGRN_WIDTH, HGRN_WIDTH, HGRN_WIDTH, HGRN_WIDTH, LANES)
    dtypes = (BF16,) * 6 + (F32,)
    return pl.pallas_call(
        functools.partial(_inproj_kernel, slab=slab),
        grid=(n_tok // tm,),
        in_specs=[_x_spec(tm, slab, row), pl.BlockSpec((1, D_MODEL), fixed)]
        + [pl.BlockSpec((D_MODEL, w), fixed) for w in widths],
        out_specs=[pl.BlockSpec((tm, w), row) for w in widths],
        out_shape=[jax.ShapeDtypeStruct((n_tok, w), dt) for w, dt in zip(widths, dtypes)],
        compiler_params=pltpu.CompilerParams(dimension_semantics=("arbitrary",),
                                             vmem_limit_bytes=VMEM_LIMIT),
        name="inproj",
    )(x, nw, wz, wxbc, wq, wf, wi, wg, wdt)


def _ssd_kernel(z_ref, xbc_ref, dt_ref, cw_ref, cb_ref, dtb_ref, alog_ref, dsk_ref, nw_ref, ex_ref,
                y_ref, ext_ref, h_ref):
    nbat, ts, _ = z_ref.shape
    L = SSD_SUB
    hpg = SSD_HEADS // SSD_GROUPS
    gw = SSD_WIDTH // SSD_GROUPS
    nh = SSD_HEADS
    pre = CONV_HALO
    streams = [(bb, c) for bb in range(nbat) for c in range(ts // L)]

    nlc = XBC_WIDTH // LANES

    @pl.when(pl.program_id(1) == 0)
    def _():
        ext_ref[...] = jnp.zeros_like(ext_ref)
        h_ref[...] = jnp.zeros_like(h_ref)

    for bb in range(nbat):
        for j in range(nlc):
            ext_ref[bb, j, pl.ds(pre, ts, stride=2), :] = xbc_ref[bb, :, j * LANES:(j + 1) * LANES].astype(F32)

    r_i = lax.broadcasted_iota(jnp.int32, (L, L), 0)
    c_i = lax.broadcasted_iota(jnp.int32, (L, L), 1)
    causal = r_i >= c_i
    triu = (r_i <= c_i).astype(BF16)
    even = lax.broadcasted_iota(jnp.int32, (L, LANES), 1) < SSD_HEAD_DIM

    lane_tile = lambda v: jnp.concatenate([v] * (ts // LANES), axis=1)
    dt_all, da_all, ldt_all = {}, {}, {}
    for bb in range(nbat):
        dt_raw = dt_ref[bb].T[0:nh, :] + lane_tile(dtb_ref[...])
        dt_all[bb] = jnp.maximum(dt_raw, 0.0) + jnp.log(1.0 + jnp.exp(-jnp.abs(dt_raw)))
        da_all[bb] = dt_all[bb] * lane_tile(-jnp.exp(alog_ref[...]) * LOG2E)
        ldt_all[bb] = jnp.log2(dt_all[bb])

    conv = {}
    for bb, c in streams:
        r0 = c * L
        pieces = []
        for j in range(nlc):
            ls = slice(j * LANES, (j + 1) * LANES)
            acc = cb_ref[:, ls]
            for k in range(CONV_WIDTH):
                tap = ext_ref[bb, j, pl.ds(pre + 2 * (r0 - (CONV_WIDTH - 1 - k)), L, stride=2), :]
                acc = acc + cw_ref[k:k + 1, ls] * tap
            pieces.append(_silu(acc))
        conv[bb, c] = jnp.concatenate(pieces, axis=1)

    cols, key_row, e_cum, xsc = {}, {}, {}, {}
    for bb, c in streams:
        r0 = c * L
        dt = dt_all[bb][:, r0:r0 + L]
        p1, p2, p3 = _split3(da_all[bb][:, r0:r0 + L])
        cum = (jnp.dot(p1, triu, preferred_element_type=F32) + jnp.dot(p2, triu, preferred_element_type=F32)
               + jnp.dot(p3, triu, preferred_element_type=F32))
        cum_last = cum[:, L - 1:L]
        key_row[bb, c] = cum - ldt_all[bb][:, r0:r0 + L]
        rows = jnp.concatenate([cum, jnp.exp2(cum), jnp.exp2(cum_last - cum) * dt,
                                jnp.zeros((LANES - 3 * nh, L), F32)], axis=0)
        cols[bb, c] = rows.T
        hi = cols[bb, c].astype(BF16)
        lo = (cols[bb, c] - hi.astype(F32)).astype(BF16)
        spread = jnp.dot(jnp.concatenate([hi, lo], axis=1), ex_ref[...],
                         preferred_element_type=F32)
        e_cum[bb, c] = spread[:, :SSD_WIDTH]
        xsc[bb, c] = conv[bb, c][:, :SSD_WIDTH] * spread[:, SSD_WIDTH:]

    y_intra = {}
    for bb, c in streams:
        xs = conv[bb, c][:, :SSD_WIDTH]
        for g in range(SSD_GROUPS):
            lo_b = SSD_WIDTH + g * SSD_STATE
            lo_c = SSD_WIDTH + (SSD_GROUPS + g) * SSD_STATE
            cb = _mm_nt(conv[bb, c][:, lo_c:lo_c + SSD_STATE], conv[bb, c][:, lo_b:lo_b + SSD_STATE])
            y_parts = []
            for p in range(hpg // 2):
                h0 = g * hpg + 2 * p
                xp = xs[:, h0 * SSD_HEAD_DIM:(h0 + 2) * SSD_HEAD_DIM]
                scores = []
                for hh in (h0, h0 + 1):
                    seg = cols[bb, c][:, hh:hh + 1] - key_row[bb, c][hh:hh + 1, :]
                    scores.append((cb * jnp.exp2(jnp.where(causal, seg, NEG_BIG))).astype(BF16))
                rhs = jnp.concatenate([jnp.where(even, xp, 0.0), jnp.where(even, 0.0, xp)],
                                      axis=0).astype(BF16)
                y_parts.append(jnp.dot(jnp.concatenate(scores, axis=1), rhs,
                                       preferred_element_type=F32))
            y_intra[bb, c, g] = jnp.concatenate(y_parts, axis=1)

    for c in range(ts // L):
        r0 = c * L
        for bb in range(nbat):
            xs = conv[bb, c][:, :SSD_WIDTH]
            zc = z_ref[bb, r0:r0 + L, :].astype(F32)
            e_last = e_cum[bb, c][L - 1:L, :]
            for g in range(SSD_GROUPS):
                gs = slice(g * gw, (g + 1) * gw)
                lo_b = SSD_WIDTH + g * SSD_STATE
                lo_c = SSD_WIDTH + (SSD_GROUPS + g) * SSD_STATE
                h_t = h_ref[bb, g]
                yg = (y_intra[bb, c, g] + _mm(conv[bb, c][:, lo_c:lo_c + SSD_STATE], h_t) * e_cum[bb, c][:, gs]
                      + dsk_ref[:, gs] * xs[:, gs])
                h_ref[bb, g] = h_t * e_last[:, gs] + _mm_tn(conv[bb, c][:, lo_b:lo_b + SSD_STATE],
                                                            xsc[bb, c][:, gs])
                yg = _rms(yg * _silu(zc[:, gs]), nw_ref[:, gs])
                y_ref[bb, r0:r0 + L, gs] = yg.astype(y_ref.dtype)

    ext_ref[:, :, 0:pre, :] = ext_ref[:, :, 2 * ts:2 * ts + pre, :]


def _head_spread_matrix():
    col = jnp.arange(LANES)[:, None]
    out = jnp.arange(2 * SSD_WIDTH)[None, :]
    want = SSD_HEADS * (1 + out // SSD_WIDTH) + (out % SSD_WIDTH) // SSD_HEAD_DIM
    once = (col == want).astype(BF16)
    return jnp.concatenate([once, once], axis=0)


def _ssd(z, xbc, dt, bsz, seq, cw, cb, dtb, alog, dsk, nw):
    ts = TS_MIX
    nbat = MIX_BATCH
    tile = lambda b, s: (b, s, 0)
    fixed = lambda b, s: (0, 0)
    as3d = lambda v: v.reshape(bsz, seq, v.shape[-1])
    per_head_rows = lambda v: jnp.broadcast_to(v.astype(F32)[:, None], (SSD_HEADS, LANES))
    out = pl.pallas_call(
        _ssd_kernel,
        grid=(bsz // nbat, seq // ts),
        in_specs=[pl.BlockSpec((nbat, ts, SSD_WIDTH), tile), pl.BlockSpec((nbat, ts, XBC_WIDTH), tile),
                  pl.BlockSpec((nbat, ts, LANES), tile),
                  pl.BlockSpec((CONV_WIDTH, XBC_WIDTH), fixed), pl.BlockSpec((1, XBC_WIDTH), fixed),
                  pl.BlockSpec((SSD_HEADS, LANES), fixed), pl.BlockSpec((SSD_HEADS, LANES), fixed),
                  pl.BlockSpec((1, SSD_WIDTH), fixed), pl.BlockSpec((1, SSD_WIDTH), fixed),
                  pl.BlockSpec((2 * LANES, 2 * SSD_WIDTH), fixed)],
        out_specs=pl.BlockSpec((nbat, ts, SSD_WIDTH), tile),
        out_shape=jax.ShapeDtypeStruct((bsz, seq, SSD_WIDTH), BF16),
        scratch_shapes=[pltpu.VMEM((nbat, XBC_WIDTH // LANES, 2 * ts + CONV_HALO, LANES), F32),
                        pltpu.VMEM((nbat, SSD_GROUPS, SSD_STATE, SSD_WIDTH // SSD_GROUPS), F32)],
        compiler_params=pltpu.CompilerParams(dimension_semantics=("arbitrary", "arbitrary"),
                                             vmem_limit_bytes=VMEM_LIMIT),
        name="ssd",
    )(as3d(z), as3d(xbc), as3d(dt), cw, cb, per_head_rows(dtb), per_head_rows(alog), dsk, nw,
      _head_spread_matrix())
    return out.reshape(bsz * seq, SSD_WIDTH)


def _hgrn_body(q_ref, f_ref, i_ref, g_ref, lb_ref, nw_ref, o_ref, st_ref):
    ts = q_ref.shape[0]
    blk = HGRN_BLOCK
    nb = ts // blk
    hd = HGRN_HEAD_DIM

    @pl.when(pl.program_id(1) == 0)
    def _():
        st_ref[...] = jnp.zeros_like(st_ref)

    sub = HGRN_SUB
    r_i = lax.broadcasted_iota(jnp.int32, (sub, sub), 0)
    c_i = lax.broadcasted_iota(jnp.int32, (sub, sub), 1)
    blk_causal = (r_i >= c_i) & ((r_i // blk) == (c_i // blk))
    row_in_blk = lax.broadcasted_iota(jnp.int32, (ts, hd), 0) % blk

    heads = []
    for h in range(HGRN_HEADS):
        sl = slice(h * hd, (h + 1) * hd)
        lb = lb_ref[:, sl]
        forget = lb + (1.0 - lb) * _sigmoid(f_ref[:, sl].astype(F32))
        kk = 1.0 - forget
        cum = jnp.log(forget)
        shift = 1
        while shift < blk:
            cum = cum + jnp.where(row_in_blk >= shift, pltpu.roll(cum, shift, axis=0), 0.0)
            shift *= 2
        cum = cum * LOG2E
        cum3 = cum.reshape(nb, blk, hd)
        b_mid = cum3[:, blk // 2:blk // 2 + 1, :]
        b_end = cum3[:, blk - 1:blk, :]
        rel = (cum3 - b_mid).reshape(ts, hd)
        to_end = (b_end - cum3).reshape(ts, hd)
        q = q_ref[:, sl].astype(F32)
        v = i_ref[:, sl].astype(BF16)
        heads.append(dict(sl=sl, v=v, o_intra=[], o_inter=[],
                          qs=(q * jnp.exp2(rel)).astype(BF16),
                          ks=(kk * jnp.exp2(-rel)).astype(BF16),
                          q_dec=(q * jnp.exp2(cum)).astype(BF16),
                          k_end=(kk * jnp.exp2(to_end)).astype(BF16),
                          dec=jnp.exp2(b_end), st=st_ref[h]))

    def pair_diag(x0, x1):
        z = jnp.zeros_like(x0)
        return jnp.concatenate([jnp.concatenate([x0, z], axis=1),
                                jnp.concatenate([z, x1], axis=1)], axis=0)

    pairs = [(heads[i], heads[i + 1]) for i in range(0, HGRN_HEADS, 2)]
    mask2 = jnp.concatenate([blk_causal, blk_causal], axis=1)

    for a, b in pairs:
        for c in range(ts // sub):
            cs = slice(c * sub, (c + 1) * sub)
            sc = _mm_nt(jnp.concatenate([a["qs"][cs], b["qs"][cs]], axis=1),
                        pair_diag(a["ks"][cs], b["ks"][cs]))
            sc = jnp.where(mask2, sc, 0.0).astype(BF16)
            o = jnp.dot(sc, pair_diag(a["v"][cs], b["v"][cs]), preferred_element_type=F32)
            a["o_intra"].append(o[:, :hd])
            b["o_intra"].append(o[:, hd:])

    for n in range(nb):
        rs = slice(n * blk, (n + 1) * blk)
        for a, b in pairs:
            o = _mm_nt(jnp.concatenate([a["q_dec"][rs], b["q_dec"][rs]], axis=1),
                       pair_diag(a["st"].astype(BF16), b["st"].astype(BF16)))
            a["o_inter"].append(o[:, :hd])
            b["o_inter"].append(o[:, hd:])
            for hh in (a, b):
                hh["st"] = hh["st"] * hh["dec"][n] + _mm_tn(hh["v"][rs], hh["k_end"][rs])

    for h, hh in enumerate(heads):
        sl = hh["sl"]
        st_ref[h] = hh["st"]
        o = jnp.concatenate(hh["o_intra"], axis=0) + jnp.concatenate(hh["o_inter"], axis=0)
        o = _rms(o, nw_ref[:, sl]) * _silu(g_ref[:, sl].astype(F32))
        o_ref[:, sl] = o.astype(o_ref.dtype)


def _hgrn_kernel(q_ref, f_ref, i_ref, g_ref, lb_ref, nw_ref, o_ref, st_ref):
    for bb in range(q_ref.shape[0]):
        _hgrn_body(q_ref.at[bb], f_ref.at[bb], i_ref.at[bb], g_ref.at[bb], lb_ref, nw_ref,
                   o_ref.at[bb], st_ref.at[bb])


def _hgrn(q, f, i, g, bsz, seq, lb, nw):
    ts = TS_MIX
    nbat = MIX_BATCH
    tile = lambda b, s: (b, s, 0)
    fixed = lambda b, s: (0, 0)
    as3d = lambda v: v.reshape(bsz, seq, v.shape[-1])
    out = pl.pallas_call(
        _hgrn_kernel,
        grid=(bsz // nbat, seq // ts),
        in_specs=[pl.BlockSpec((nbat, ts, HGRN_WIDTH), tile)] * 4
        + [pl.BlockSpec((1, HGRN_WIDTH), fixed)] * 2,
        out_specs=pl.BlockSpec((nbat, ts, HGRN_WIDTH), tile),
        out_shape=jax.ShapeDtypeStruct((bsz, seq, HGRN_WIDTH), BF16),
        scratch_shapes=[pltpu.VMEM((nbat, HGRN_HEADS, HGRN_HEAD_DIM, HGRN_HEAD_DIM), F32)],
        compiler_params=pltpu.CompilerParams(dimension_semantics=("arbitrary", "arbitrary"),
                                             vmem_limit_bytes=VMEM_LIMIT),
        name="hgrn",
    )(as3d(q), as3d(f), as3d(i), as3d(g), lb, nw)
    return out.reshape(bsz * seq, HGRN_WIDTH)


def _first_argmax4(v):
    m = jnp.maximum(jnp.maximum(v[0], v[1]), jnp.maximum(v[2], v[3]))
    idx = jnp.where(v[0] == m, 0.0, jnp.where(v[1] == m, 1.0, jnp.where(v[2] == m, 2.0, 3.0)))
    return m, idx


def _outproj_kernel(ys_ref, yh_ref, x_ref, wo1_ref, wo2_ref, nw_ref, wr_ref, xe_ref, rt_ref, cnt_ref,
                    base_ref, *, slab):
    tm = ys_ref.shape[0]

    @pl.when(pl.program_id(0) == 0)
    def _():
        base_ref[...] = jnp.zeros_like(base_ref)

    x = _slab_load(x_ref, tm) if slab else x_ref[...]
    xm = (x + jnp.dot(ys_ref[...], wo1_ref[...], preferred_element_type=F32)
          + jnp.dot(yh_ref[...], wo2_ref[...], preferred_element_type=F32))
    h = _rms(xm, nw_ref[...])
    lt = _mm_nt(wr_ref[...], h)
    row = lambda r: lt[r:r + 1, :]
    gl = [row(j) for j in range(N_EXPERT_GROUPS)]
    g_max, g_idx = _first_argmax4(gl)
    g_w = 1.0 / (jnp.exp(gl[0] - g_max) + jnp.exp(gl[1] - g_max)
                 + jnp.exp(gl[2] - g_max) + jnp.exp(gl[3] - g_max))
    el = []
    for j in range(EXPERTS_PER_GROUP):
        base = N_EXPERT_GROUPS + j
        el.append(jnp.where(g_idx == 0.0, row(base),
                  jnp.where(g_idx == 1.0, row(base + EXPERTS_PER_GROUP),
                  jnp.where(g_idx == 2.0, row(base + 2 * EXPERTS_PER_GROUP),
                            row(base + 3 * EXPERTS_PER_GROUP)))))
    e1, i1 = _first_argmax4(el)
    el2 = [jnp.where(i1 == float(j), NEG_BIG, el[j]) for j in range(EXPERTS_PER_GROUP)]
    e2, i2 = _first_argmax4(el2)
    w1 = 1.0 / (1.0 + jnp.exp(e2 - e1))
    w2 = jnp.exp(e2 - e1) * w1
    lo = jnp.minimum(i1, i2)
    hi = jnp.maximum(i1, i2)
    w_lo = g_w * jnp.where(i1 < i2, w1, w2)
    w_hi = g_w * jnp.where(i1 < i2, w2, w1)
    pair = lo * (7.0 - lo) * 0.5 + (hi - lo - 1.0)
    bucket = g_idx * float(N_PAIRS) + pair
    sub_l = lax.broadcasted_iota(jnp.int32, (ROUTE_LANES, tm), 0)
    onehot = jnp.where(sub_l.astype(F32) == bucket, 1.0, 0.0)
    r_i = lax.broadcasted_iota(jnp.int32, (tm, tm), 0)
    c_i = lax.broadcasted_iota(jnp.int32, (tm, tm), 1)
    before = jnp.dot(onehot.astype(BF16), (r_i < c_i).astype(BF16), preferred_element_type=F32)
    base = base_ref[...]
    before = before + jnp.concatenate([base] * (tm // LANES), axis=1)
    rank = jnp.sum(onehot * before, axis=0, keepdims=True)
    base = base + jnp.sum(onehot, axis=1, keepdims=True)
    base_ref[...] = base
    cnt_ref[...] = base
    sub = lax.broadcasted_iota(jnp.int32, (8, tm), 0)
    rt_ref[0] = jnp.where(sub == 0, bucket * float(RANK_SPAN) + rank, 0.0).astype(jnp.int32)
    wide = jnp.where(sub_l == 0, bucket, jnp.where(sub_l == 1, w_lo, jnp.where(sub_l == 2, w_hi, 0.0)))
    for c in range(D_MODEL // LANES):
        xe_ref[pl.ds(c, tm, stride=SLAB), :] = xm[:, c * LANES:(c + 1) * LANES]
    xe_ref[pl.ds(D_MODEL // LANES, tm, stride=SLAB), :] = wide.T


def _outproj(ys, yh, x, n_tok, slab, wo1, wo2, nw, wr_t):
    tm = TM_PROJ
    row = lambda i: (i, 0)
    fixed = lambda i: (0, 0)
    return pl.pallas_call(
        functools.partial(_outproj_kernel, slab=slab),
        grid=(n_tok // tm,),
        in_specs=[pl.BlockSpec((tm, SSD_WIDTH), row), pl.BlockSpec((tm, HGRN_WIDTH), row),
                  _x_spec(tm, slab, row),
                  pl.BlockSpec((SSD_WIDTH, D_MODEL), fixed), pl.BlockSpec((HGRN_WIDTH, D_MODEL), fixed),
                  pl.BlockSpec((1, D_MODEL), fixed), pl.BlockSpec((LANES, D_MODEL), fixed)],
        out_specs=[pl.BlockSpec((tm * SLAB, LANES), row), pl.BlockSpec((1, 8, tm), lambda i: (i, 0, 0)),
                   pl.BlockSpec((ROUTE_LANES, LANES), fixed)],
        out_shape=[jax.ShapeDtypeStruct((n_tok * SLAB, LANES), F32),
                   jax.ShapeDtypeStruct((n_tok // tm, 8, tm), jnp.int32),
                   jax.ShapeDtypeStruct((ROUTE_LANES, LANES), F32)],
        scratch_shapes=[pltpu.VMEM((ROUTE_LANES, LANES), F32)],
        compiler_params=pltpu.CompilerParams(dimension_semantics=("arbitrary",),
                                             vmem_limit_bytes=VMEM_LIMIT),
        name="outproj_router",
    )(ys, yh, x, wo1, wo2, nw, wr_t)


def _ffn_kernel(ea_ref, eb_ref, nvalid_ref, nused_ref,
                src0_ref, src1_ref, src2_ref, xe_hbm, nw_ref, fw_ref,
                wga_ref, wgb_ref, wua_ref, wub_ref, wda_ref, wdb_ref,
                out_hbm, xbuf0, xbuf1, xbuf2, obuf, gsem, ssem, *, final):
    del ea_ref, eb_ref
    tm = TM_FFN
    j = pl.program_id(0)
    nt = pl.num_programs(0)
    n_used = nused_ref[0]
    xbufs = (xbuf0, xbuf1, xbuf2)
    nx = len(xbufs)
    orow = 1 if final else SLAB

    def issue_gather(src_ref, q):
        for r in range(tm):
            pltpu.make_async_copy(xe_hbm.at[pl.ds(src_ref[0, 0, r] * SLAB, SLAB)],
                                  xbufs[q].at[pl.ds(r * SLAB, SLAB)], gsem.at[q]).start(priority=r % 2)

    def wait_gather(q):
        pltpu.make_async_copy(xe_hbm.at[pl.ds(0, tm * SLAB)], xbufs[q], gsem.at[q]).wait()

    def scatter_copy(r, p):
        return pltpu.make_async_copy(obuf.at[p, pl.ds(r * orow, orow)],
                                     out_hbm.at[pl.ds(src0_ref[0, 0, r] * orow, orow)], ssem.at[p])

    def wait_scatter(p, tile):
        n = nvalid_ref[tile]
        n8 = pl.multiple_of((n // 8) * 8, 8)

        @pl.when(n8 > 0)
        def _():
            rows = pl.multiple_of(n8 * orow, 8)
            pltpu.make_async_copy(obuf.at[p, pl.ds(0, rows)], out_hbm.at[pl.ds(0, rows)],
                                  ssem.at[p]).wait()

        def body(r, carry):
            pltpu.make_async_copy(obuf.at[p, pl.ds(0, orow)], out_hbm.at[pl.ds(0, orow)],
                                  ssem.at[p]).wait()
            return carry
        lax.fori_loop(0, n - n8, body, 0)

    @pl.when((j == 0) & (n_used > 0))
    def _():
        issue_gather(src0_ref, 0)
        issue_gather(src1_ref, 1)

    def step(q):
        p = j % 2

        @pl.when(j >= 2)
        def _():
            wait_scatter(p, j - 2)

        wait_gather(q)
        issue_gather(src2_ref, (q + 2) % nx)
        xm = _slab_load(xbufs[q], tm)
        route = xbufs[q][pl.ds(D_MODEL // LANES, tm, stride=SLAB), :]
        w_lo = route[:, 1:2]
        w_hi = route[:, 2:3]
        h = _rms(xm, nw_ref[...]).astype(BF16)
        wt = lambda w_ref: w_ref[0, 0].astype(BF16)
        hid_a = _silu(jnp.dot(h, wt(wga_ref), preferred_element_type=F32)) \
            * jnp.dot(h, wt(wua_ref), preferred_element_type=F32)
        hid_b = _silu(jnp.dot(h, wt(wgb_ref), preferred_element_type=F32)) \
            * jnp.dot(h, wt(wub_ref), preferred_element_type=F32)
        y = (jnp.dot((hid_a * w_lo).astype(BF16), wt(wda_ref), preferred_element_type=F32)
             + jnp.dot((hid_b * w_hi).astype(BF16), wt(wdb_ref), preferred_element_type=F32))
        out = xm + y
        if final:
            obuf[p] = _rms(out, fw_ref[...])
        else:
            for c in range(D_MODEL // LANES):
                obuf[p, pl.ds(c, tm, stride=SLAB), :] = out[:, c * LANES:(c + 1) * LANES]
            obuf[p, pl.ds(D_MODEL // LANES, tm, stride=SLAB), :] = route

        n_valid = nvalid_ref[j]

        @pl.when(n_valid == tm)
        def _():
            for r in range(tm):
                scatter_copy(r, p).start(priority=r % 2)

        @pl.when(n_valid < tm)
        def _():
            def single(r, carry):
                scatter_copy(r, p).start()
                return carry
            lax.fori_loop(0, n_valid, single, 0)

    for q in range(nx):
        @pl.when((j % nx == q) & (j < n_used))
        def _():
            step(q)

    @pl.when((j == nt - 1) & (n_used >= 1))
    def _():
        for q in range(nx):
            @pl.when((n_used % nx == q) | ((n_used + 1) % nx == q))
            def _():
                wait_gather(q)

        wait_scatter((n_used - 1) % 2, n_used - 1)

        @pl.when(n_used >= 2)
        def _():
            wait_scatter(n_used % 2, n_used - 2)


def _ffn(xe, n_tok, tile_ea, tile_eb, tile_nvalid, n_used, src, nw, fw, layer, wg, wu, wd, final):
    tm = TM_FFN
    nt = src.shape[0]
    obuf_shape = (tm, D_MODEL) if final else (tm * SLAB, LANES)
    out_rows = (n_tok, D_MODEL) if final else (n_tok * SLAB, LANES)
    clamp = lambda j, nu: jnp.minimum(j, jnp.maximum(nu[0] - 1, 0))
    wa = lambda j, ea, eb, nv, nu: (layer, ea[clamp(j, nu)], 0, 0)
    wb = lambda j, ea, eb, nv, nu: (layer, eb[clamp(j, nu)], 0, 0)
    fixed = lambda j, ea, eb, nv, nu: (0, 0)
    smem_blk = lambda f: pl.BlockSpec((1, 1, tm), f, memory_space=pltpu.SMEM)
    grid_spec = pltpu.PrefetchScalarGridSpec(
        num_scalar_prefetch=4,
        grid=(nt,),
        in_specs=[smem_blk(lambda j, ea, eb, nv, nu: (j, 0, 0)),
                  smem_blk(lambda j, ea, eb, nv, nu: (jnp.minimum(j + 1, nt - 1), 0, 0)),
                  smem_blk(lambda j, ea, eb, nv, nu: (jnp.minimum(j + 2, nt - 1), 0, 0)),
                  pl.BlockSpec(memory_space=pl.ANY),
                  pl.BlockSpec((1, D_MODEL), fixed), pl.BlockSpec((1, D_MODEL), fixed),
                  pl.BlockSpec((1, 1, D_MODEL, EXPERT_DIM), wa), pl.BlockSpec((1, 1, D_MODEL, EXPERT_DIM), wb),
                  pl.BlockSpec((1, 1, D_MODEL, EXPERT_DIM), wa), pl.BlockSpec((1, 1, D_MODEL, EXPERT_DIM), wb),
                  pl.BlockSpec((1, 1, EXPERT_DIM, D_MODEL), wa), pl.BlockSpec((1, 1, EXPERT_DIM, D_MODEL), wb)],
        out_specs=pl.BlockSpec(memory_space=pl.ANY),
        scratch_shapes=[pltpu.VMEM((tm * SLAB, LANES), F32)] * 3
        + [pltpu.VMEM((2,) + obuf_shape, F32),
           pltpu.SemaphoreType.DMA((3,)), pltpu.SemaphoreType.DMA((2,))],
    )
    return pl.pallas_call(
        functools.partial(_ffn_kernel, final=final),
        grid_spec=grid_spec,
        out_shape=jax.ShapeDtypeStruct(out_rows, F32),
        compiler_params=pltpu.CompilerParams(dimension_semantics=("arbitrary",),
                                             vmem_limit_bytes=VMEM_LIMIT),
        name="moe_ffn",
    )(tile_ea, tile_eb, tile_nvalid, n_used, src, src, src, xe, nw, fw, wg, wg, wu, wu, wd, wd)


def _pair_tables():
    lo, hi = [], []
    for a in range(EXPERTS_PER_GROUP):
        for b in range(a + 1, EXPERTS_PER_GROUP):
            lo.append(a)
            hi.append(b)
    return lo, hi


def _plan_kernel(counts_ref, key_ref, src_ref, tab_ref, base_ref):
    i = pl.program_id(0)
    ch = key_ref.shape[0]
    tm = TM_FFN
    nt = tab_ref.shape[1]
    lo, hi = _pair_tables()

    def zero_slots(first, last):
        def zero(k, carry):
            src_ref[k] = 0
            return carry
        lax.fori_loop(first, last, zero, 0)

    @pl.when(i == 0)
    def _():
        tile0 = 0
        for b in range(N_BUCKETS):
            cnt = counts_ref[b]
            n_tiles = (cnt + (tm - 1)) // tm
            base_ref[b] = tile0 * tm
            zero_slots(tile0 * tm + cnt, (tile0 + n_tiles) * tm)
            ea = (b // N_PAIRS) * EXPERTS_PER_GROUP + lo[b % N_PAIRS]
            eb = (b // N_PAIRS) * EXPERTS_PER_GROUP + hi[b % N_PAIRS]

            def tile(k, carry, tile0=tile0, cnt=cnt, ea=ea, eb=eb):
                tab_ref[0, tile0 + k] = ea
                tab_ref[1, tile0 + k] = eb
                tab_ref[2, tile0 + k] = jnp.minimum(cnt - k * tm, tm)
                return carry
            lax.fori_loop(0, n_tiles, tile, 0)
            tile0 = tile0 + n_tiles
        n_used = tile0

        def spare(k, carry):
            tab_ref[0, k] = 0
            tab_ref[1, k] = 0
            tab_ref[2, k] = 0
            return carry
        lax.fori_loop(n_used, nt, spare, 0)
        zero_slots(n_used * tm, nt * tm)

        def rest(k, carry):
            tab_ref[3, k] = n_used
            return carry
        lax.fori_loop(0, nt, rest, 0)

    def body(k, carry):
        for u in range(PLAN_UNROLL):
            t = k * PLAN_UNROLL + u
            key = key_ref[t]
            src_ref[base_ref[key >> RANK_BITS] + (key & (RANK_SPAN - 1))] = i * ch + t
        return carry
    lax.fori_loop(0, ch // PLAN_UNROLL, body, 0)


def _dispatch_plan(route_i, counts, n_tok):
    tm = TM_FFN
    nt = n_tok // tm + N_BUCKETS
    assert n_tok <= RANK_SPAN
    key = route_i[:, 0, :].reshape(n_tok)
    ch = min(PLAN_CHUNK, n_tok)
    tok = lambda i, *_: (i,)
    src, tab = pl.pallas_call(
        _plan_kernel,
        grid_spec=pltpu.PrefetchScalarGridSpec(
            num_scalar_prefetch=1,
            grid=(n_tok // ch,),
            in_specs=[pl.BlockSpec((ch,), tok, memory_space=pltpu.SMEM)],
            out_specs=[pl.BlockSpec(memory_space=pltpu.SMEM), pl.BlockSpec(memory_space=pltpu.SMEM)],
            scratch_shapes=[pltpu.SMEM((N_BUCKETS,), jnp.int32)],
        ),
        out_shape=[jax.ShapeDtypeStruct((nt * tm,), jnp.int32),
                   jax.ShapeDtypeStruct((4, nt), jnp.int32)],
        compiler_params=pltpu.CompilerParams(dimension_semantics=("arbitrary",)),
        name="dispatch_plan",
    )(counts, key)
    return tab[0], tab[1], tab[2], tab[3, :1], src.reshape(nt, 1, tm)


def kernel(x, norm_mix_w, w_in, conv_w, conv_b, dt_bias, a_log, d_skip, ssd_norm_w, hgrn_lb_logits,
           hgrn_norm_w, w_out, norm_ffn_w, router_group, router_expert, w_gate, w_up, w_down,
           final_norm_w):
    bsz, seq, _ = x.shape
    depth = w_in.shape[0]
    n_tok = bsz * seq
    assert n_tok % TM_PROJ == 0 and seq % TS_MIX == 0 and n_tok % TM_FFN == 0

    p = jax.nn.softmax(hgrn_lb_logits.astype(F32), axis=0)
    lower_bounds = jnp.cumsum(p, axis=0) - p[0:1]

    o_xbc = SSD_WIDTH
    o_dt = o_xbc + XBC_WIDTH
    o_q = o_dt + SSD_HEADS
    pad_h = lambda v: jnp.pad(v.astype(F32), (0, LANES - SSD_HEADS)).reshape(1, LANES)
    row = lambda v: v.astype(F32).reshape(1, -1)

    xcur = x.reshape(n_tok, D_MODEL)
    for l in range(depth):
        wl = w_in[l].astype(BF16)
        wq, wf, wi, wg = (wl[:, o_q + k * HGRN_WIDTH:o_q + (k + 1) * HGRN_WIDTH] for k in range(4))
        wdt = jnp.pad(wl[:, o_dt:o_q], ((0, 0), (0, LANES - SSD_HEADS)))
        z, xbc, q, f, i, g, dt = _inproj(xcur, n_tok, l > 0, row(norm_mix_w[l]), wl[:, :o_xbc],
                                         wl[:, o_xbc:o_dt], wq, wf, wi, wg, wdt)
        y_ssd = _ssd(z, xbc, dt, bsz, seq, conv_w[l].astype(F32), row(conv_b[l]), dt_bias[l],
                     a_log[l], row(jnp.repeat(d_skip[l], SSD_HEAD_DIM)), row(ssd_norm_w[l]))
        y_hgrn = _hgrn(q, f, i, g, bsz, seq, row(lower_bounds[l]), row(hgrn_norm_w[l]))
        wo = w_out[l].astype(BF16)
        wr_t = jnp.concatenate([router_group[l], router_expert[l]], axis=1).T
        wr_t = jnp.pad(wr_t, ((0, LANES - wr_t.shape[0]), (0, 0))).astype(BF16)
        xe, route, cnt = _outproj(y_ssd, y_hgrn, xcur, n_tok, l > 0, wo[:SSD_WIDTH], wo[SSD_WIDTH:],
                                  row(norm_ffn_w[l]), wr_t)
        counts = cnt[:N_BUCKETS, 0].astype(jnp.int32)
        tile_ea, tile_eb, tile_nvalid, n_used, src = _dispatch_plan(route, counts, n_tok)
        xcur = _ffn(xe, n_tok, tile_ea, tile_eb, tile_nvalid, n_used, src, row(norm_ffn_w[l]),
                    row(final_norm_w), l, w_gate, w_up, w_down, final=(l == depth - 1))
    return xcur.reshape(bsz, seq, D_MODEL)
```

```python
import functools

import jax
import jax.numpy as jnp
from jax import lax
from jax.experimental import pallas as pl
from jax.experimental.pallas import tpu as pltpu

F32 = jnp.float32
BF16 = jnp.bfloat16

D_MODEL = 1024
SSD_HEADS = 8
SSD_HEAD_DIM = 64
SSD_WIDTH = SSD_HEADS * SSD_HEAD_DIM
SSD_GROUPS = 2
SSD_STATE = 128
CONV_WIDTH = 4
XBC_WIDTH = SSD_WIDTH + 2 * SSD_GROUPS * SSD_STATE
HGRN_HEADS = 4
HGRN_HEAD_DIM = 128
HGRN_WIDTH = HGRN_HEADS * HGRN_HEAD_DIM
HGRN_BLOCK = 32
N_EXPERT_GROUPS = 4
EXPERTS_PER_GROUP = 4
N_EXPERTS = N_EXPERT_GROUPS * EXPERTS_PER_GROUP
EXPERT_DIM = 256
EPS = 1e-6

LANES = 128
N_PAIRS = 6
N_BUCKETS = N_EXPERT_GROUPS * N_PAIRS
ROUTE_LANES = LANES
SLAB = D_MODEL // LANES + 1
NEG_BIG = -1e30
LOG2E = 1.4426950408889634

TM_PROJ = 512
TS_MIX = 256
MIX_BATCH = 4
SSD_SUB = 128
HGRN_SUB = 128
CONV_HALO = 8
TM_FFN = 256
PLAN_UNROLL = 32
RANK_BITS = 16
RANK_SPAN = 1 << RANK_BITS
PLAN_CHUNK = 2048
VMEM_LIMIT = 56 * 1024 * 1024


def _mm(a, b):
    return jnp.dot(a.astype(BF16), b.astype(BF16), preferred_element_type=F32)


def _mm_nt(a, b):
    return lax.dot_general(a.astype(BF16), b.astype(BF16), (((1,), (1,)), ((), ())),
                           preferred_element_type=F32)


def _mm_tn(a, b):
    return lax.dot_general(a.astype(BF16), b.astype(BF16), (((0,), (0,)), ((), ())),
                           preferred_element_type=F32)


def _split3(x):
    p1 = x.astype(BF16)
    r1 = x - p1.astype(F32)
    p2 = r1.astype(BF16)
    p3 = (r1 - p2.astype(F32)).astype(BF16)
    return p1, p2, p3


def _cumsum_mm(tri, x):
    p1, p2, p3 = _split3(x)
    acc = jnp.dot(tri, p1, preferred_element_type=F32)
    acc = acc + jnp.dot(tri, p2, preferred_element_type=F32)
    return acc + jnp.dot(tri, p3, preferred_element_type=F32)


def _sigmoid(x):
    return 0.5 * jnp.tanh(0.5 * x) + 0.5


def _silu(x):
    hx = 0.5 * x
    return hx + hx * jnp.tanh(hx)


def _rms(x, w):
    return x * lax.rsqrt(jnp.mean(x * x, axis=-1, keepdims=True) + EPS) * w


def _slab_load(ref, rows):
    return jnp.concatenate([ref[pl.ds(c, rows, stride=SLAB), :] for c in range(D_MODEL // LANES)],
                           axis=1)


def _x_spec(rows, slab, index):
    if slab:
        return pl.BlockSpec((rows * SLAB, LANES), index)
    return pl.BlockSpec((rows, D_MODEL), index)


def _inproj_kernel(x_ref, nw_ref, wz_ref, wxbc_ref, wq_ref, wf_ref, wi_ref, wg_ref, wdt_ref,
                   z_ref, xbc_ref, q_ref, f_ref, i_ref, g_ref, dt_ref, *, slab):
    x = _slab_load(x_ref, z_ref.shape[0]) if slab else x_ref[...]
    h = _rms(x, nw_ref[...]).astype(BF16)
    for w_ref, o_ref in ((wz_ref, z_ref), (wxbc_ref, xbc_ref), (wq_ref, q_ref),
                         (wf_ref, f_ref), (wi_ref, i_ref), (wg_ref, g_ref), (wdt_ref, dt_ref)):
        o_ref[...] = jnp.dot(h, w_ref[...], preferred_element_type=F32).astype(o_ref.dtype)


def _inproj(x, n_tok, slab, nw, wz, wxbc, wq, wf, wi, wg, wdt):
    tm = TM_PROJ
    row = lambda i: (i, 0)
    fixed = lambda i: (0, 0)
    widths = (SSD_WIDTH, XBC_WIDTH, HGRN_WIDTH, HGRN_WIDTH, HGRN_WIDTH, HGRN_WIDTH, LANES)
    dtypes = (BF16,) * 6 + (F32,)
    return pl.pallas_call(
        functools.partial(_inproj_kernel, slab=slab),
        grid=(n_tok // tm,),
        in_specs=[_x_spec(tm, slab, row), pl.BlockSpec((1, D_MODEL), fixed)]
        + [pl.BlockSpec((D_MODEL, w), fixed) for w in widths],
        out_specs=[pl.BlockSpec((tm, w), row) for w in widths],
        out_shape=[jax.ShapeDtypeStruct((n_tok, w), dt) for w, dt in zip(widths, dtypes)],
        compiler_params=pltpu.CompilerParams(dimension_semantics=("arbitrary",),
                                             vmem_limit_bytes=VMEM_LIMIT),
        name="inproj",
    )(x, nw, wz, wxbc, wq, wf, wi, wg, wdt)


def _ssd_kernel(z_ref, xbc_ref, dt_ref, cw_ref, cb_ref, dtb_ref, alog_ref, dsk_ref, nw_ref, ex_ref,
                y_ref, ext_ref, h_ref):
    nbat, ts, _ = z_ref.shape
    L = SSD_SUB
    hpg = SSD_HEADS // SSD_GROUPS
    gw = SSD_WIDTH // SSD_GROUPS
    nh = SSD_HEADS
    pre = CONV_HALO
    streams = [(bb, c) for bb in range(nbat) for c in range(ts // L)]

    nlc = XBC_WIDTH // LANES

    @pl.when(pl.program_id(1) == 0)
    def _():
        ext_ref[...] = jnp.zeros_like(ext_ref)
        h_ref[...] = jnp.zeros_like(h_ref)

    for bb in range(nbat):
        for j in range(nlc):
            ext_ref[bb, j, pl.ds(pre, ts, stride=2), :] = xbc_ref[bb, :, j * LANES:(j + 1) * LANES].astype(F32)

    r_i = lax.broadcasted_iota(jnp.int32, (L, L), 0)
    c_i = lax.broadcasted_iota(jnp.int32, (L, L), 1)
    causal = r_i >= c_i
    triu = (r_i <= c_i).astype(BF16)
    even = lax.broadcasted_iota(jnp.int32, (L, LANES), 1) < SSD_HEAD_DIM

    lane_tile = lambda v: jnp.concatenate([v] * (ts // LANES), axis=1)
    dt_all, da_all, ldt_all = {}, {}, {}
    for bb in range(nbat):
        dt_raw = dt_ref[bb].T[0:nh, :] + lane_tile(dtb_ref[...])
        dt_all[bb] = jnp.maximum(dt_raw, 0.0) + jnp.log(1.0 + jnp.exp(-jnp.abs(dt_raw)))
        da_all[bb] = dt_all[bb] * lane_tile(-jnp.exp(alog_ref[...]) * LOG2E)
        ldt_all[bb] = jnp.log2(dt_all[bb])

    conv = {}
    for bb, c in streams:
        r0 = c * L
        pieces = []
        for j in range(nlc):
            ls = slice(j * LANES, (j + 1) * LANES)
            acc = cb_ref[:, ls]
            for k in range(CONV_WIDTH):
                tap = ext_ref[bb, j, pl.ds(pre + 2 * (r0 - (CONV_WIDTH - 1 - k)), L, stride=2), :]
                acc = acc + cw_ref[k:k + 1, ls] * tap
            pieces.append(_silu(acc))
        conv[bb, c] = jnp.concatenate(pieces, axis=1)

    cols, key_row, e_cum, xsc = {}, {}, {}, {}
    for bb, c in streams:
        r0 = c * L
        dt = dt_all[bb][:, r0:r0 + L]
        p1, p2, p3 = _split3(da_all[bb][:, r0:r0 + L])
        cum = (jnp.dot(p1, triu, preferred_element_type=F32) + jnp.dot(p2, triu, preferred_element_type=F32)
               + jnp.dot(p3, triu, preferred_element_type=F32))
        cum_last = cum[:, L - 1:L]
        key_row[bb, c] = cum - ldt_all[bb][:, r0:r0 + L]
        rows = jnp.concatenate([cum, jnp.exp2(cum), jnp.exp2(cum_last - cum) * dt,
                                jnp.zeros((LANES - 3 * nh, L), F32)], axis=0)
        cols[bb, c] = rows.T
        hi = cols[bb, c].astype(BF16)
        lo = (cols[bb, c] - hi.astype(F32)).astype(BF16)
        spread = jnp.dot(jnp.concatenate([hi, lo], axis=1), ex_ref[...],
                         preferred_element_type=F32)
        e_cum[bb, c] = spread[:, :SSD_WIDTH]
        xsc[bb, c] = conv[bb, c][:, :SSD_WIDTH] * spread[:, SSD_WIDTH:]

    y_intra = {}
    for bb, c in streams:
        xs = conv[bb, c][:, :SSD_WIDTH]
        for g in range(SSD_GROUPS):
            lo_b = SSD_WIDTH + g * SSD_STATE
            lo_c = SSD_WIDTH + (SSD_GROUPS + g) * SSD_STATE
            cb = _mm_nt(conv[bb, c][:, lo_c:lo_c + SSD_STATE], conv[bb, c][:, lo_b:lo_b + SSD_STATE])
            y_parts = []
            for p in range(hpg // 2):
                h0 = g * hpg + 2 * p
                xp = xs[:, h0 * SSD_HEAD_DIM:(h0 + 2) * SSD_HEAD_DIM]
                scores = []
                for hh in (h0, h0 + 1):
                    seg = cols[bb, c][:, hh:hh + 1] - key_row[bb, c][hh:hh + 1, :]
                    scores.append((cb * jnp.exp2(jnp.where(causal, seg, NEG_BIG))).astype(BF16))
                rhs = jnp.concatenate([jnp.where(even, xp, 0.0), jnp.where(even, 0.0, xp)],
                                      axis=0).astype(BF16)
                y_parts.append(jnp.dot(jnp.concatenate(scores, axis=1), rhs,
                                       preferred_element_type=F32))
            y_intra[bb, c, g] = jnp.concatenate(y_parts, axis=1)

    for c in range(ts // L):
        r0 = c * L
        for bb in range(nbat):
            xs = conv[bb, c][:, :SSD_WIDTH]
            zc = z_ref[bb, r0:r0 + L, :].astype(F32)
            e_last = e_cum[bb, c][L - 1:L, :]
            for g in range(SSD_GROUPS):
                gs = slice(g * gw, (g + 1) * gw)
                lo_b = SSD_WIDTH + g * SSD_STATE
                lo_c = SSD_WIDTH + (SSD_GROUPS + g) * SSD_STATE
                h_t = h_ref[bb, g]
                yg = (y_intra[bb, c, g] + _mm(conv[bb, c][:, lo_c:lo_c + SSD_STATE], h_t) * e_cum[bb, c][:, gs]
                      + dsk_ref[:, gs] * xs[:, gs])
                h_ref[bb, g] = h_t * e_last[:, gs] + _mm_tn(conv[bb, c][:, lo_b:lo_b + SSD_STATE],
                                                            xsc[bb, c][:, gs])
                yg = _rms(yg * _silu(zc[:, gs]), nw_ref[:, gs])
                y_ref[bb, r0:r0 + L, gs] = yg.astype(y_ref.dtype)

    ext_ref[:, :, 0:pre, :] = ext_ref[:, :, 2 * ts:2 * ts + pre, :]


def _head_spread_matrix():
    col = jnp.arange(LANES)[:, None]
    out = jnp.arange(2 * SSD_WIDTH)[None, :]
    want = SSD_HEADS * (1 + out // SSD_WIDTH) + (out % SSD_WIDTH) // SSD_HEAD_DIM
    once = (col == want).astype(BF16)
    return jnp.concatenate([once, once], axis=0)


def _ssd(z, xbc, dt, bsz, seq, cw, cb, dtb, alog, dsk, nw):
    ts = TS_MIX
    nbat = MIX_BATCH
    tile = lambda b, s: (b, s, 0)
    fixed = lambda b, s: (0, 0)
    as3d = lambda v: v.reshape(bsz, seq, v.shape[-1])
    per_head_rows = lambda v: jnp.broadcast_to(v.astype(F32)[:, None], (SSD_HEADS, LANES))
    out = pl.pallas_call(
        _ssd_kernel,
        grid=(bsz // nbat, seq // ts),
        in_specs=[pl.BlockSpec((nbat, ts, SSD_WIDTH), tile), pl.BlockSpec((nbat, ts, XBC_WIDTH), tile),
                  pl.BlockSpec((nbat, ts, LANES), tile),
                  pl.BlockSpec((CONV_WIDTH, XBC_WIDTH), fixed), pl.BlockSpec((1, XBC_WIDTH), fixed),
                  pl.BlockSpec((SSD_HEADS, LANES), fixed), pl.BlockSpec((SSD_HEADS, LANES), fixed),
                  pl.BlockSpec((1, SSD_WIDTH), fixed), pl.BlockSpec((1, SSD_WIDTH), fixed),
                  pl.BlockSpec((2 * LANES, 2 * SSD_WIDTH), fixed)],
        out_specs=pl.BlockSpec((nbat, ts, SSD_WIDTH), tile),
        out_shape=jax.ShapeDtypeStruct((bsz, seq, SSD_WIDTH), BF16),
        scratch_shapes=[pltpu.VMEM((nbat, XBC_WIDTH // LANES, 2 * ts + CONV_HALO, LANES), F32),
                        pltpu.VMEM((nbat, SSD_GROUPS, SSD_STATE, SSD_WIDTH // SSD_GROUPS), F32)],
        compiler_params=pltpu.CompilerParams(dimension_semantics=("arbitrary", "arbitrary"),
                                             vmem_limit_bytes=VMEM_LIMIT),
        name="ssd",
    )(as3d(z), as3d(xbc), as3d(dt), cw, cb, per_head_rows(dtb), per_head_rows(alog), dsk, nw,
      _head_spread_matrix())
    return out.reshape(bsz * seq, SSD_WIDTH)


def _hgrn_body(q_ref, f_ref, i_ref, g_ref, lb_ref, nw_ref, o_ref, st_ref):
    ts = q_ref.shape[0]
    blk = HGRN_BLOCK
    nb = ts // blk
    hd = HGRN_HEAD_DIM

    @pl.when(pl.program_id(1) == 0)
    def _():
        st_ref[...] = jnp.zeros_like(st_ref)

    sub = HGRN_SUB
    r_i = lax.broadcasted_iota(jnp.int32, (sub, sub), 0)
    c_i = lax.broadcasted_iota(jnp.int32, (sub, sub), 1)
    blk_causal = (r_i >= c_i) & ((r_i // blk) == (c_i // blk))
    row_in_blk = lax.broadcasted_iota(jnp.int32, (ts, hd), 0) % blk

    heads = []
    for h in range(HGRN_HEADS):
        sl = slice(h * hd, (h + 1) * hd)
        lb = lb_ref[:, sl]
        forget = lb + (1.0 - lb) * _sigmoid(f_ref[:, sl].astype(F32))
        kk = 1.0 - forget
        cum = jnp.log(forget)
        shift = 1
        while shift < blk:
            cum = cum + jnp.where(row_in_blk >= shift, pltpu.roll(cum, shift, axis=0), 0.0)
            shift *= 2
        cum = cum * LOG2E
        cum3 = cum.reshape(nb, blk, hd)
        b_mid = cum3[:, blk // 2:blk // 2 + 1, :]
        b_end = cum3[:, blk - 1:blk, :]
        rel = (cum3 - b_mid).reshape(ts, hd)
        to_end = (b_end - cum3).reshape(ts, hd)
        q = q_ref[:, sl].astype(F32)
        v = i_ref[:, sl].astype(BF16)
        heads.append(dict(sl=sl, v=v, o_intra=[], o_inter=[],
                          qs=(q * jnp.exp2(rel)).astype(BF16),
                          ks=(kk * jnp.exp2(-rel)).astype(BF16),
                          q_dec=(q * jnp.exp2(cum)).astype(BF16),
                          k_end=(kk * jnp.exp2(to_end)).astype(BF16),
                          dec=jnp.exp2(b_end), st=st_ref[h]))

    def pair_diag(x0, x1):
        z = jnp.zeros_like(x0)
        return jnp.concatenate([jnp.concatenate([x0, z], axis=1),
                                jnp.concatenate([z, x1], axis=1)], axis=0)

    pairs = [(heads[i], heads[i + 1]) for i in range(0, HGRN_HEADS, 2)]
    mask2 = jnp.concatenate([blk_causal, blk_causal], axis=1)

    for a, b in pairs:
        for c in range(ts // sub):
            cs = slice(c * sub, (c + 1) * sub)
            sc = _mm_nt(jnp.concatenate([a["qs"][cs], b["qs"][cs]], axis=1),
                        pair_diag(a["ks"][cs], b["ks"][cs]))
            sc = jnp.where(mask2, sc, 0.0).astype(BF16)
            o = jnp.dot(sc, pair_diag(a["v"][cs], b["v"][cs]), preferred_element_type=F32)
            a["o_intra"].append(o[:, :hd])
            b["o_intra"].append(o[:, hd:])

    for n in range(nb):
        rs = slice(n * blk, (n + 1) * blk)
        for a, b in pairs:
            o = _mm_nt(jnp.concatenate([a["q_dec"][rs], b["q_dec"][rs]], axis=1),
                       pair_diag(a["st"].astype(BF16), b["st"].astype(BF16)))
            a["o_inter"].append(o[:, :hd])
            b["o_inter"].append(o[:, hd:])
            for hh in (a, b):
                hh["st"] = hh["st"] * hh["dec"][n] + _mm_tn(hh["v"][rs], hh["k_end"][rs])

    for h, hh in enumerate(heads):
        sl = hh["sl"]
        st_ref[h] = hh["st"]
        o = jnp.concatenate(hh["o_intra"], axis=0) + jnp.concatenate(hh["o_inter"], axis=0)
        o = _rms(o, nw_ref[:, sl]) * _silu(g_ref[:, sl].astype(F32))
        o_ref[:, sl] = o.astype(o_ref.dtype)


def _hgrn_kernel(q_ref, f_ref, i_ref, g_ref, lb_ref, nw_ref, o_ref, st_ref):
    for bb in range(q_ref.shape[0]):
        _hgrn_body(q_ref.at[bb], f_ref.at[bb], i_ref.at[bb], g_ref.at[bb], lb_ref, nw_ref,
                   o_ref.at[bb], st_ref.at[bb])


def _hgrn(q, f, i, g, bsz, seq, lb, nw):
    ts = TS_MIX
    nbat = MIX_BATCH
    tile = lambda b, s: (b, s, 0)
    fixed = lambda b, s: (0, 0)
    as3d = lambda v: v.reshape(bsz, seq, v.shape[-1])
    out = pl.pallas_call(
        _hgrn_kernel,
        grid=(bsz // nbat, seq // ts),
        in_specs=[pl.BlockSpec((nbat, ts, HGRN_WIDTH), tile)] * 4
        + [pl.BlockSpec((1, HGRN_WIDTH), fixed)] * 2,
        out_specs=pl.BlockSpec((nbat, ts, HGRN_WIDTH), tile),
        out_shape=jax.ShapeDtypeStruct((bsz, seq, HGRN_WIDTH), BF16),
        scratch_shapes=[pltpu.VMEM((nbat, HGRN_HEADS, HGRN_HEAD_DIM, HGRN_HEAD_DIM), F32)],
        compiler_params=pltpu.CompilerParams(dimension_semantics=("arbitrary", "arbitrary"),
                                             vmem_limit_bytes=VMEM_LIMIT),
        name="hgrn",
    )(as3d(q), as3d(f), as3d(i), as3d(g), lb, nw)
    return out.reshape(bsz * seq, HGRN_WIDTH)


def _first_argmax4(v):
    m = jnp.maximum(jnp.maximum(v[0], v[1]), jnp.maximum(v[2], v[3]))
    idx = jnp.where(v[0] == m, 0.0, jnp.where(v[1] == m, 1.0, jnp.where(v[2] == m, 2.0, 3.0)))
    return m, idx


def _outproj_kernel(ys_ref, yh_ref, x_ref, wo1_ref, wo2_ref, nw_ref, wr_ref, xe_ref, rt_ref, cnt_ref,
                    base_ref, *, slab):
    tm = ys_ref.shape[0]

    @pl.when(pl.program_id(0) == 0)
    def _():
        base_ref[...] = jnp.zeros_like(base_ref)

    x = _slab_load(x_ref, tm) if slab else x_ref[...]
    xm = (x + jnp.dot(ys_ref[...], wo1_ref[...], preferred_element_type=F32)
          + jnp.dot(yh_ref[...], wo2_ref[...], preferred_element_type=F32))
    h = _rms(xm, nw_ref[...])
    lt = _mm_nt(wr_ref[...], h)
    row = lambda r: lt[r:r + 1, :]
    gl = [row(j) for j in range(N_EXPERT_GROUPS)]
    g_max, g_idx = _first_argmax4(gl)
    g_w = 1.0 / (jnp.exp(gl[0] - g_max) + jnp.exp(gl[1] - g_max)
                 + jnp.exp(gl[2] - g_max) + jnp.exp(gl[3] - g_max))
    el = []
    for j in range(EXPERTS_PER_GROUP):
        base = N_EXPERT_GROUPS + j
        el.append(jnp.where(g_idx == 0.0, row(base),
                  jnp.where(g_idx == 1.0, row(base + EXPERTS_PER_GROUP),
                  jnp.where(g_idx == 2.0, row(base + 2 * EXPERTS_PER_GROUP),
                            row(base + 3 * EXPERTS_PER_GROUP)))))
    e1, i1 = _first_argmax4(el)
    el2 = [jnp.where(i1 == float(j), NEG_BIG, el[j]) for j in range(EXPERTS_PER_GROUP)]
    e2, i2 = _first_argmax4(el2)
    w1 = 1.0 / (1.0 + jnp.exp(e2 - e1))
    w2 = jnp.exp(e2 - e1) * w1
    lo = jnp.minimum(i1, i2)
    hi = jnp.maximum(i1, i2)
    w_lo = g_w * jnp.where(i1 < i2, w1, w2)
    w_hi = g_w * jnp.where(i1 < i2, w2, w1)
    pair = lo * (7.0 - lo) * 0.5 + (hi - lo - 1.0)
    bucket = g_idx * float(N_PAIRS) + pair
    sub_l = lax.broadcasted_iota(jnp.int32, (ROUTE_LANES, tm), 0)
    onehot = jnp.where(sub_l.astype(F32) == bucket, 1.0, 0.0)
    r_i = lax.broadcasted_iota(jnp.int32, (tm, tm), 0)
    c_i = lax.broadcasted_iota(jnp.int32, (tm, tm), 1)
    before = jnp.dot(onehot.astype(BF16), (r_i < c_i).astype(BF16), preferred_element_type=F32)
    base = base_ref[...]
    before = before + jnp.concatenate([base] * (tm // LANES), axis=1)
    rank = jnp.sum(onehot * before, axis=0, keepdims=True)
    base = base + jnp.sum(onehot, axis=1, keepdims=True)
    base_ref[...] = base
    cnt_ref[...] = base
    sub = lax.broadcasted_iota(jnp.int32, (8, tm), 0)
    rt_ref[0] = jnp.where(sub == 0, bucket * float(RANK_SPAN) + rank, 0.0).astype(jnp.int32)
    wide = jnp.where(sub_l == 0, bucket, jnp.where(sub_l == 1, w_lo, jnp.where(sub_l == 2, w_hi, 0.0)))
    for c in range(D_MODEL // LANES):
        xe_ref[pl.ds(c, tm, stride=SLAB), :] = xm[:, c * LANES:(c + 1) * LANES]
    xe_ref[pl.ds(D_MODEL // LANES, tm, stride=SLAB), :] = wide.T


def _outproj(ys, yh, x, n_tok, slab, wo1, wo2, nw, wr_t):
    tm = TM_PROJ
    row = lambda i: (i, 0)
    fixed = lambda i: (0, 0)
    return pl.pallas_call(
        functools.partial(_outproj_kernel, slab=slab),
        grid=(n_tok // tm,),
        in_specs=[pl.BlockSpec((tm, SSD_WIDTH), row), pl.BlockSpec((tm, HGRN_WIDTH), row),
                  _x_spec(tm, slab, row),
                  pl.BlockSpec((SSD_WIDTH, D_MODEL), fixed), pl.BlockSpec((HGRN_WIDTH, D_MODEL), fixed),
                  pl.BlockSpec((1, D_MODEL), fixed), pl.BlockSpec((LANES, D_MODEL), fixed)],
        out_specs=[pl.BlockSpec((tm * SLAB, LANES), row), pl.BlockSpec((1, 8, tm), lambda i: (i, 0, 0)),
                   pl.BlockSpec((ROUTE_LANES, LANES), fixed)],
        out_shape=[jax.ShapeDtypeStruct((n_tok * SLAB, LANES), F32),
                   jax.ShapeDtypeStruct((n_tok // tm, 8, tm), jnp.int32),
                   jax.ShapeDtypeStruct((ROUTE_LANES, LANES), F32)],
        scratch_shapes=[pltpu.VMEM((ROUTE_LANES, LANES), F32)],
        compiler_params=pltpu.CompilerParams(dimension_semantics=("arbitrary",),
                                             vmem_limit_bytes=VMEM_LIMIT),
        name="outproj_router",
    )(ys, yh, x, wo1, wo2, nw, wr_t)


def _ffn_kernel(ea_ref, eb_ref, nvalid_ref, nused_ref,
                src0_ref, src1_ref, src2_ref, xe_hbm, nw_ref, fw_ref,
                wga_ref, wgb_ref, wua_ref, wub_ref, wda_ref, wdb_ref,
                out_hbm, xbuf0, xbuf1, xbuf2, obuf, gsem, ssem, *, final):
    del ea_ref, eb_ref
    tm = TM_FFN
    j = pl.program_id(0)
    nt = pl.num_programs(0)
    n_used = nused_ref[0]
    xbufs = (xbuf0, xbuf1, xbuf2)
    nx = len(xbufs)
    orow = 1 if final else SLAB

    def issue_gather(src_ref, q):
        for r in range(tm):
            pltpu.make_async_copy(xe_hbm.at[pl.ds(src_ref[0, 0, r] * SLAB, SLAB)],
                                  xbufs[q].at[pl.ds(r * SLAB, SLAB)], gsem.at[q]).start(priority=r % 2)

    def wait_gather(q):
        pltpu.make_async_copy(xe_hbm.at[pl.ds(0, tm * SLAB)], xbufs[q], gsem.at[q]).wait()

    def scatter_copy(r, p):
        return pltpu.make_async_copy(obuf.at[p, pl.ds(r * orow, orow)],
                                     out_hbm.at[pl.ds(src0_ref[0, 0, r] * orow, orow)], ssem.at[p])

    def wait_scatter(p, tile):
        n = nvalid_ref[tile]
        n8 = pl.multiple_of((n // 8) * 8, 8)

        @pl.when(n8 > 0)
        def _():
            rows = pl.multiple_of(n8 * orow, 8)
            pltpu.make_async_copy(obuf.at[p, pl.ds(0, rows)], out_hbm.at[pl.ds(0, rows)],
                                  ssem.at[p]).wait()

        def body(r, carry):
            pltpu.make_async_copy(obuf.at[p, pl.ds(0, orow)], out_hbm.at[pl.ds(0, orow)],
                                  ssem.at[p]).wait()
            return carry
        lax.fori_loop(0, n - n8, body, 0)

    @pl.when((j == 0) & (n_used > 0))
    def _():
        issue_gather(src0_ref, 0)
        issue_gather(src1_ref, 1)

    def step(q):
        p = j % 2

        @pl.when(j >= 2)
        def _():
            wait_scatter(p, j - 2)

        wait_gather(q)
        issue_gather(src2_ref, (q + 2) % nx)
        xm = _slab_load(xbufs[q], tm)
        route = xbufs[q][pl.ds(D_MODEL // LANES, tm, stride=SLAB), :]
        w_lo = route[:, 1:2]
        w_hi = route[:, 2:3]
        h = _rms(xm, nw_ref[...]).astype(BF16)
        wt = lambda w_ref: w_ref[0, 0].astype(BF16)
        hid_a = _silu(jnp.dot(h, wt(wga_ref), preferred_element_type=F32)) \
            * jnp.dot(h, wt(wua_ref), preferred_element_type=F32)
        hid_b = _silu(jnp.dot(h, wt(wgb_ref), preferred_element_type=F32)) \
            * jnp.dot(h, wt(wub_ref), preferred_element_type=F32)
        y = (jnp.dot((hid_a * w_lo).astype(BF16), wt(wda_ref), preferred_element_type=F32)
             + jnp.dot((hid_b * w_hi).astype(BF16), wt(wdb_ref), preferred_element_type=F32))
        out = xm + y
        if final:
            obuf[p] = _rms(out, fw_ref[...])
        else:
            for c in range(D_MODEL // LANES):
                obuf[p, pl.ds(c, tm, stride=SLAB), :] = out[:, c * LANES:(c + 1) * LANES]
            obuf[p, pl.ds(D_MODEL // LANES, tm, stride=SLAB), :] = route

        n_valid = nvalid_ref[j]

        @pl.when(n_valid == tm)
        def _():
            for r in range(tm):
                scatter_copy(r, p).start(priority=r % 2)

        @pl.when(n_valid < tm)
        def _():
            def single(r, carry):
                scatter_copy(r, p).start()
                return carry
            lax.fori_loop(0, n_valid, single, 0)

    for q in range(nx):
        @pl.when((j % nx == q) & (j < n_used))
        def _():
            step(q)

    @pl.when((j == nt - 1) & (n_used >= 1))
    def _():
        for q in range(nx):
            @pl.when((n_used % nx == q) | ((n_used + 1) % nx == q))
            def _():
                wait_gather(q)

        wait_scatter((n_used - 1) % 2, n_used - 1)

        @pl.when(n_used >= 2)
        def _():
            wait_scatter(n_used % 2, n_used - 2)


def _ffn(xe, n_tok, tile_ea, tile_eb, tile_nvalid, n_used, src, nw, fw, layer, wg, wu, wd, final):
    tm = TM_FFN
    nt = src.shape[0]
    obuf_shape = (tm, D_MODEL) if final else (tm * SLAB, LANES)
    out_rows = (n_tok, D_MODEL) if final else (n_tok * SLAB, LANES)
    clamp = lambda j, nu: jnp.minimum(j, jnp.maximum(nu[0] - 1, 0))
    wa = lambda j, ea, eb, nv, nu: (layer, ea[clamp(j, nu)], 0, 0)
    wb = lambda j, ea, eb, nv, nu: (layer, eb[clamp(j, nu)], 0, 0)
    fixed = lambda j, ea, eb, nv, nu: (0, 0)
    smem_blk = lambda f: pl.BlockSpec((1, 1, tm), f, memory_space=pltpu.SMEM)
    grid_spec = pltpu.PrefetchScalarGridSpec(
        num_scalar_prefetch=4,
        grid=(nt,),
        in_specs=[smem_blk(lambda j, ea, eb, nv, nu: (j, 0, 0)),
                  smem_blk(lambda j, ea, eb, nv, nu: (jnp.minimum(j + 1, nt - 1), 0, 0)),
                  smem_blk(lambda j, ea, eb, nv, nu: (jnp.minimum(j + 2, nt - 1), 0, 0)),
                  pl.BlockSpec(memory_space=pl.ANY),
                  pl.BlockSpec((1, D_MODEL), fixed), pl.BlockSpec((1, D_MODEL), fixed),
                  pl.BlockSpec((1, 1, D_MODEL, EXPERT_DIM), wa), pl.BlockSpec((1, 1, D_MODEL, EXPERT_DIM), wb),
                  pl.BlockSpec((1, 1, D_MODEL, EXPERT_DIM), wa), pl.BlockSpec((1, 1, D_MODEL, EXPERT_DIM), wb),
                  pl.BlockSpec((1, 1, EXPERT_DIM, D_MODEL), wa), pl.BlockSpec((1, 1, EXPERT_DIM, D_MODEL), wb)],
        out_specs=pl.BlockSpec(memory_space=pl.ANY),
        scratch_shapes=[pltpu.VMEM((tm * SLAB, LANES), F32)] * 3
        + [pltpu.VMEM((2,) + obuf_shape, F32),
           pltpu.SemaphoreType.DMA((3,)), pltpu.SemaphoreType.DMA((2,))],
    )
    return pl.pallas_call(
        functools.partial(_ffn_kernel, final=final),
        grid_spec=grid_spec,
        out_shape=jax.ShapeDtypeStruct(out_rows, F32),
        compiler_params=pltpu.CompilerParams(dimension_semantics=("arbitrary",),
                                             vmem_limit_bytes=VMEM_LIMIT),
        name="moe_ffn",
    )(tile_ea, tile_eb, tile_nvalid, n_used, src, src, src, xe, nw, fw, wg, wg, wu, wu, wd, wd)


def _pair_tables():
    lo, hi = [], []
    for a in range(EXPERTS_PER_GROUP):
        for b in range(a + 1, EXPERTS_PER_GROUP):
            lo.append(a)
            hi.append(b)
    return lo, hi


def _plan_tables_kernel(counts_ref, key_ref, pos_ref, tab_ref):
    tm = TM_FFN
    nt = tab_ref.shape[1]
    lo, hi = _pair_tables()
    key = key_ref[...]
    bucket = key >> RANK_BITS
    pos = key & (RANK_SPAN - 1)
    tile0 = 0
    for b in range(N_BUCKETS):
        cnt = counts_ref[b]
        n_tiles = (cnt + (tm - 1)) // tm
        pos = pos + jnp.where(bucket == b, tile0 * tm, 0)
        tab_ref[4, b] = tile0 * tm + cnt
        tab_ref[5, b] = (tile0 + n_tiles) * tm if b + 1 < N_BUCKETS else nt * tm
        ea = (b // N_PAIRS) * EXPERTS_PER_GROUP + lo[b % N_PAIRS]
        eb = (b // N_PAIRS) * EXPERTS_PER_GROUP + hi[b % N_PAIRS]

        def tile(k, carry, tile0=tile0, cnt=cnt, ea=ea, eb=eb):
            tab_ref[0, tile0 + k] = ea
            tab_ref[1, tile0 + k] = eb
            tab_ref[2, tile0 + k] = jnp.minimum(cnt - k * tm, tm)
            return carry
        lax.fori_loop(0, n_tiles, tile, 0)
        tile0 = tile0 + n_tiles
    n_used = tile0
    pos_ref[...] = pos

    def spare(k, carry):
        tab_ref[0, k] = 0
        tab_ref[1, k] = 0
        tab_ref[2, k] = 0
        return carry
    lax.fori_loop(n_used, nt, spare, 0)

    def rest(k, carry):
        tab_ref[3, k] = n_used
        return carry
    lax.fori_loop(0, nt, rest, 0)

    def unused(k, carry):
        tab_ref[4, k] = 0
        tab_ref[5, k] = 0
        return carry
    lax.fori_loop(N_BUCKETS, nt, unused, 0)


def _plan_invert_kernel(pad_lo_ref, pad_hi_ref, pos_ref, src_ref):
    i = pl.program_id(0)
    ch = pos_ref.shape[0]

    @pl.when(i == 0)
    def _():
        for b in range(N_BUCKETS):
            def zero(k, carry):
                src_ref[k] = 0
                return carry
            lax.fori_loop(pad_lo_ref[b], pad_hi_ref[b], zero, 0)

    def body(k, carry):
        for u in range(PLAN_UNROLL):
            t = k * PLAN_UNROLL + u
            src_ref[pos_ref[t]] = i * ch + t
        return carry
    lax.fori_loop(0, ch // PLAN_UNROLL, body, 0)


def _dispatch_plan(route_i, counts, n_tok):
    tm = TM_FFN
    nt = n_tok // tm + N_BUCKETS
    assert n_tok <= RANK_SPAN and n_tok % LANES == 0
    key = route_i[:, 0, :].reshape(n_tok // LANES, LANES)
    pos, tab = pl.pallas_call(
        _plan_tables_kernel,
        grid_spec=pltpu.PrefetchScalarGridSpec(
            num_scalar_prefetch=1,
            grid=(1,),
            in_specs=[pl.BlockSpec(key.shape, lambda i, c: (0, 0))],
            out_specs=[pl.BlockSpec(key.shape, lambda i, c: (0, 0)),
                       pl.BlockSpec(memory_space=pltpu.SMEM)],
        ),
        out_shape=[jax.ShapeDtypeStruct(key.shape, jnp.int32),
                   jax.ShapeDtypeStruct((6, nt), jnp.int32)],
        compiler_params=pltpu.CompilerParams(dimension_semantics=("arbitrary",)),
        name="dispatch_tables",
    )(counts, key)
    ch = min(PLAN_CHUNK, n_tok)
    src = pl.pallas_call(
        _plan_invert_kernel,
        grid_spec=pltpu.PrefetchScalarGridSpec(
            num_scalar_prefetch=2,
            grid=(n_tok // ch,),
            in_specs=[pl.BlockSpec((ch,), lambda i, *_: (i,), memory_space=pltpu.SMEM)],
            out_specs=pl.BlockSpec(memory_space=pltpu.SMEM),
        ),
        out_shape=jax.ShapeDtypeStruct((nt * tm,), jnp.int32),
        compiler_params=pltpu.CompilerParams(dimension_semantics=("arbitrary",)),
        name="dispatch_invert",
    )(tab[4, :N_BUCKETS], tab[5, :N_BUCKETS], pos.reshape(n_tok))
    return tab[0], tab[1], tab[2], tab[3, :1], src.reshape(nt, 1, tm)


def kernel(x, norm_mix_w, w_in, conv_w, conv_b, dt_bias, a_log, d_skip, ssd_norm_w, hgrn_lb_logits,
           hgrn_norm_w, w_out, norm_ffn_w, router_group, router_expert, w_gate, w_up, w_down,
           final_norm_w):
    bsz, seq, _ = x.shape
    depth = w_in.shape[0]
    n_tok = bsz * seq
    assert n_tok % TM_PROJ == 0 and seq % TS_MIX == 0 and n_tok % TM_FFN == 0

    p = jax.nn.softmax(hgrn_lb_logits.astype(F32), axis=0)
    lower_bounds = jnp.cumsum(p, axis=0) - p[0:1]

    o_xbc = SSD_WIDTH
    o_dt = o_xbc + XBC_WIDTH
    o_q = o_dt + SSD_HEADS
    pad_h = lambda v: jnp.pad(v.astype(F32), (0, LANES - SSD_HEADS)).reshape(1, LANES)
    row = lambda v: v.astype(F32).reshape(1, -1)

    xcur = x.reshape(n_tok, D_MODEL)
    for l in range(depth):
        wl = w_in[l].astype(BF16)
        wq, wf, wi, wg = (wl[:, o_q + k * HGRN_WIDTH:o_q + (k + 1) * HGRN_WIDTH] for k in range(4))
        wdt = jnp.pad(wl[:, o_dt:o_q], ((0, 0), (0, LANES - SSD_HEADS)))
        z, xbc, q, f, i, g, dt = _inproj(xcur, n_tok, l > 0, row(norm_mix_w[l]), wl[:, :o_xbc],
                                         wl[:, o_xbc:o_dt], wq, wf, wi, wg, wdt)
        y_ssd = _ssd(z, xbc, dt, bsz, seq, conv_w[l].astype(F32), row(conv_b[l]), dt_bias[l],
                     a_log[l], row(jnp.repeat(d_skip[l], SSD_HEAD_DIM)), row(ssd_norm_w[l]))
        y_hgrn = _hgrn(q, f, i, g, bsz, seq, row(lower_bounds[l]), row(hgrn_norm_w[l]))
        wo = w_out[l].astype(BF16)
        wr_t = jnp.concatenate([router_group[l], router_expert[l]], axis=1).T
        wr_t = jnp.pad(wr_t, ((0, LANES - wr_t.shape[0]), (0, 0))).astype(BF16)
        xe, route, cnt = _outproj(y_ssd, y_hgrn, xcur, n_tok, l > 0, wo[:SSD_WIDTH], wo[SSD_WIDTH:],
                                  row(norm_ffn_w[l]), wr_t)
        counts = cnt[:N_BUCKETS, 0].astype(jnp.int32)
        tile_ea, tile_eb, tile_nvalid, n_used, src = _dispatch_plan(route, counts, n_tok)
        xcur = _ffn(xe, n_tok, tile_ea, tile_eb, tile_nvalid, n_used, src, row(norm_ffn_w[l]),
                    row(final_norm_w), l, w_gate, w_up, w_down, final=(l == depth - 1))
    return xcur.reshape(bsz, seq, D_MODEL)
```

```python
import functools

import jax
import jax.numpy as jnp
from jax import lax
from jax.experimental import pallas as pl
from jax.experimental.pallas import tpu as pltpu

F32 = jnp.float32
BF16 = jnp.bfloat16

D_MODEL = 1024
SSD_HEADS = 8
SSD_HEAD_DIM = 64
SSD_WIDTH = SSD_HEADS * SSD_HEAD_DIM
SSD_GROUPS = 2
SSD_STATE = 128
CONV_WIDTH = 4
XBC_WIDTH = SSD_WIDTH + 2 * SSD_GROUPS * SSD_STATE
HGRN_HEADS = 4
HGRN_HEAD_DIM = 128
HGRN_WIDTH = HGRN_HEADS * HGRN_HEAD_DIM
HGRN_BLOCK = 32
N_EXPERT_GROUPS = 4
EXPERTS_PER_GROUP = 4
N_EXPERTS = N_EXPERT_GROUPS * EXPERTS_PER_GROUP
EXPERT_DIM = 256
EPS = 1e-6

LANES = 128
N_PAIRS = 6
N_BUCKETS = N_EXPERT_GROUPS * N_PAIRS
ROUTE_LANES = LANES
SLAB = D_MODEL // LANES + 1
NEG_BIG = -1e30
LOG2E = 1.4426950408889634

TM_PROJ = 512
TS_MIX = 256
MIX_BATCH = 4
SSD_SUB = 128
HGRN_SUB = 128
CONV_HALO = 8
TM_FFN = 256
PLAN_UNROLL = 32
RANK_BITS = 16
RANK_SPAN = 1 << RANK_BITS
PLAN_CHUNK = 2048
ROW_COPY_CHUNK = 512
VMEM_LIMIT = 56 * 1024 * 1024


def _mm(a, b):
    return jnp.dot(a.astype(BF16), b.astype(BF16), preferred_element_type=F32)


def _mm_nt(a, b):
    return lax.dot_general(a.astype(BF16), b.astype(BF16), (((1,), (1,)), ((), ())),
                           preferred_element_type=F32)


def _mm_tn(a, b):
    return lax.dot_general(a.astype(BF16), b.astype(BF16), (((0,), (0,)), ((), ())),
                           preferred_element_type=F32)


def _split3(x):
    p1 = x.astype(BF16)
    r1 = x - p1.astype(F32)
    p2 = r1.astype(BF16)
    p3 = (r1 - p2.astype(F32)).astype(BF16)
    return p1, p2, p3


def _cumsum_mm(tri, x):
    p1, p2, p3 = _split3(x)
    acc = jnp.dot(tri, p1, preferred_element_type=F32)
    acc = acc + jnp.dot(tri, p2, preferred_element_type=F32)
    return acc + jnp.dot(tri, p3, preferred_element_type=F32)


def _sigmoid(x):
    return 0.5 * jnp.tanh(0.5 * x) + 0.5


def _silu(x):
    hx = 0.5 * x
    return hx + hx * jnp.tanh(hx)


def _rms(x, w):
    return x * lax.rsqrt(jnp.mean(x * x, axis=-1, keepdims=True) + EPS) * w


def _slab_load(ref, rows):
    return jnp.concatenate([ref[pl.ds(c, rows, stride=SLAB), :] for c in range(D_MODEL // LANES)],
                           axis=1)


def _x_spec(rows, slab, index):
    if slab:
        return pl.BlockSpec((rows * SLAB, LANES), index)
    return pl.BlockSpec((rows, D_MODEL), index)


def _inproj_kernel(x_ref, nw_ref, wz_ref, wxbc_ref, wq_ref, wf_ref, wi_ref, wg_ref, wdt_ref,
                   z_ref, xbc_ref, q_ref, f_ref, i_ref, g_ref, dt_ref, *, slab):
    x = _slab_load(x_ref, z_ref.shape[0]) if slab else x_ref[...]
    h = _rms(x, nw_ref[...]).astype(BF16)
    for w_ref, o_ref in ((wz_ref, z_ref), (wxbc_ref, xbc_ref), (wq_ref, q_ref),
                         (wf_ref, f_ref), (wi_ref, i_ref), (wg_ref, g_ref), (wdt_ref, dt_ref)):
        o_ref[...] = jnp.dot(h, w_ref[...], preferred_element_type=F32).astype(o_ref.dtype)


def _inproj(x, n_tok, slab, nw, wz, wxbc, wq, wf, wi, wg, wdt):
    tm = TM_PROJ
    row = lambda i: (i, 0)
    fixed = lambda i: (0, 0)
    widths = (SSD_WIDTH, XBC_WIDTH, HGRN_WIDTH, HGRN_WIDTH, HGRN_WIDTH, HGRN_WIDTH, LANES)
    dtypes = (BF16,) * 6 + (F32,)
    return pl.pallas_call(
        functools.partial(_inproj_kernel, slab=slab),
        grid=(n_tok // tm,),
        in_specs=[_x_spec(tm, slab, row), pl.BlockSpec((1, D_MODEL), fixed)]
        + [pl.BlockSpec((D_MODEL, w), fixed) for w in widths],
        out_specs=[pl.BlockSpec((tm, w), row) for w in widths],
        out_shape=[jax.ShapeDtypeStruct((n_tok, w), dt) for w, dt in zip(widths, dtypes)],
        compiler_params=pltpu.CompilerParams(dimension_semantics=("arbitrary",),
                                             vmem_limit_bytes=VMEM_LIMIT),
        name="inproj",
    )(x, nw, wz, wxbc, wq, wf, wi, wg, wdt)


def _ssd_kernel(z_ref, xbc_ref, dt_ref, cw_ref, cb_ref, dtb_ref, alog_ref, dsk_ref, nw_ref, ex_ref,
                y_ref, ext_ref, h_ref):
    nbat, ts, _ = z_ref.shape
    L = SSD_SUB
    hpg = SSD_HEADS // SSD_GROUPS
    gw = SSD_WIDTH // SSD_GROUPS
    nh = SSD_HEADS
    pre = CONV_HALO
    streams = [(bb, c) for bb in range(nbat) for c in range(ts // L)]

    nlc = XBC_WIDTH // LANES

    @pl.when(pl.program_id(1) == 0)
    def _():
        ext_ref[...] = jnp.zeros_like(ext_ref)
        h_ref[...] = jnp.zeros_like(h_ref)

    for bb in range(nbat):
        for j in range(nlc):
            ext_ref[bb, j, pl.ds(pre, ts, stride=2), :] = xbc_ref[bb, :, j * LANES:(j + 1) * LANES].astype(F32)

    r_i = lax.broadcasted_iota(jnp.int32, (L, L), 0)
    c_i = lax.broadcasted_iota(jnp.int32, (L, L), 1)
    causal = r_i >= c_i
    triu = (r_i <= c_i).astype(BF16)
    even = lax.broadcasted_iota(jnp.int32, (L, LANES), 1) < SSD_HEAD_DIM

    lane_tile = lambda v: jnp.concatenate([v] * (ts // LANES), axis=1)
    dt_all, da_all, ldt_all = {}, {}, {}
    for bb in range(nbat):
        dt_raw = dt_ref[bb].T[0:nh, :] + lane_tile(dtb_ref[...])
        dt_all[bb] = jnp.maximum(dt_raw, 0.0) + jnp.log(1.0 + jnp.exp(-jnp.abs(dt_raw)))
        da_all[bb] = dt_all[bb] * lane_tile(-jnp.exp(alog_ref[...]) * LOG2E)
        ldt_all[bb] = jnp.log2(dt_all[bb])

    conv = {}
    for bb, c in streams:
        r0 = c * L
        pieces = []
        for j in range(nlc):
            ls = slice(j * LANES, (j + 1) * LANES)
            acc = cb_ref[:, ls]
            for k in range(CONV_WIDTH):
                tap = ext_ref[bb, j, pl.ds(pre + 2 * (r0 - (CONV_WIDTH - 1 - k)), L, stride=2), :]
                acc = acc + cw_ref[k:k + 1, ls] * tap
            pieces.append(_silu(acc))
        conv[bb, c] = jnp.concatenate(pieces, axis=1)

    cols, key_row, e_cum, xsc = {}, {}, {}, {}
    for bb, c in streams:
        r0 = c * L
        dt = dt_all[bb][:, r0:r0 + L]
        p1, p2, p3 = _split3(da_all[bb][:, r0:r0 + L])
        cum = (jnp.dot(p1, triu, preferred_element_type=F32) + jnp.dot(p2, triu, preferred_element_type=F32)
               + jnp.dot(p3, triu, preferred_element_type=F32))
        cum_last = cum[:, L - 1:L]
        key_row[bb, c] = cum - ldt_all[bb][:, r0:r0 + L]
        rows = jnp.concatenate([cum, jnp.exp2(cum), jnp.exp2(cum_last - cum) * dt,
                                jnp.zeros((LANES - 3 * nh, L), F32)], axis=0)
        cols[bb, c] = rows.T
        hi = cols[bb, c].astype(BF16)
        lo = (cols[bb, c] - hi.astype(F32)).astype(BF16)
        spread = jnp.dot(jnp.concatenate([hi, lo], axis=1), ex_ref[...],
                         preferred_element_type=F32)
        e_cum[bb, c] = spread[:, :SSD_WIDTH]
        xsc[bb, c] = conv[bb, c][:, :SSD_WIDTH] * spread[:, SSD_WIDTH:]

    y_intra = {}
    for bb, c in streams:
        xs = conv[bb, c][:, :SSD_WIDTH]
        for g in range(SSD_GROUPS):
            lo_b = SSD_WIDTH + g * SSD_STATE
            lo_c = SSD_WIDTH + (SSD_GROUPS + g) * SSD_STATE
            cb = _mm_nt(conv[bb, c][:, lo_c:lo_c + SSD_STATE], conv[bb, c][:, lo_b:lo_b + SSD_STATE])
            y_parts = []
            for p in range(hpg // 2):
                h0 = g * hpg + 2 * p
                xp = xs[:, h0 * SSD_HEAD_DIM:(h0 + 2) * SSD_HEAD_DIM]
                scores = []
                for hh in (h0, h0 + 1):
                    seg = cols[bb, c][:, hh:hh + 1] - key_row[bb, c][hh:hh + 1, :]
                    scores.append((cb * jnp.exp2(jnp.where(causal, seg, NEG_BIG))).astype(BF16))
                rhs = jnp.concatenate([jnp.where(even, xp, 0.0), jnp.where(even, 0.0, xp)],
                                      axis=0).astype(BF16)
                y_parts.append(jnp.dot(jnp.concatenate(scores, axis=1), rhs,
                                       preferred_element_type=F32))
            y_intra[bb, c, g] = jnp.concatenate(y_parts, axis=1)

    for c in range(ts // L):
        r0 = c * L
        for bb in range(nbat):
            xs = conv[bb, c][:, :SSD_WIDTH]
            zc = z_ref[bb, r0:r0 + L, :].astype(F32)
            e_last = e_cum[bb, c][L - 1:L, :]
            for g in range(SSD_GROUPS):
                gs = slice(g * gw, (g + 1) * gw)
                lo_b = SSD_WIDTH + g * SSD_STATE
                lo_c = SSD_WIDTH + (SSD_GROUPS + g) * SSD_STATE
                h_t = h_ref[bb, g]
                yg = (y_intra[bb, c, g] + _mm(conv[bb, c][:, lo_c:lo_c + SSD_STATE], h_t) * e_cum[bb, c][:, gs]
                      + dsk_ref[:, gs] * xs[:, gs])
                h_ref[bb, g] = h_t * e_last[:, gs] + _mm_tn(conv[bb, c][:, lo_b:lo_b + SSD_STATE],
                                                            xsc[bb, c][:, gs])
                yg = _rms(yg * _silu(zc[:, gs]), nw_ref[:, gs])
                y_ref[bb, r0:r0 + L, gs] = yg.astype(y_ref.dtype)

    ext_ref[:, :, 0:pre, :] = ext_ref[:, :, 2 * ts:2 * ts + pre, :]


def _head_spread_matrix():
    col = jnp.arange(LANES)[:, None]
    out = jnp.arange(2 * SSD_WIDTH)[None, :]
    want = SSD_HEADS * (1 + out // SSD_WIDTH) + (out % SSD_WIDTH) // SSD_HEAD_DIM
    once = (col == want).astype(BF16)
    return jnp.concatenate([once, once], axis=0)


def _ssd(z, xbc, dt, bsz, seq, cw, cb, dtb, alog, dsk, nw):
    ts = TS_MIX
    nbat = MIX_BATCH
    tile = lambda b, s: (b, s, 0)
    fixed = lambda b, s: (0, 0)
    as3d = lambda v: v.reshape(bsz, seq, v.shape[-1])
    per_head_rows = lambda v: jnp.broadcast_to(v.astype(F32)[:, None], (SSD_HEADS, LANES))
    out = pl.pallas_call(
        _ssd_kernel,
        grid=(bsz // nbat, seq // ts),
        in_specs=[pl.BlockSpec((nbat, ts, SSD_WIDTH), tile), pl.BlockSpec((nbat, ts, XBC_WIDTH), tile),
                  pl.BlockSpec((nbat, ts, LANES), tile),
                  pl.BlockSpec((CONV_WIDTH, XBC_WIDTH), fixed), pl.BlockSpec((1, XBC_WIDTH), fixed),
                  pl.BlockSpec((SSD_HEADS, LANES), fixed), pl.BlockSpec((SSD_HEADS, LANES), fixed),
                  pl.BlockSpec((1, SSD_WIDTH), fixed), pl.BlockSpec((1, SSD_WIDTH), fixed),
                  pl.BlockSpec((2 * LANES, 2 * SSD_WIDTH), fixed)],
        out_specs=pl.BlockSpec((nbat, ts, SSD_WIDTH), tile),
        out_shape=jax.ShapeDtypeStruct((bsz, seq, SSD_WIDTH), BF16),
        scratch_shapes=[pltpu.VMEM((nbat, XBC_WIDTH // LANES, 2 * ts + CONV_HALO, LANES), F32),
                        pltpu.VMEM((nbat, SSD_GROUPS, SSD_STATE, SSD_WIDTH // SSD_GROUPS), F32)],
        compiler_params=pltpu.CompilerParams(dimension_semantics=("arbitrary", "arbitrary"),
                                             vmem_limit_bytes=VMEM_LIMIT),
        name="ssd",
    )(as3d(z), as3d(xbc), as3d(dt), cw, cb, per_head_rows(dtb), per_head_rows(alog), dsk, nw,
      _head_spread_matrix())
    return out.reshape(bsz * seq, SSD_WIDTH)


def _hgrn_body(q_ref, f_ref, i_ref, g_ref, lb_ref, nw_ref, o_ref, st_ref):
    ts = q_ref.shape[0]
    blk = HGRN_BLOCK
    nb = ts // blk
    hd = HGRN_HEAD_DIM

    @pl.when(pl.program_id(1) == 0)
    def _():
        st_ref[...] = jnp.zeros_like(st_ref)

    sub = HGRN_SUB
    r_i = lax.broadcasted_iota(jnp.int32, (sub, sub), 0)
    c_i = lax.broadcasted_iota(jnp.int32, (sub, sub), 1)
    blk_causal = (r_i >= c_i) & ((r_i // blk) == (c_i // blk))
    row_in_blk = lax.broadcasted_iota(jnp.int32, (ts, hd), 0) % blk

    heads = []
    for h in range(HGRN_HEADS):
        sl = slice(h * hd, (h + 1) * hd)
        lb = lb_ref[:, sl]
        forget = lb + (1.0 - lb) * _sigmoid(f_ref[:, sl].astype(F32))
        kk = 1.0 - forget
        cum = jnp.log(forget)
        shift = 1
        while shift < blk:
            cum = cum + jnp.where(row_in_blk >= shift, pltpu.roll(cum, shift, axis=0), 0.0)
            shift *= 2
        cum = cum * LOG2E
        cum3 = cum.reshape(nb, blk, hd)
        b_mid = cum3[:, blk // 2:blk // 2 + 1, :]
        b_end = cum3[:, blk - 1:blk, :]
        rel = (cum3 - b_mid).reshape(ts, hd)
        to_end = (b_end - cum3).reshape(ts, hd)
        q = q_ref[:, sl].astype(F32)
        v = i_ref[:, sl].astype(BF16)
        heads.append(dict(sl=sl, v=v, o_intra=[], o_inter=[],
                          qs=(q * jnp.exp2(rel)).astype(BF16),
                          ks=(kk * jnp.exp2(-rel)).astype(BF16),
                          q_dec=(q * jnp.exp2(cum)).astype(BF16),
                          k_end=(kk * jnp.exp2(to_end)).astype(BF16),
                          dec=jnp.exp2(b_end), st=st_ref[h]))

    def pair_diag(x0, x1):
        z = jnp.zeros_like(x0)
        return jnp.concatenate([jnp.concatenate([x0, z], axis=1),
                                jnp.concatenate([z, x1], axis=1)], axis=0)

    pairs = [(heads[i], heads[i + 1]) for i in range(0, HGRN_HEADS, 2)]
    mask2 = jnp.concatenate([blk_causal, blk_causal], axis=1)

    for a, b in pairs:
        for c in range(ts // sub):
            cs = slice(c * sub, (c + 1) * sub)
            sc = _mm_nt(jnp.concatenate([a["qs"][cs], b["qs"][cs]], axis=1),
                        pair_diag(a["ks"][cs], b["ks"][cs]))
            sc = jnp.where(mask2, sc, 0.0).astype(BF16)
            o = jnp.dot(sc, pair_diag(a["v"][cs], b["v"][cs]), preferred_element_type=F32)
            a["o_intra"].append(o[:, :hd])
            b["o_intra"].append(o[:, hd:])

    for n in range(nb):
        rs = slice(n * blk, (n + 1) * blk)
        for a, b in pairs:
            o = _mm_nt(jnp.concatenate([a["q_dec"][rs], b["q_dec"][rs]], axis=1),
                       pair_diag(a["st"].astype(BF16), b["st"].astype(BF16)))
            a["o_inter"].append(o[:, :hd])
            b["o_inter"].append(o[:, hd:])
            for hh in (a, b):
                hh["st"] = hh["st"] * hh["dec"][n] + _mm_tn(hh["v"][rs], hh["k_end"][rs])

    for h, hh in enumerate(heads):
        sl = hh["sl"]
        st_ref[h] = hh["st"]
        o = jnp.concatenate(hh["o_intra"], axis=0) + jnp.concatenate(hh["o_inter"], axis=0)
        o = _rms(o, nw_ref[:, sl]) * _silu(g_ref[:, sl].astype(F32))
        o_ref[:, sl] = o.astype(o_ref.dtype)


def _hgrn_kernel(q_ref, f_ref, i_ref, g_ref, lb_ref, nw_ref, o_ref, st_ref):
    for bb in range(q_ref.shape[0]):
        _hgrn_body(q_ref.at[bb], f_ref.at[bb], i_ref.at[bb], g_ref.at[bb], lb_ref, nw_ref,
                   o_ref.at[bb], st_ref.at[bb])


def _hgrn(q, f, i, g, bsz, seq, lb, nw):
    ts = TS_MIX
    nbat = MIX_BATCH
    tile = lambda b, s: (b, s, 0)
    fixed = lambda b, s: (0, 0)
    as3d = lambda v: v.reshape(bsz, seq, v.shape[-1])
    out = pl.pallas_call(
        _hgrn_kernel,
        grid=(bsz // nbat, seq // ts),
        in_specs=[pl.BlockSpec((nbat, ts, HGRN_WIDTH), tile)] * 4
        + [pl.BlockSpec((1, HGRN_WIDTH), fixed)] * 2,
        out_specs=pl.BlockSpec((nbat, ts, HGRN_WIDTH), tile),
        out_shape=jax.ShapeDtypeStruct((bsz, seq, HGRN_WIDTH), BF16),
        scratch_shapes=[pltpu.VMEM((nbat, HGRN_HEADS, HGRN_HEAD_DIM, HGRN_HEAD_DIM), F32)],
        compiler_params=pltpu.CompilerParams(dimension_semantics=("arbitrary", "arbitrary"),
                                             vmem_limit_bytes=VMEM_LIMIT),
        name="hgrn",
    )(as3d(q), as3d(f), as3d(i), as3d(g), lb, nw)
    return out.reshape(bsz * seq, HGRN_WIDTH)


def _first_argmax4(v):
    m = jnp.maximum(jnp.maximum(v[0], v[1]), jnp.maximum(v[2], v[3]))
    idx = jnp.where(v[0] == m, 0.0, jnp.where(v[1] == m, 1.0, jnp.where(v[2] == m, 2.0, 3.0)))
    return m, idx


def _outproj_kernel(ys_ref, yh_ref, x_ref, wo1_ref, wo2_ref, nw_ref, wr_ref, xe_ref, rt_ref, cnt_ref,
                    base_ref, *, slab):
    tm = ys_ref.shape[0]

    @pl.when(pl.program_id(0) == 0)
    def _():
        base_ref[...] = jnp.zeros_like(base_ref)

    x = _slab_load(x_ref, tm) if slab else x_ref[...]
    xm = (x + jnp.dot(ys_ref[...], wo1_ref[...], preferred_element_type=F32)
          + jnp.dot(yh_ref[...], wo2_ref[...], preferred_element_type=F32))
    h = _rms(xm, nw_ref[...])
    lt = _mm_nt(wr_ref[...], h)
    row = lambda r: lt[r:r + 1, :]
    gl = [row(j) for j in range(N_EXPERT_GROUPS)]
    g_max, g_idx = _first_argmax4(gl)
    g_w = 1.0 / (jnp.exp(gl[0] - g_max) + jnp.exp(gl[1] - g_max)
                 + jnp.exp(gl[2] - g_max) + jnp.exp(gl[3] - g_max))
    el = []
    for j in range(EXPERTS_PER_GROUP):
        base = N_EXPERT_GROUPS + j
        el.append(jnp.where(g_idx == 0.0, row(base),
                  jnp.where(g_idx == 1.0, row(base + EXPERTS_PER_GROUP),
                  jnp.where(g_idx == 2.0, row(base + 2 * EXPERTS_PER_GROUP),
                            row(base + 3 * EXPERTS_PER_GROUP)))))
    e1, i1 = _first_argmax4(el)
    el2 = [jnp.where(i1 == float(j), NEG_BIG, el[j]) for j in range(EXPERTS_PER_GROUP)]
    e2, i2 = _first_argmax4(el2)
    w1 = 1.0 / (1.0 + jnp.exp(e2 - e1))
    w2 = jnp.exp(e2 - e1) * w1
    lo = jnp.minimum(i1, i2)
    hi = jnp.maximum(i1, i2)
    w_lo = g_w * jnp.where(i1 < i2, w1, w2)
    w_hi = g_w * jnp.where(i1 < i2, w2, w1)
    pair = lo * (7.0 - lo) * 0.5 + (hi - lo - 1.0)
    bucket = g_idx * float(N_PAIRS) + pair
    sub_l = lax.broadcasted_iota(jnp.int32, (ROUTE_LANES, tm), 0)
    onehot = jnp.where(sub_l.astype(F32) == bucket, 1.0, 0.0)
    r_i = lax.broadcasted_iota(jnp.int32, (tm, tm), 0)
    c_i = lax.broadcasted_iota(jnp.int32, (tm, tm), 1)
    before = jnp.dot(onehot.astype(BF16), (r_i < c_i).astype(BF16), preferred_element_type=F32)
    base = base_ref[...]
    before = before + jnp.concatenate([base] * (tm // LANES), axis=1)
    rank = jnp.sum(onehot * before, axis=0, keepdims=True)
    base = base + jnp.sum(onehot, axis=1, keepdims=True)
    base_ref[...] = base
    cnt_ref[...] = base
    sub = lax.broadcasted_iota(jnp.int32, (8, tm), 0)
    rt_ref[0] = jnp.where(sub == 0, bucket * float(RANK_SPAN) + rank, 0.0).astype(jnp.int32)
    wide = jnp.where(sub_l == 0, bucket, jnp.where(sub_l == 1, w_lo, jnp.where(sub_l == 2, w_hi, 0.0)))
    for c in range(D_MODEL // LANES):
        xe_ref[pl.ds(c, tm, stride=SLAB), :] = xm[:, c * LANES:(c + 1) * LANES]
    xe_ref[pl.ds(D_MODEL // LANES, tm, stride=SLAB), :] = wide.T


def _outproj(ys, yh, x, n_tok, slab, wo1, wo2, nw, wr_t):
    tm = TM_PROJ
    row = lambda i: (i, 0)
    fixed = lambda i: (0, 0)
    return pl.pallas_call(
        functools.partial(_outproj_kernel, slab=slab),
        grid=(n_tok // tm,),
        in_specs=[pl.BlockSpec((tm, SSD_WIDTH), row), pl.BlockSpec((tm, HGRN_WIDTH), row),
                  _x_spec(tm, slab, row),
                  pl.BlockSpec((SSD_WIDTH, D_MODEL), fixed), pl.BlockSpec((HGRN_WIDTH, D_MODEL), fixed),
                  pl.BlockSpec((1, D_MODEL), fixed), pl.BlockSpec((LANES, D_MODEL), fixed)],
        out_specs=[pl.BlockSpec((tm * SLAB, LANES), row), pl.BlockSpec((1, 8, tm), lambda i: (i, 0, 0)),
                   pl.BlockSpec((ROUTE_LANES, LANES), fixed)],
        out_shape=[jax.ShapeDtypeStruct((n_tok * SLAB, LANES), F32),
                   jax.ShapeDtypeStruct((n_tok // tm, 8, tm), jnp.int32),
                   jax.ShapeDtypeStruct((ROUTE_LANES, LANES), F32)],
        scratch_shapes=[pltpu.VMEM((ROUTE_LANES, LANES), F32)],
        compiler_params=pltpu.CompilerParams(dimension_semantics=("arbitrary",),
                                             vmem_limit_bytes=VMEM_LIMIT),
        name="outproj_router",
    )(ys, yh, x, wo1, wo2, nw, wr_t)


def _ffn_kernel(ea_ref, eb_ref, nvalid_ref, nused_ref,
                src0_ref, src1_ref, src2_ref, xe_hbm, nw_ref, fw_ref,
                wga_ref, wgb_ref, wua_ref, wub_ref, wda_ref, wdb_ref,
                out_hbm, xbuf0, xbuf1, xbuf2, obuf, gsem, ssem, *, final):
    del ea_ref, eb_ref
    tm = TM_FFN
    j = pl.program_id(0)
    nt = pl.num_programs(0)
    n_used = nused_ref[0]
    xbufs = (xbuf0, xbuf1, xbuf2)
    nx = len(xbufs)
    orow = 1 if final else SLAB

    def issue_gather(src_ref, q):
        for r in range(tm):
            pltpu.make_async_copy(xe_hbm.at[pl.ds(src_ref[0, 0, r] * SLAB, SLAB)],
                                  xbufs[q].at[pl.ds(r * SLAB, SLAB)], gsem.at[q]).start(priority=r % 2)

    def wait_gather(q):
        pltpu.make_async_copy(xe_hbm.at[pl.ds(0, tm * SLAB)], xbufs[q], gsem.at[q]).wait()

    def scatter_copy(r, p):
        return pltpu.make_async_copy(obuf.at[p, pl.ds(r * orow, orow)],
                                     out_hbm.at[pl.ds(src0_ref[0, 0, r] * orow, orow)], ssem.at[p])

    def wait_scatter(p, tile):
        n = nvalid_ref[tile]
        n8 = pl.multiple_of((n // 8) * 8, 8)

        @pl.when(n8 > 0)
        def _():
            rows = pl.multiple_of(n8 * orow, 8)
            pltpu.make_async_copy(obuf.at[p, pl.ds(0, rows)], out_hbm.at[pl.ds(0, rows)],
                                  ssem.at[p]).wait()

        def body(r, carry):
            pltpu.make_async_copy(obuf.at[p, pl.ds(0, orow)], out_hbm.at[pl.ds(0, orow)],
                                  ssem.at[p]).wait()
            return carry
        lax.fori_loop(0, n - n8, body, 0)

    @pl.when((j == 0) & (n_used > 0))
    def _():
        issue_gather(src0_ref, 0)
        issue_gather(src1_ref, 1)

    def step(q):
        p = j % 2

        @pl.when(j >= 2)
        def _():
            wait_scatter(p, j - 2)

        wait_gather(q)
        issue_gather(src2_ref, (q + 2) % nx)
        xm = _slab_load(xbufs[q], tm)
        route = xbufs[q][pl.ds(D_MODEL // LANES, tm, stride=SLAB), :]
        w_lo = route[:, 1:2]
        w_hi = route[:, 2:3]
        h = _rms(xm, nw_ref[...]).astype(BF16)
        wt = lambda w_ref: w_ref[0, 0].astype(BF16)
        hid_a = _silu(jnp.dot(h, wt(wga_ref), preferred_element_type=F32)) \
            * jnp.dot(h, wt(wua_ref), preferred_element_type=F32)
        hid_b = _silu(jnp.dot(h, wt(wgb_ref), preferred_element_type=F32)) \
            * jnp.dot(h, wt(wub_ref), preferred_element_type=F32)
        y = (jnp.dot((hid_a * w_lo).astype(BF16), wt(wda_ref), preferred_element_type=F32)
             + jnp.dot((hid_b * w_hi).astype(BF16), wt(wdb_ref), preferred_element_type=F32))
        out = xm + y
        if final:
            obuf[p] = _rms(out, fw_ref[...])
        else:
            for c in range(D_MODEL // LANES):
                obuf[p, pl.ds(c, tm, stride=SLAB), :] = out[:, c * LANES:(c + 1) * LANES]
            obuf[p, pl.ds(D_MODEL // LANES, tm, stride=SLAB), :] = route

        n_valid = nvalid_ref[j]

        @pl.when(n_valid == tm)
        def _():
            for r in range(tm):
                scatter_copy(r, p).start(priority=r % 2)

        @pl.when(n_valid < tm)
        def _():
            def single(r, carry):
                scatter_copy(r, p).start()
                return carry
            lax.fori_loop(0, n_valid, single, 0)

    for q in range(nx):
        @pl.when((j % nx == q) & (j < n_used))
        def _():
            step(q)

    @pl.when((j == nt - 1) & (n_used >= 1))
    def _():
        for q in range(nx):
            @pl.when((n_used % nx == q) | ((n_used + 1) % nx == q))
            def _():
                wait_gather(q)

        wait_scatter((n_used - 1) % 2, n_used - 1)

        @pl.when(n_used >= 2)
        def _():
            wait_scatter(n_used % 2, n_used - 2)


def _ffn(xe, n_tok, tile_ea, tile_eb, tile_nvalid, n_used, src, nw, fw, layer, wg, wu, wd, final):
    tm = TM_FFN
    nt = src.shape[0]
    obuf_shape = (tm, D_MODEL) if final else (tm * SLAB, LANES)
    out_rows = (n_tok, D_MODEL) if final else (n_tok * SLAB, LANES)
    clamp = lambda j, nu: jnp.minimum(j, jnp.maximum(nu[0] - 1, 0))
    wa = lambda j, ea, eb, nv, nu: (layer, ea[clamp(j, nu)], 0, 0)
    wb = lambda j, ea, eb, nv, nu: (layer, eb[clamp(j, nu)], 0, 0)
    fixed = lambda j, ea, eb, nv, nu: (0, 0)
    smem_blk = lambda f: pl.BlockSpec((1, 1, tm), f, memory_space=pltpu.SMEM)
    grid_spec = pltpu.PrefetchScalarGridSpec(
        num_scalar_prefetch=4,
        grid=(nt,),
        in_specs=[smem_blk(lambda j, ea, eb, nv, nu: (j, 0, 0)),
                  smem_blk(lambda j, ea, eb, nv, nu: (jnp.minimum(j + 1, nt - 1), 0, 0)),
                  smem_blk(lambda j, ea, eb, nv, nu: (jnp.minimum(j + 2, nt - 1), 0, 0)),
                  pl.BlockSpec(memory_space=pl.ANY),
                  pl.BlockSpec((1, D_MODEL), fixed), pl.BlockSpec((1, D_MODEL), fixed),
                  pl.BlockSpec((1, 1, D_MODEL, EXPERT_DIM), wa), pl.BlockSpec((1, 1, D_MODEL, EXPERT_DIM), wb),
                  pl.BlockSpec((1, 1, D_MODEL, EXPERT_DIM), wa), pl.BlockSpec((1, 1, D_MODEL, EXPERT_DIM), wb),
                  pl.BlockSpec((1, 1, EXPERT_DIM, D_MODEL), wa), pl.BlockSpec((1, 1, EXPERT_DIM, D_MODEL), wb)],
        out_specs=pl.BlockSpec(memory_space=pl.ANY),
        scratch_shapes=[pltpu.VMEM((tm * SLAB, LANES), F32)] * 3
        + [pltpu.VMEM((2,) + obuf_shape, F32),
           pltpu.SemaphoreType.DMA((3,)), pltpu.SemaphoreType.DMA((2,))],
    )
    return pl.pallas_call(
        functools.partial(_ffn_kernel, final=final),
        grid_spec=grid_spec,
        out_shape=jax.ShapeDtypeStruct(out_rows, F32),
        compiler_params=pltpu.CompilerParams(dimension_semantics=("arbitrary",),
                                             vmem_limit_bytes=VMEM_LIMIT),
        name="moe_ffn",
    )(tile_ea, tile_eb, tile_nvalid, n_used, src, src, src, xe, nw, fw, wg, wg, wu, wu, wd, wd)


def _push_rows_kernel(pad_lo_ref, pad_hi_ref, pos_ref, xe_ref, xs_hbm, buf0, buf1, zrow, sem, zsem,
                      *, n_steps):
    i = pl.program_id(0)
    ch = pos_ref.shape[0]
    bufs = (buf0, buf1)

    def zero_copy(k):
        return pltpu.make_async_copy(zrow, xs_hbm.at[pl.ds(k * SLAB, SLAB)], zsem)

    @pl.when(i == 0)
    def _():
        zrow[...] = jnp.zeros_like(zrow)
        for b in range(N_BUCKETS):
            def start(k, carry):
                zero_copy(k).start()
                return carry
            lax.fori_loop(pad_lo_ref[b], pad_hi_ref[b], start, 0)

    def burst_wait(p):
        pltpu.make_async_copy(bufs[p], xs_hbm.at[pl.ds(0, ch * SLAB)], sem.at[p]).wait()

    for p in (0, 1):
        @pl.when(i % 2 == p)
        def _():
            @pl.when(i >= 2)
            def _():
                burst_wait(p)

            bufs[p][...] = xe_ref[...]
            for r in range(ch):
                pltpu.make_async_copy(bufs[p].at[pl.ds(r * SLAB, SLAB)],
                                      xs_hbm.at[pl.ds(pos_ref[r] * SLAB, SLAB)],
                                      sem.at[p]).start(priority=r % 2)

    @pl.when(i == n_steps - 1)
    def _():
        burst_wait((n_steps - 1) % 2)
        if n_steps >= 2:
            burst_wait(n_steps % 2)
        for b in range(N_BUCKETS):
            def wait(k, carry):
                zero_copy(0).wait()
                return carry
            lax.fori_loop(pad_lo_ref[b], pad_hi_ref[b], wait, 0)


def _push_rows(pad_lo, pad_hi, pos, xe, n_slots):
    n_tok = pos.shape[0]
    ch = min(ROW_COPY_CHUNK, n_tok)
    return pl.pallas_call(
        functools.partial(_push_rows_kernel, n_steps=n_tok // ch),
        grid_spec=pltpu.PrefetchScalarGridSpec(
            num_scalar_prefetch=2,
            grid=(n_tok // ch,),
            in_specs=[pl.BlockSpec((ch,), lambda i, *_: (i,), memory_space=pltpu.SMEM),
                      pl.BlockSpec((ch * SLAB, LANES), lambda i, *_: (i, 0))],
            out_specs=pl.BlockSpec(memory_space=pl.ANY),
            scratch_shapes=[pltpu.VMEM((ch * SLAB, LANES), F32), pltpu.VMEM((ch * SLAB, LANES), F32),
                            pltpu.VMEM((SLAB, LANES), F32), pltpu.SemaphoreType.DMA((2,)),
                            pltpu.SemaphoreType.DMA(())],
        ),
        out_shape=jax.ShapeDtypeStruct((n_slots * SLAB, LANES), F32),
        compiler_params=pltpu.CompilerParams(dimension_semantics=("arbitrary",),
                                             vmem_limit_bytes=VMEM_LIMIT),
        name="push_rows",
    )(pad_lo, pad_hi, pos, xe)


def _pull_rows_kernel(pos0_ref, pos1_ref, ys_hbm, out_ref, buf0, buf1, sem):
    i = pl.program_id(0)
    n = pl.num_programs(0)
    tm = out_ref.shape[0]
    bufs = (buf0, buf1)

    def issue(pos_ref, p):
        for r in range(tm):
            pltpu.make_async_copy(ys_hbm.at[pl.ds(pos_ref[r] * SLAB, SLAB)],
                                  bufs[p].at[pl.ds(r * SLAB, SLAB)], sem.at[p]).start(priority=r % 2)

    @pl.when(i == 0)
    def _():
        issue(pos0_ref, 0)

    for p in (0, 1):
        @pl.when(i % 2 == p)
        def _():
            @pl.when(i + 1 < n)
            def _():
                issue(pos1_ref, 1 - p)

            pltpu.make_async_copy(ys_hbm.at[pl.ds(0, tm * SLAB)], bufs[p], sem.at[p]).wait()
            out_ref[...] = _slab_load(bufs[p], tm)


def _pull_rows(pos, ys):
    n_tok = pos.shape[0]
    tm = min(ROW_COPY_CHUNK, n_tok)
    nblk = n_tok // tm
    return pl.pallas_call(
        _pull_rows_kernel,
        grid=(nblk,),
        in_specs=[pl.BlockSpec((tm,), lambda i: (i,), memory_space=pltpu.SMEM),
                  pl.BlockSpec((tm,), lambda i: (jnp.minimum(i + 1, nblk - 1),), memory_space=pltpu.SMEM),
                  pl.BlockSpec(memory_space=pl.ANY)],
        out_specs=pl.BlockSpec((tm, D_MODEL), lambda i: (i, 0)),
        out_shape=jax.ShapeDtypeStruct((n_tok, D_MODEL), F32),
        scratch_shapes=[pltpu.VMEM((tm * SLAB, LANES), F32), pltpu.VMEM((tm * SLAB, LANES), F32),
                        pltpu.SemaphoreType.DMA((2,))],
        compiler_params=pltpu.CompilerParams(dimension_semantics=("arbitrary",),
                                             vmem_limit_bytes=VMEM_LIMIT),
        name="pull_rows",
    )(pos, pos, ys)


def _ffn_sorted_kernel(ea_ref, eb_ref, nvalid_ref, nused_ref,
                       xs_ref, nw_ref, fw_ref, wga_ref, wgb_ref, wua_ref, wub_ref, wda_ref, wdb_ref,
                       ys_ref, *, final):
    del ea_ref, eb_ref, nvalid_ref
    tm = TM_FFN
    j = pl.program_id(0)

    @pl.when(j < nused_ref[0])
    def _():
        xm = _slab_load(xs_ref, tm)
        route = xs_ref[pl.ds(D_MODEL // LANES, tm, stride=SLAB), :]
        w_lo = route[:, 1:2]
        w_hi = route[:, 2:3]
        h = _rms(xm, nw_ref[...]).astype(BF16)
        wt = lambda w_ref: w_ref[0, 0].astype(BF16)
        hid_a = _silu(jnp.dot(h, wt(wga_ref), preferred_element_type=F32)) \
            * jnp.dot(h, wt(wua_ref), preferred_element_type=F32)
        hid_b = _silu(jnp.dot(h, wt(wgb_ref), preferred_element_type=F32)) \
            * jnp.dot(h, wt(wub_ref), preferred_element_type=F32)
        y = (jnp.dot((hid_a * w_lo).astype(BF16), wt(wda_ref), preferred_element_type=F32)
             + jnp.dot((hid_b * w_hi).astype(BF16), wt(wdb_ref), preferred_element_type=F32))
        out = xm + y
        if final:
            out = _rms(out, fw_ref[...])
        for c in range(D_MODEL // LANES):
            ys_ref[pl.ds(c, tm, stride=SLAB), :] = out[:, c * LANES:(c + 1) * LANES]
        ys_ref[pl.ds(D_MODEL // LANES, tm, stride=SLAB), :] = route

    @pl.when(j >= nused_ref[0])
    def _():
        ys_ref[...] = jnp.zeros_like(ys_ref)


def _ffn_sorted(xs, tile_ea, tile_eb, tile_nvalid, n_used, nw, fw, layer, wg, wu, wd, final):
    tm = TM_FFN
    nt = tile_ea.shape[0]
    clamp = lambda j, nu: jnp.minimum(j, jnp.maximum(nu[0] - 1, 0))
    wa = lambda j, ea, eb, nv, nu: (layer, ea[clamp(j, nu)], 0, 0)
    wb = lambda j, ea, eb, nv, nu: (layer, eb[clamp(j, nu)], 0, 0)
    fixed = lambda j, ea, eb, nv, nu: (0, 0)
    grid_spec = pltpu.PrefetchScalarGridSpec(
        num_scalar_prefetch=4,
        grid=(nt,),
        in_specs=[pl.BlockSpec((tm * SLAB, LANES), lambda j, ea, eb, nv, nu: (clamp(j, nu), 0)),
                  pl.BlockSpec((1, D_MODEL), fixed), pl.BlockSpec((1, D_MODEL), fixed),
                  pl.BlockSpec((1, 1, D_MODEL, EXPERT_DIM), wa), pl.BlockSpec((1, 1, D_MODEL, EXPERT_DIM), wb),
                  pl.BlockSpec((1, 1, D_MODEL, EXPERT_DIM), wa), pl.BlockSpec((1, 1, D_MODEL, EXPERT_DIM), wb),
                  pl.BlockSpec((1, 1, EXPERT_DIM, D_MODEL), wa), pl.BlockSpec((1, 1, EXPERT_DIM, D_MODEL), wb)],
        out_specs=pl.BlockSpec((tm * SLAB, LANES), lambda j, ea, eb, nv, nu: (j, 0)),
    )
    return pl.pallas_call(
        functools.partial(_ffn_sorted_kernel, final=final),
        grid_spec=grid_spec,
        out_shape=jax.ShapeDtypeStruct((nt * tm * SLAB, LANES), F32),
        compiler_params=pltpu.CompilerParams(dimension_semantics=("arbitrary",),
                                             vmem_limit_bytes=VMEM_LIMIT),
        name="moe_ffn_sorted",
    )(tile_ea, tile_eb, tile_nvalid, n_used, xs, nw, fw, wg, wg, wu, wu, wd, wd)


def _pair_tables():
    lo, hi = [], []
    for a in range(EXPERTS_PER_GROUP):
        for b in range(a + 1, EXPERTS_PER_GROUP):
            lo.append(a)
            hi.append(b)
    return lo, hi


def _plan_tables_kernel(counts_ref, key_ref, pos_ref, tab_ref):
    tm = TM_FFN
    nt = tab_ref.shape[1]
    lo, hi = _pair_tables()
    key = key_ref[...]
    bucket = key >> RANK_BITS
    pos = key & (RANK_SPAN - 1)
    tile0 = 0
    for b in range(N_BUCKETS):
        cnt = counts_ref[b]
        n_tiles = (cnt + (tm - 1)) // tm
        pos = pos + jnp.where(bucket == b, tile0 * tm, 0)
        tab_ref[4, b] = tile0 * tm + cnt
        tab_ref[5, b] = (tile0 + n_tiles) * tm if b + 1 < N_BUCKETS else nt * tm
        ea = (b // N_PAIRS) * EXPERTS_PER_GROUP + lo[b % N_PAIRS]
        eb = (b // N_PAIRS) * EXPERTS_PER_GROUP + hi[b % N_PAIRS]

        def tile(k, carry, tile0=tile0, cnt=cnt, ea=ea, eb=eb):
            tab_ref[0, tile0 + k] = ea
            tab_ref[1, tile0 + k] = eb
            tab_ref[2, tile0 + k] = jnp.minimum(cnt - k * tm, tm)
            return carry
        lax.fori_loop(0, n_tiles, tile, 0)
        tile0 = tile0 + n_tiles
    n_used = tile0
    pos_ref[...] = pos

    def spare(k, carry):
        tab_ref[0, k] = 0
        tab_ref[1, k] = 0
        tab_ref[2, k] = 0
        return carry
    lax.fori_loop(n_used, nt, spare, 0)

    def rest(k, carry):
        tab_ref[3, k] = n_used
        return carry
    lax.fori_loop(0, nt, rest, 0)

    def unused(k, carry):
        tab_ref[4, k] = 0
        tab_ref[5, k] = 0
        return carry
    lax.fori_loop(N_BUCKETS, nt, unused, 0)


def _plan_invert_kernel(pad_lo_ref, pad_hi_ref, pos_ref, src_ref):
    i = pl.program_id(0)
    ch = pos_ref.shape[0]

    @pl.when(i == 0)
    def _():
        for b in range(N_BUCKETS):
            def zero(k, carry):
                src_ref[k] = 0
                return carry
            lax.fori_loop(pad_lo_ref[b], pad_hi_ref[b], zero, 0)

    def body(k, carry):
        for u in range(PLAN_UNROLL):
            t = k * PLAN_UNROLL + u
            src_ref[pos_ref[t]] = i * ch + t
        return carry
    lax.fori_loop(0, ch // PLAN_UNROLL, body, 0)


def _dispatch_plan(route_i, counts, n_tok):
    tm = TM_FFN
    nt = n_tok // tm + N_BUCKETS
    assert n_tok <= RANK_SPAN and n_tok % LANES == 0
    key = route_i[:, 0, :].reshape(n_tok // LANES, LANES)
    pos, tab = pl.pallas_call(
        _plan_tables_kernel,
        grid_spec=pltpu.PrefetchScalarGridSpec(
            num_scalar_prefetch=1,
            grid=(1,),
            in_specs=[pl.BlockSpec(key.shape, lambda i, c: (0, 0))],
            out_specs=[pl.BlockSpec(key.shape, lambda i, c: (0, 0)),
                       pl.BlockSpec(memory_space=pltpu.SMEM)],
        ),
        out_shape=[jax.ShapeDtypeStruct(key.shape, jnp.int32),
                   jax.ShapeDtypeStruct((6, nt), jnp.int32)],
        compiler_params=pltpu.CompilerParams(dimension_semantics=("arbitrary",)),
        name="dispatch_tables",
    )(counts, key)
    return tab, pos.reshape(n_tok)


def kernel(x, norm_mix_w, w_in, conv_w, conv_b, dt_bias, a_log, d_skip, ssd_norm_w, hgrn_lb_logits,
           hgrn_norm_w, w_out, norm_ffn_w, router_group, router_expert, w_gate, w_up, w_down,
           final_norm_w):
    bsz, seq, _ = x.shape
    depth = w_in.shape[0]
    n_tok = bsz * seq
    assert n_tok % TM_PROJ == 0 and seq % TS_MIX == 0 and n_tok % TM_FFN == 0

    p = jax.nn.softmax(hgrn_lb_logits.astype(F32), axis=0)
    lower_bounds = jnp.cumsum(p, axis=0) - p[0:1]

    o_xbc = SSD_WIDTH
    o_dt = o_xbc + XBC_WIDTH
    o_q = o_dt + SSD_HEADS
    pad_h = lambda v: jnp.pad(v.astype(F32), (0, LANES - SSD_HEADS)).reshape(1, LANES)
    row = lambda v: v.astype(F32).reshape(1, -1)

    xcur = x.reshape(n_tok, D_MODEL)
    for l in range(depth):
        wl = w_in[l].astype(BF16)
        wq, wf, wi, wg = (wl[:, o_q + k * HGRN_WIDTH:o_q + (k + 1) * HGRN_WIDTH] for k in range(4))
        wdt = jnp.pad(wl[:, o_dt:o_q], ((0, 0), (0, LANES - SSD_HEADS)))
        z, xbc, q, f, i, g, dt = _inproj(xcur, n_tok, False, row(norm_mix_w[l]), wl[:, :o_xbc],
                                         wl[:, o_xbc:o_dt], wq, wf, wi, wg, wdt)
        y_ssd = _ssd(z, xbc, dt, bsz, seq, conv_w[l].astype(F32), row(conv_b[l]), dt_bias[l],
                     a_log[l], row(jnp.repeat(d_skip[l], SSD_HEAD_DIM)), row(ssd_norm_w[l]))
        y_hgrn = _hgrn(q, f, i, g, bsz, seq, row(lower_bounds[l]), row(hgrn_norm_w[l]))
        wo = w_out[l].astype(BF16)
        wr_t = jnp.concatenate([router_group[l], router_expert[l]], axis=1).T
        wr_t = jnp.pad(wr_t, ((0, LANES - wr_t.shape[0]), (0, 0))).astype(BF16)
        xe, route, cnt = _outproj(y_ssd, y_hgrn, xcur, n_tok, False, wo[:SSD_WIDTH], wo[SSD_WIDTH:],
                                  row(norm_ffn_w[l]), wr_t)
        counts = cnt[:N_BUCKETS, 0].astype(jnp.int32)
        tab, pos = _dispatch_plan(route, counts, n_tok)
        n_slots = tab.shape[1] * TM_FFN
        xs = _push_rows(tab[4, :N_BUCKETS], tab[5, :N_BUCKETS], pos, xe, n_slots)
        ys = _ffn_sorted(xs, tab[0], tab[1], tab[2], tab[3, :1], row(norm_ffn_w[l]),
                         row(final_norm_w), l, w_gate, w_up, w_down, final=(l == depth - 1))
        xcur = _pull_rows(pos, ys)
    return xcur.reshape(bsz, seq, D_MODEL)
```

```python
import functools

import jax
import jax.numpy as jnp
from jax import lax
from jax.experimental import pallas as pl
from jax.experimental.pallas import tpu as pltpu

F32 = jnp.float32
BF16 = jnp.bfloat16

D_MODEL = 1024
SSD_HEADS = 8
SSD_HEAD_DIM = 64
SSD_WIDTH = SSD_HEADS * SSD_HEAD_DIM
SSD_GROUPS = 2
SSD_STATE = 128
CONV_WIDTH = 4
XBC_WIDTH = SSD_WIDTH + 2 * SSD_GROUPS * SSD_STATE
HGRN_HEADS = 4
HGRN_HEAD_DIM = 128
HGRN_WIDTH = HGRN_HEADS * HGRN_HEAD_DIM
HGRN_BLOCK = 32
N_EXPERT_GROUPS = 4
EXPERTS_PER_GROUP = 4
N_EXPERTS = N_EXPERT_GROUPS * EXPERTS_PER_GROUP
EXPERT_DIM = 256
EPS = 1e-6

LANES = 128
N_PAIRS = 6
N_BUCKETS = N_EXPERT_GROUPS * N_PAIRS
ROUTE_LANES = LANES
SLAB = D_MODEL // LANES + 1
NEG_BIG = -1e30
LOG2E = 1.4426950408889634

TM_PROJ = 512
TS_MIX = 256
MIX_BATCH = 4
SSD_SUB = 128
HGRN_SUB = 128
CONV_HALO = 8
TM_FFN = 256
RANK_BITS = 16
RANK_SPAN = 1 << RANK_BITS
ROW_COPY_CHUNK = 512
VMEM_LIMIT = 56 * 1024 * 1024


def _mm(a, b):
    return jnp.dot(a.astype(BF16), b.astype(BF16), preferred_element_type=F32)


def _mm_nt(a, b):
    return lax.dot_general(a.astype(BF16), b.astype(BF16), (((1,), (1,)), ((), ())),
                           preferred_element_type=F32)


def _mm_tn(a, b):
    return lax.dot_general(a.astype(BF16), b.astype(BF16), (((0,), (0,)), ((), ())),
                           preferred_element_type=F32)


def _split3(x):
    p1 = x.astype(BF16)
    r1 = x - p1.astype(F32)
    p2 = r1.astype(BF16)
    p3 = (r1 - p2.astype(F32)).astype(BF16)
    return p1, p2, p3


def _sigmoid(x):
    return 0.5 * jnp.tanh(0.5 * x) + 0.5


def _silu(x):
    hx = 0.5 * x
    return hx + hx * jnp.tanh(hx)


def _rms(x, w):
    return x * lax.rsqrt(jnp.mean(x * x, axis=-1, keepdims=True) + EPS) * w


def _slab_load(ref, rows):
    return jnp.concatenate([ref[pl.ds(c, rows, stride=SLAB), :] for c in range(D_MODEL // LANES)],
                           axis=1)


PROJ_WIDTHS = (SSD_WIDTH, XBC_WIDTH, HGRN_WIDTH, HGRN_WIDTH, HGRN_WIDTH, HGRN_WIDTH, LANES)
PROJ_DTYPES = (BF16,) * 6 + (F32,)


def _project(x, nw_ref, w_refs, o_refs):
    h = _rms(x, nw_ref[...]).astype(BF16)
    for w_ref, o_ref in zip(w_refs, o_refs):
        o_ref[...] = jnp.dot(h, w_ref[...], preferred_element_type=F32).astype(o_ref.dtype)


def _inproj_kernel(x_ref, nw_ref, *refs):
    _project(x_ref[...], nw_ref, refs[:len(PROJ_WIDTHS)], refs[len(PROJ_WIDTHS):])


def _inproj_pull_kernel(pos0_ref, pos1_ref, ys_hbm, nw_ref, *refs):
    nw = len(PROJ_WIDTHS)
    w_refs, x_out_ref, o_refs = refs[:nw], refs[nw], refs[nw + 1:2 * nw + 1]
    bufs, sem = refs[2 * nw + 1:2 * nw + 3], refs[2 * nw + 3]
    i = pl.program_id(0)
    n = pl.num_programs(0)
    tm = x_out_ref.shape[0]

    def issue(pos_ref, p):
        for r in range(tm):
            pltpu.make_async_copy(ys_hbm.at[pl.ds(pos_ref[r] * SLAB, SLAB)],
                                  bufs[p].at[pl.ds(r * SLAB, SLAB)], sem.at[p]).start(priority=r % 2)

    @pl.when(i == 0)
    def _():
        issue(pos0_ref, 0)

    for p in (0, 1):
        @pl.when(i % 2 == p)
        def _():
            @pl.when(i + 1 < n)
            def _():
                issue(pos1_ref, 1 - p)

            pltpu.make_async_copy(ys_hbm.at[pl.ds(0, tm * SLAB)], bufs[p], sem.at[p]).wait()
            x = _slab_load(bufs[p], tm)
            x_out_ref[...] = x
            _project(x, nw_ref, w_refs, o_refs)


def _inproj_pull(pos, ys, n_tok, nw, *weights):
    tm = TM_PROJ
    nblk = n_tok // tm
    row = lambda i: (i, 0)
    fixed = lambda i: (0, 0)
    outs = pl.pallas_call(
        _inproj_pull_kernel,
        grid=(nblk,),
        in_specs=[pl.BlockSpec((tm,), lambda i: (i,), memory_space=pltpu.SMEM),
                  pl.BlockSpec((tm,), lambda i: (jnp.minimum(i + 1, nblk - 1),), memory_space=pltpu.SMEM),
                  pl.BlockSpec(memory_space=pl.ANY), pl.BlockSpec((1, D_MODEL), fixed)]
        + [pl.BlockSpec((D_MODEL, w), fixed) for w in PROJ_WIDTHS],
        out_specs=[pl.BlockSpec((tm, D_MODEL), row)] + [pl.BlockSpec((tm, w), row) for w in PROJ_WIDTHS],
        out_shape=[jax.ShapeDtypeStruct((n_tok, D_MODEL), F32)]
        + [jax.ShapeDtypeStruct((n_tok, w), dt) for w, dt in zip(PROJ_WIDTHS, PROJ_DTYPES)],
        scratch_shapes=[pltpu.VMEM((tm * SLAB, LANES), F32), pltpu.VMEM((tm * SLAB, LANES), F32),
                        pltpu.SemaphoreType.DMA((2,))],
        compiler_params=pltpu.CompilerParams(dimension_semantics=("arbitrary",),
                                             vmem_limit_bytes=VMEM_LIMIT),
        name="inproj_pull",
    )(pos, pos, ys, nw, *weights)
    return outs[0], outs[1:]


def _inproj(x, n_tok, nw, wz, wxbc, wq, wf, wi, wg, wdt):
    tm = TM_PROJ
    row = lambda i: (i, 0)
    fixed = lambda i: (0, 0)
    widths = PROJ_WIDTHS
    dtypes = PROJ_DTYPES
    return pl.pallas_call(
        _inproj_kernel,
        grid=(n_tok // tm,),
        in_specs=[pl.BlockSpec((tm, D_MODEL), row), pl.BlockSpec((1, D_MODEL), fixed)]
        + [pl.BlockSpec((D_MODEL, w), fixed) for w in widths],
        out_specs=[pl.BlockSpec((tm, w), row) for w in widths],
        out_shape=[jax.ShapeDtypeStruct((n_tok, w), dt) for w, dt in zip(widths, dtypes)],
        compiler_params=pltpu.CompilerParams(dimension_semantics=("arbitrary",),
                                             vmem_limit_bytes=VMEM_LIMIT),
        name="inproj",
    )(x, nw, wz, wxbc, wq, wf, wi, wg, wdt)


def _ssd_kernel(z_ref, xbc_ref, dt_ref, cw_ref, cb_ref, dtb_ref, alog_ref, dsk_ref, nw_ref, ex_ref,
                y_ref, ext_ref, h_ref):
    nbat, ts, _ = z_ref.shape
    L = SSD_SUB
    hpg = SSD_HEADS // SSD_GROUPS
    gw = SSD_WIDTH // SSD_GROUPS
    nh = SSD_HEADS
    pre = CONV_HALO
    streams = [(bb, c) for bb in range(nbat) for c in range(ts // L)]

    nlc = XBC_WIDTH // LANES

    @pl.when(pl.program_id(1) == 0)
    def _():
        ext_ref[...] = jnp.zeros_like(ext_ref)
        h_ref[...] = jnp.zeros_like(h_ref)

    for bb in range(nbat):
        for j in range(nlc):
            ext_ref[bb, j, pl.ds(pre, ts, stride=2), :] = xbc_ref[bb, :, j * LANES:(j + 1) * LANES].astype(F32)

    r_i = lax.broadcasted_iota(jnp.int32, (L, L), 0)
    c_i = lax.broadcasted_iota(jnp.int32, (L, L), 1)
    causal = r_i >= c_i
    triu = (r_i <= c_i).astype(BF16)
    even = lax.broadcasted_iota(jnp.int32, (L, LANES), 1) < SSD_HEAD_DIM

    lane_tile = lambda v: jnp.concatenate([v] * (ts // LANES), axis=1)
    dt_all, da_all, ldt_all = {}, {}, {}
    for bb in range(nbat):
        dt_raw = dt_ref[bb].T[0:nh, :] + lane_tile(dtb_ref[...])
        dt_all[bb] = jnp.maximum(dt_raw, 0.0) + jnp.log(1.0 + jnp.exp(-jnp.abs(dt_raw)))
        da_all[bb] = dt_all[bb] * lane_tile(-jnp.exp(alog_ref[...]) * LOG2E)
        ldt_all[bb] = jnp.log2(dt_all[bb])

    conv = {}
    for bb, c in streams:
        r0 = c * L
        pieces = []
        for j in range(nlc):
            ls = slice(j * LANES, (j + 1) * LANES)
            acc = cb_ref[:, ls]
            for k in range(CONV_WIDTH):
                tap = ext_ref[bb, j, pl.ds(pre + 2 * (r0 - (CONV_WIDTH - 1 - k)), L, stride=2), :]
                acc = acc + cw_ref[k:k + 1, ls] * tap
            pieces.append(_silu(acc))
        conv[bb, c] = jnp.concatenate(pieces, axis=1)

    cols, key_row, e_cum, xsc = {}, {}, {}, {}
    for bb, c in streams:
        r0 = c * L
        dt = dt_all[bb][:, r0:r0 + L]
        p1, p2, p3 = _split3(da_all[bb][:, r0:r0 + L])
        cum = (jnp.dot(p1, triu, preferred_element_type=F32) + jnp.dot(p2, triu, preferred_element_type=F32)
               + jnp.dot(p3, triu, preferred_element_type=F32))
        cum_last = cum[:, L - 1:L]
        key_row[bb, c] = cum - ldt_all[bb][:, r0:r0 + L]
        rows = jnp.concatenate([cum, jnp.exp2(cum), jnp.exp2(cum_last - cum) * dt,
                                jnp.zeros((LANES - 3 * nh, L), F32)], axis=0)
        cols[bb, c] = rows.T
        hi = cols[bb, c].astype(BF16)
        lo = (cols[bb, c] - hi.astype(F32)).astype(BF16)
        spread = jnp.dot(jnp.concatenate([hi, lo], axis=1), ex_ref[...],
                         preferred_element_type=F32)
        e_cum[bb, c] = spread[:, :SSD_WIDTH]
        xsc[bb, c] = conv[bb, c][:, :SSD_WIDTH] * spread[:, SSD_WIDTH:]

    y_intra = {}
    for bb, c in streams:
        xs = conv[bb, c][:, :SSD_WIDTH]
        for g in range(SSD_GROUPS):
            lo_b = SSD_WIDTH + g * SSD_STATE
            lo_c = SSD_WIDTH + (SSD_GROUPS + g) * SSD_STATE
            cb = _mm_nt(conv[bb, c][:, lo_c:lo_c + SSD_STATE], conv[bb, c][:, lo_b:lo_b + SSD_STATE])
            y_parts = []
            for p in range(hpg // 2):
                h0 = g * hpg + 2 * p
                xp = xs[:, h0 * SSD_HEAD_DIM:(h0 + 2) * SSD_HEAD_DIM]
                scores = []
                for hh in (h0, h0 + 1):
                    seg = cols[bb, c][:, hh:hh + 1] - key_row[bb, c][hh:hh + 1, :]
                    scores.append((cb * jnp.exp2(jnp.where(causal, seg, NEG_BIG))).astype(BF16))
                rhs = jnp.concatenate([jnp.where(even, xp, 0.0), jnp.where(even, 0.0, xp)],
                                      axis=0).astype(BF16)
                y_parts.append(jnp.dot(jnp.concatenate(scores, axis=1), rhs,
                                       preferred_element_type=F32))
            y_intra[bb, c, g] = jnp.concatenate(y_parts, axis=1)

    for c in range(ts // L):
        r0 = c * L
        for bb in range(nbat):
            xs = conv[bb, c][:, :SSD_WIDTH]
            zc = z_ref[bb, r0:r0 + L, :].astype(F32)
            e_last = e_cum[bb, c][L - 1:L, :]
            for g in range(SSD_GROUPS):
                gs = slice(g * gw, (g + 1) * gw)
                lo_b = SSD_WIDTH + g * SSD_STATE
                lo_c = SSD_WIDTH + (SSD_GROUPS + g) * SSD_STATE
                h_t = h_ref[bb, g]
                yg = (y_intra[bb, c, g] + _mm(conv[bb, c][:, lo_c:lo_c + SSD_STATE], h_t) * e_cum[bb, c][:, gs]
                      + dsk_ref[:, gs] * xs[:, gs])
                h_ref[bb, g] = h_t * e_last[:, gs] + _mm_tn(conv[bb, c][:, lo_b:lo_b + SSD_STATE],
                                                            xsc[bb, c][:, gs])
                yg = _rms(yg * _silu(zc[:, gs]), nw_ref[:, gs])
                y_ref[bb, r0:r0 + L, gs] = yg.astype(y_ref.dtype)

    ext_ref[:, :, 0:pre, :] = ext_ref[:, :, 2 * ts:2 * ts + pre, :]


def _head_spread_matrix():
    col = jnp.arange(LANES)[:, None]
    out = jnp.arange(2 * SSD_WIDTH)[None, :]
    want = SSD_HEADS * (1 + out // SSD_WIDTH) + (out % SSD_WIDTH) // SSD_HEAD_DIM
    once = (col == want).astype(BF16)
    return jnp.concatenate([once, once], axis=0)


def _ssd(z, xbc, dt, bsz, seq, cw, cb, dtb, alog, dsk, nw):
    ts = TS_MIX
    nbat = MIX_BATCH
    tile = lambda b, s: (b, s, 0)
    fixed = lambda b, s: (0, 0)
    as3d = lambda v: v.reshape(bsz, seq, v.shape[-1])
    per_head_rows = lambda v: jnp.broadcast_to(v.astype(F32)[:, None], (SSD_HEADS, LANES))
    out = pl.pallas_call(
        _ssd_kernel,
        grid=(bsz // nbat, seq // ts),
        in_specs=[pl.BlockSpec((nbat, ts, SSD_WIDTH), tile), pl.BlockSpec((nbat, ts, XBC_WIDTH), tile),
                  pl.BlockSpec((nbat, ts, LANES), tile),
                  pl.BlockSpec((CONV_WIDTH, XBC_WIDTH), fixed), pl.BlockSpec((1, XBC_WIDTH), fixed),
                  pl.BlockSpec((SSD_HEADS, LANES), fixed), pl.BlockSpec((SSD_HEADS, LANES), fixed),
                  pl.BlockSpec((1, SSD_WIDTH), fixed), pl.BlockSpec((1, SSD_WIDTH), fixed),
                  pl.BlockSpec((2 * LANES, 2 * SSD_WIDTH), fixed)],
        out_specs=pl.BlockSpec((nbat, ts, SSD_WIDTH), tile),
        out_shape=jax.ShapeDtypeStruct((bsz, seq, SSD_WIDTH), BF16),
        scratch_shapes=[pltpu.VMEM((nbat, XBC_WIDTH // LANES, 2 * ts + CONV_HALO, LANES), F32),
                        pltpu.VMEM((nbat, SSD_GROUPS, SSD_STATE, SSD_WIDTH // SSD_GROUPS), F32)],
        compiler_params=pltpu.CompilerParams(dimension_semantics=("arbitrary", "arbitrary"),
                                             vmem_limit_bytes=VMEM_LIMIT),
        name="ssd",
    )(as3d(z), as3d(xbc), as3d(dt), cw, cb, per_head_rows(dtb), per_head_rows(alog), dsk, nw,
      _head_spread_matrix())
    return out.reshape(bsz * seq, SSD_WIDTH)


def _hgrn_body(q_ref, f_ref, i_ref, g_ref, lb_ref, nw_ref, o_ref, st_ref):
    ts = q_ref.shape[0]
    blk = HGRN_BLOCK
    nb = ts // blk
    hd = HGRN_HEAD_DIM

    @pl.when(pl.program_id(1) == 0)
    def _():
        st_ref[...] = jnp.zeros_like(st_ref)

    sub = HGRN_SUB
    r_i = lax.broadcasted_iota(jnp.int32, (sub, sub), 0)
    c_i = lax.broadcasted_iota(jnp.int32, (sub, sub), 1)
    blk_causal = (r_i >= c_i) & ((r_i // blk) == (c_i // blk))
    row_in_blk = lax.broadcasted_iota(jnp.int32, (ts, hd), 0) % blk

    heads = []
    for h in range(HGRN_HEADS):
        sl = slice(h * hd, (h + 1) * hd)
        lb = lb_ref[:, sl]
        forget = lb + (1.0 - lb) * _sigmoid(f_ref[:, sl].astype(F32))
        kk = 1.0 - forget
        cum = jnp.log(forget)
        shift = 1
        while shift < blk:
            cum = cum + jnp.where(row_in_blk >= shift, pltpu.roll(cum, shift, axis=0), 0.0)
            shift *= 2
        cum = cum * LOG2E
        cum3 = cum.reshape(nb, blk, hd)
        b_mid = cum3[:, blk // 2:blk // 2 + 1, :]
        b_end = cum3[:, blk - 1:blk, :]
        rel = (cum3 - b_mid).reshape(ts, hd)
        to_end = (b_end - cum3).reshape(ts, hd)
        q = q_ref[:, sl].astype(F32)
        v = i_ref[:, sl].astype(BF16)
        heads.append(dict(sl=sl, v=v, o_intra=[], o_inter=[],
                          qs=(q * jnp.exp2(rel)).astype(BF16),
                          ks=(kk * jnp.exp2(-rel)).astype(BF16),
                          q_dec=(q * jnp.exp2(cum)).astype(BF16),
                          k_end=(kk * jnp.exp2(to_end)).astype(BF16),
                          dec=jnp.exp2(b_end), st=st_ref[h]))

    def pair_diag(x0, x1):
        z = jnp.zeros_like(x0)
        return jnp.concatenate([jnp.concatenate([x0, z], axis=1),
                                jnp.concatenate([z, x1], axis=1)], axis=0)

    pairs = [(heads[i], heads[i + 1]) for i in range(0, HGRN_HEADS, 2)]
    mask2 = jnp.concatenate([blk_causal, blk_causal], axis=1)

    for a, b in pairs:
        for c in range(ts // sub):
            cs = slice(c * sub, (c + 1) * sub)
            sc = _mm_nt(jnp.concatenate([a["qs"][cs], b["qs"][cs]], axis=1),
                        pair_diag(a["ks"][cs], b["ks"][cs]))
            sc = jnp.where(mask2, sc, 0.0).astype(BF16)
            o = jnp.dot(sc, pair_diag(a["v"][cs], b["v"][cs]), preferred_element_type=F32)
            a["o_intra"].append(o[:, :hd])
            b["o_intra"].append(o[:, hd:])

    for n in range(nb):
        rs = slice(n * blk, (n + 1) * blk)
        for a, b in pairs:
            o = _mm_nt(jnp.concatenate([a["q_dec"][rs], b["q_dec"][rs]], axis=1),
                       pair_diag(a["st"].astype(BF16), b["st"].astype(BF16)))
            a["o_inter"].append(o[:, :hd])
            b["o_inter"].append(o[:, hd:])
            for hh in (a, b):
                hh["st"] = hh["st"] * hh["dec"][n] + _mm_tn(hh["v"][rs], hh["k_end"][rs])

    for h, hh in enumerate(heads):
        sl = hh["sl"]
        st_ref[h] = hh["st"]
        o = jnp.concatenate(hh["o_intra"], axis=0) + jnp.concatenate(hh["o_inter"], axis=0)
        o = _rms(o, nw_ref[:, sl]) * _silu(g_ref[:, sl].astype(F32))
        o_ref[:, sl] = o.astype(o_ref.dtype)


def _hgrn_kernel(q_ref, f_ref, i_ref, g_ref, lb_ref, nw_ref, o_ref, st_ref):
    for bb in range(q_ref.shape[0]):
        _hgrn_body(q_ref.at[bb], f_ref.at[bb], i_ref.at[bb], g_ref.at[bb], lb_ref, nw_ref,
                   o_ref.at[bb], st_ref.at[bb])


def _hgrn(q, f, i, g, bsz, seq, lb, nw):
    ts = TS_MIX
    nbat = MIX_BATCH
    tile = lambda b, s: (b, s, 0)
    fixed = lambda b, s: (0, 0)
    as3d = lambda v: v.reshape(bsz, seq, v.shape[-1])
    out = pl.pallas_call(
        _hgrn_kernel,
        grid=(bsz // nbat, seq // ts),
        in_specs=[pl.BlockSpec((nbat, ts, HGRN_WIDTH), tile)] * 4
        + [pl.BlockSpec((1, HGRN_WIDTH), fixed)] * 2,
        out_specs=pl.BlockSpec((nbat, ts, HGRN_WIDTH), tile),
        out_shape=jax.ShapeDtypeStruct((bsz, seq, HGRN_WIDTH), BF16),
        scratch_shapes=[pltpu.VMEM((nbat, HGRN_HEADS, HGRN_HEAD_DIM, HGRN_HEAD_DIM), F32)],
        compiler_params=pltpu.CompilerParams(dimension_semantics=("arbitrary", "arbitrary"),
                                             vmem_limit_bytes=VMEM_LIMIT),
        name="hgrn",
    )(as3d(q), as3d(f), as3d(i), as3d(g), lb, nw)
    return out.reshape(bsz * seq, HGRN_WIDTH)


def _first_argmax4(v):
    m = jnp.maximum(jnp.maximum(v[0], v[1]), jnp.maximum(v[2], v[3]))
    idx = jnp.where(v[0] == m, 0.0, jnp.where(v[1] == m, 1.0, jnp.where(v[2] == m, 2.0, 3.0)))
    return m, idx


def _outproj_kernel(ys_ref, yh_ref, x_ref, wo1_ref, wo2_ref, nw_ref, wr_ref, xe_ref, rt_ref, cnt_ref,
                    base_ref):
    tm = ys_ref.shape[0]

    @pl.when(pl.program_id(0) == 0)
    def _():
        base_ref[...] = jnp.zeros_like(base_ref)

    xm = (x_ref[...] + jnp.dot(ys_ref[...], wo1_ref[...], preferred_element_type=F32)
          + jnp.dot(yh_ref[...], wo2_ref[...], preferred_element_type=F32))
    h = _rms(xm, nw_ref[...])
    lt = _mm_nt(wr_ref[...], h)
    row = lambda r: lt[r:r + 1, :]
    gl = [row(j) for j in range(N_EXPERT_GROUPS)]
    g_max, g_idx = _first_argmax4(gl)
    g_w = 1.0 / (jnp.exp(gl[0] - g_max) + jnp.exp(gl[1] - g_max)
                 + jnp.exp(gl[2] - g_max) + jnp.exp(gl[3] - g_max))
    el = []
    for j in range(EXPERTS_PER_GROUP):
        base = N_EXPERT_GROUPS + j
        el.append(jnp.where(g_idx == 0.0, row(base),
                  jnp.where(g_idx == 1.0, row(base + EXPERTS_PER_GROUP),
                  jnp.where(g_idx == 2.0, row(base + 2 * EXPERTS_PER_GROUP),
                            row(base + 3 * EXPERTS_PER_GROUP)))))
    e1, i1 = _first_argmax4(el)
    el2 = [jnp.where(i1 == float(j), NEG_BIG, el[j]) for j in range(EXPERTS_PER_GROUP)]
    e2, i2 = _first_argmax4(el2)
    w1 = 1.0 / (1.0 + jnp.exp(e2 - e1))
    w2 = jnp.exp(e2 - e1) * w1
    lo = jnp.minimum(i1, i2)
    hi = jnp.maximum(i1, i2)
    w_lo = g_w * jnp.where(i1 < i2, w1, w2)
    w_hi = g_w * jnp.where(i1 < i2, w2, w1)
    pair = lo * (7.0 - lo) * 0.5 + (hi - lo - 1.0)
    bucket = g_idx * float(N_PAIRS) + pair
    sub_l = lax.broadcasted_iota(jnp.int32, (ROUTE_LANES, tm), 0)
    onehot = jnp.where(sub_l.astype(F32) == bucket, 1.0, 0.0)
    r_i = lax.broadcasted_iota(jnp.int32, (tm, tm), 0)
    c_i = lax.broadcasted_iota(jnp.int32, (tm, tm), 1)
    before = jnp.dot(onehot.astype(BF16), (r_i < c_i).astype(BF16), preferred_element_type=F32)
    base = base_ref[...]
    before = before + jnp.concatenate([base] * (tm // LANES), axis=1)
    rank = jnp.sum(onehot * before, axis=0, keepdims=True)
    base = base + jnp.sum(onehot, axis=1, keepdims=True)
    base_ref[...] = base
    cnt_ref[...] = base
    sub = lax.broadcasted_iota(jnp.int32, (8, tm), 0)
    rt_ref[0] = jnp.where(sub == 0, bucket * float(RANK_SPAN) + rank, 0.0).astype(jnp.int32)
    wide = jnp.where(sub_l == 0, bucket, jnp.where(sub_l == 1, w_lo, jnp.where(sub_l == 2, w_hi, 0.0)))
    for c in range(D_MODEL // LANES):
        xe_ref[pl.ds(c, tm, stride=SLAB), :] = xm[:, c * LANES:(c + 1) * LANES]
    xe_ref[pl.ds(D_MODEL // LANES, tm, stride=SLAB), :] = wide.T


def _outproj(ys, yh, x, n_tok, wo1, wo2, nw, wr_t):
    tm = TM_PROJ
    row = lambda i: (i, 0)
    fixed = lambda i: (0, 0)
    return pl.pallas_call(
        _outproj_kernel,
        grid=(n_tok // tm,),
        in_specs=[pl.BlockSpec((tm, SSD_WIDTH), row), pl.BlockSpec((tm, HGRN_WIDTH), row),
                  pl.BlockSpec((tm, D_MODEL), row),
                  pl.BlockSpec((SSD_WIDTH, D_MODEL), fixed), pl.BlockSpec((HGRN_WIDTH, D_MODEL), fixed),
                  pl.BlockSpec((1, D_MODEL), fixed), pl.BlockSpec((LANES, D_MODEL), fixed)],
        out_specs=[pl.BlockSpec((tm * SLAB, LANES), row), pl.BlockSpec((1, 8, tm), lambda i: (i, 0, 0)),
                   pl.BlockSpec((ROUTE_LANES, LANES), fixed)],
        out_shape=[jax.ShapeDtypeStruct((n_tok * SLAB, LANES), F32),
                   jax.ShapeDtypeStruct((n_tok // tm, 8, tm), jnp.int32),
                   jax.ShapeDtypeStruct((ROUTE_LANES, LANES), F32)],
        scratch_shapes=[pltpu.VMEM((ROUTE_LANES, LANES), F32)],
        compiler_params=pltpu.CompilerParams(dimension_semantics=("arbitrary",),
                                             vmem_limit_bytes=VMEM_LIMIT),
        name="outproj_router",
    )(ys, yh, x, wo1, wo2, nw, wr_t)


def _push_rows_kernel(pad_lo_ref, pad_hi_ref, pos_ref, xe_ref, xs_hbm, buf0, buf1, zrow, sem, zsem,
                      *, n_steps):
    i = pl.program_id(0)
    ch = pos_ref.shape[0]
    bufs = (buf0, buf1)

    def zero_copy(k):
        return pltpu.make_async_copy(zrow, xs_hbm.at[pl.ds(k * SLAB, SLAB)], zsem)

    @pl.when(i == 0)
    def _():
        zrow[...] = jnp.zeros_like(zrow)
        for b in range(N_BUCKETS):
            def start(k, carry):
                zero_copy(k).start()
                return carry
            lax.fori_loop(pad_lo_ref[b], pad_hi_ref[b], start, 0)

    def burst_wait(p):
        pltpu.make_async_copy(bufs[p], xs_hbm.at[pl.ds(0, ch * SLAB)], sem.at[p]).wait()

    for p in (0, 1):
        @pl.when(i % 2 == p)
        def _():
            @pl.when(i >= 2)
            def _():
                burst_wait(p)

            bufs[p][...] = xe_ref[...]
            for r in range(ch):
                pltpu.make_async_copy(bufs[p].at[pl.ds(r * SLAB, SLAB)],
                                      xs_hbm.at[pl.ds(pos_ref[r] * SLAB, SLAB)],
                                      sem.at[p]).start(priority=r % 2)

    @pl.when(i == n_steps - 1)
    def _():
        burst_wait((n_steps - 1) % 2)
        if n_steps >= 2:
            burst_wait(n_steps % 2)
        for b in range(N_BUCKETS):
            def wait(k, carry):
                zero_copy(0).wait()
                return carry
            lax.fori_loop(pad_lo_ref[b], pad_hi_ref[b], wait, 0)


def _push_rows(pad_lo, pad_hi, pos, xe, n_slots):
    n_tok = pos.shape[0]
    ch = min(ROW_COPY_CHUNK, n_tok)
    return pl.pallas_call(
        functools.partial(_push_rows_kernel, n_steps=n_tok // ch),
        grid_spec=pltpu.PrefetchScalarGridSpec(
            num_scalar_prefetch=2,
            grid=(n_tok // ch,),
            in_specs=[pl.BlockSpec((ch,), lambda i, *_: (i,), memory_space=pltpu.SMEM),
                      pl.BlockSpec((ch * SLAB, LANES), lambda i, *_: (i, 0))],
            out_specs=pl.BlockSpec(memory_space=pl.ANY),
            scratch_shapes=[pltpu.VMEM((ch * SLAB, LANES), F32), pltpu.VMEM((ch * SLAB, LANES), F32),
                            pltpu.VMEM((SLAB, LANES), F32), pltpu.SemaphoreType.DMA((2,)),
                            pltpu.SemaphoreType.DMA(())],
        ),
        out_shape=jax.ShapeDtypeStruct((n_slots * SLAB, LANES), F32),
        compiler_params=pltpu.CompilerParams(dimension_semantics=("arbitrary",),
                                             vmem_limit_bytes=VMEM_LIMIT),
        name="push_rows",
    )(pad_lo, pad_hi, pos, xe)


def _pull_rows_kernel(pos0_ref, pos1_ref, ys_hbm, out_ref, buf0, buf1, sem):
    i = pl.program_id(0)
    n = pl.num_programs(0)
    tm = out_ref.shape[0]
    bufs = (buf0, buf1)

    def issue(pos_ref, p):
        for r in range(tm):
            pltpu.make_async_copy(ys_hbm.at[pl.ds(pos_ref[r] * SLAB, SLAB)],
                                  bufs[p].at[pl.ds(r * SLAB, SLAB)], sem.at[p]).start(priority=r % 2)

    @pl.when(i == 0)
    def _():
        issue(pos0_ref, 0)

    for p in (0, 1):
        @pl.when(i % 2 == p)
        def _():
            @pl.when(i + 1 < n)
            def _():
                issue(pos1_ref, 1 - p)

            pltpu.make_async_copy(ys_hbm.at[pl.ds(0, tm * SLAB)], bufs[p], sem.at[p]).wait()
            out_ref[...] = _slab_load(bufs[p], tm)


def _pull_rows(pos, ys):
    n_tok = pos.shape[0]
    tm = min(ROW_COPY_CHUNK, n_tok)
    nblk = n_tok // tm
    return pl.pallas_call(
        _pull_rows_kernel,
        grid=(nblk,),
        in_specs=[pl.BlockSpec((tm,), lambda i: (i,), memory_space=pltpu.SMEM),
                  pl.BlockSpec((tm,), lambda i: (jnp.minimum(i + 1, nblk - 1),), memory_space=pltpu.SMEM),
                  pl.BlockSpec(memory_space=pl.ANY)],
        out_specs=pl.BlockSpec((tm, D_MODEL), lambda i: (i, 0)),
        out_shape=jax.ShapeDtypeStruct((n_tok, D_MODEL), F32),
        scratch_shapes=[pltpu.VMEM((tm * SLAB, LANES), F32), pltpu.VMEM((tm * SLAB, LANES), F32),
                        pltpu.SemaphoreType.DMA((2,))],
        compiler_params=pltpu.CompilerParams(dimension_semantics=("arbitrary",),
                                             vmem_limit_bytes=VMEM_LIMIT),
        name="pull_rows",
    )(pos, pos, ys)


def _ffn_sorted_kernel(ea_ref, eb_ref, nused_ref,
                       xs_ref, nw_ref, fw_ref, wga_ref, wgb_ref, wua_ref, wub_ref, wda_ref, wdb_ref,
                       ys_ref, *, final):
    del ea_ref, eb_ref
    tm = TM_FFN
    j = pl.program_id(0)

    @pl.when(j < nused_ref[0])
    def _():
        xm = _slab_load(xs_ref, tm)
        route = xs_ref[pl.ds(D_MODEL // LANES, tm, stride=SLAB), :]
        w_lo = route[:, 1:2]
        w_hi = route[:, 2:3]
        h = _rms(xm, nw_ref[...]).astype(BF16)
        wt = lambda w_ref: w_ref[0, 0].astype(BF16)
        hid_a = _silu(jnp.dot(h, wt(wga_ref), preferred_element_type=F32)) \
            * jnp.dot(h, wt(wua_ref), preferred_element_type=F32)
        hid_b = _silu(jnp.dot(h, wt(wgb_ref), preferred_element_type=F32)) \
            * jnp.dot(h, wt(wub_ref), preferred_element_type=F32)
        y = (jnp.dot((hid_a * w_lo).astype(BF16), wt(wda_ref), preferred_element_type=F32)
             + jnp.dot((hid_b * w_hi).astype(BF16), wt(wdb_ref), preferred_element_type=F32))
        out = xm + y
        if final:
            out = _rms(out, fw_ref[...])
        for c in range(D_MODEL // LANES):
            ys_ref[pl.ds(c, tm, stride=SLAB), :] = out[:, c * LANES:(c + 1) * LANES]
        ys_ref[pl.ds(D_MODEL // LANES, tm, stride=SLAB), :] = route

    @pl.when(j >= nused_ref[0])
    def _():
        ys_ref[...] = jnp.zeros_like(ys_ref)


def _ffn_sorted(xs, tile_ea, tile_eb, n_used, nw, fw, layer, wg, wu, wd, final):
    tm = TM_FFN
    nt = tile_ea.shape[0]
    clamp = lambda j, nu: jnp.minimum(j, jnp.maximum(nu[0] - 1, 0))
    wa = lambda j, ea, eb, nu: (layer, ea[clamp(j, nu)], 0, 0)
    wb = lambda j, ea, eb, nu: (layer, eb[clamp(j, nu)], 0, 0)
    fixed = lambda j, ea, eb, nu: (0, 0)
    grid_spec = pltpu.PrefetchScalarGridSpec(
        num_scalar_prefetch=3,
        grid=(nt,),
        in_specs=[pl.BlockSpec((tm * SLAB, LANES), lambda j, ea, eb, nu: (clamp(j, nu), 0)),
                  pl.BlockSpec((1, D_MODEL), fixed), pl.BlockSpec((1, D_MODEL), fixed),
                  pl.BlockSpec((1, 1, D_MODEL, EXPERT_DIM), wa), pl.BlockSpec((1, 1, D_MODEL, EXPERT_DIM), wb),
                  pl.BlockSpec((1, 1, D_MODEL, EXPERT_DIM), wa), pl.BlockSpec((1, 1, D_MODEL, EXPERT_DIM), wb),
                  pl.BlockSpec((1, 1, EXPERT_DIM, D_MODEL), wa), pl.BlockSpec((1, 1, EXPERT_DIM, D_MODEL), wb)],
        out_specs=pl.BlockSpec((tm * SLAB, LANES), lambda j, ea, eb, nu: (j, 0)),
    )
    return pl.pallas_call(
        functools.partial(_ffn_sorted_kernel, final=final),
        grid_spec=grid_spec,
        out_shape=jax.ShapeDtypeStruct((nt * tm * SLAB, LANES), F32),
        compiler_params=pltpu.CompilerParams(dimension_semantics=("arbitrary",),
                                             vmem_limit_bytes=VMEM_LIMIT),
        name="moe_ffn_sorted",
    )(tile_ea, tile_eb, n_used, xs, nw, fw, wg, wg, wu, wu, wd, wd)


def _pair_tables():
    lo, hi = [], []
    for a in range(EXPERTS_PER_GROUP):
        for b in range(a + 1, EXPERTS_PER_GROUP):
            lo.append(a)
            hi.append(b)
    return lo, hi


def _plan_tables_kernel(counts_ref, key_ref, pos_ref, tab_ref):
    tm = TM_FFN
    nt = tab_ref.shape[1]
    lo, hi = _pair_tables()
    key = key_ref[...]
    bucket = key >> RANK_BITS
    pos = key & (RANK_SPAN - 1)
    tile0 = 0
    for b in range(N_BUCKETS):
        cnt = counts_ref[b]
        n_tiles = (cnt + (tm - 1)) // tm
        pos = pos + jnp.where(bucket == b, tile0 * tm, 0)
        tab_ref[4, b] = tile0 * tm + cnt
        tab_ref[5, b] = (tile0 + n_tiles) * tm if b + 1 < N_BUCKETS else nt * tm
        ea = (b // N_PAIRS) * EXPERTS_PER_GROUP + lo[b % N_PAIRS]
        eb = (b // N_PAIRS) * EXPERTS_PER_GROUP + hi[b % N_PAIRS]

        def tile(k, carry, tile0=tile0, cnt=cnt, ea=ea, eb=eb):
            tab_ref[0, tile0 + k] = ea
            tab_ref[1, tile0 + k] = eb
            tab_ref[2, tile0 + k] = jnp.minimum(cnt - k * tm, tm)
            return carry
        lax.fori_loop(0, n_tiles, tile, 0)
        tile0 = tile0 + n_tiles
    n_used = tile0
    pos_ref[...] = pos

    def spare(k, carry):
        tab_ref[0, k] = 0
        tab_ref[1, k] = 0
        tab_ref[2, k] = 0
        return carry
    lax.fori_loop(n_used, nt, spare, 0)

    def rest(k, carry):
        tab_ref[3, k] = n_used
        return carry
    lax.fori_loop(0, nt, rest, 0)

    def unused(k, carry):
        tab_ref[4, k] = 0
        tab_ref[5, k] = 0
        return carry
    lax.fori_loop(N_BUCKETS, nt, unused, 0)


def _dispatch_plan(route_i, counts, n_tok):
    tm = TM_FFN
    nt = n_tok // tm + N_BUCKETS
    assert n_tok <= RANK_SPAN and n_tok % LANES == 0
    key = route_i[:, 0, :].reshape(n_tok // LANES, LANES)
    pos, tab = pl.pallas_call(
        _plan_tables_kernel,
        grid_spec=pltpu.PrefetchScalarGridSpec(
            num_scalar_prefetch=1,
            grid=(1,),
            in_specs=[pl.BlockSpec(key.shape, lambda i, c: (0, 0))],
            out_specs=[pl.BlockSpec(key.shape, lambda i, c: (0, 0)),
                       pl.BlockSpec(memory_space=pltpu.SMEM)],
        ),
        out_shape=[jax.ShapeDtypeStruct(key.shape, jnp.int32),
                   jax.ShapeDtypeStruct((6, nt), jnp.int32)],
        compiler_params=pltpu.CompilerParams(dimension_semantics=("arbitrary",)),
        name="dispatch_tables",
    )(counts, key)
    return tab, pos.reshape(n_tok)


def kernel(x, norm_mix_w, w_in, conv_w, conv_b, dt_bias, a_log, d_skip, ssd_norm_w, hgrn_lb_logits,
           hgrn_norm_w, w_out, norm_ffn_w, router_group, router_expert, w_gate, w_up, w_down,
           final_norm_w):
    bsz, seq, _ = x.shape
    depth = w_in.shape[0]
    n_tok = bsz * seq
    assert n_tok % TM_PROJ == 0 and seq % TS_MIX == 0 and n_tok % TM_FFN == 0

    p = jax.nn.softmax(hgrn_lb_logits.astype(F32), axis=0)
    lower_bounds = jnp.cumsum(p, axis=0) - p[0:1]

    o_xbc = SSD_WIDTH
    o_dt = o_xbc + XBC_WIDTH
    o_q = o_dt + SSD_HEADS
    pad_h = lambda v: jnp.pad(v.astype(F32), (0, LANES - SSD_HEADS)).reshape(1, LANES)
    row = lambda v: v.astype(F32).reshape(1, -1)

    xcur = x.reshape(n_tok, D_MODEL)
    for l in range(depth):
        wl = w_in[l].astype(BF16)
        wq, wf, wi, wg = (wl[:, o_q + k * HGRN_WIDTH:o_q + (k + 1) * HGRN_WIDTH] for k in range(4))
        wdt = jnp.pad(wl[:, o_dt:o_q], ((0, 0), (0, LANES - SSD_HEADS)))
        proj_w = (wl[:, :o_xbc], wl[:, o_xbc:o_dt], wq, wf, wi, wg, wdt)
        if l == 0:
            z, xbc, q, f, i, g, dt = _inproj(xcur, n_tok, row(norm_mix_w[l]), *proj_w)
        else:
            xcur, (z, xbc, q, f, i, g, dt) = _inproj_pull(pos, ys, n_tok, row(norm_mix_w[l]), *proj_w)
        y_ssd = _ssd(z, xbc, dt, bsz, seq, conv_w[l].astype(F32), row(conv_b[l]), dt_bias[l],
                     a_log[l], row(jnp.repeat(d_skip[l], SSD_HEAD_DIM)), row(ssd_norm_w[l]))
        y_hgrn = _hgrn(q, f, i, g, bsz, seq, row(lower_bounds[l]), row(hgrn_norm_w[l]))
        wo = w_out[l].astype(BF16)
        wr_t = jnp.concatenate([router_group[l], router_expert[l]], axis=1).T
        wr_t = jnp.pad(wr_t, ((0, LANES - wr_t.shape[0]), (0, 0))).astype(BF16)
        xe, route, cnt = _outproj(y_ssd, y_hgrn, xcur, n_tok, wo[:SSD_WIDTH], wo[SSD_WIDTH:],
                                  row(norm_ffn_w[l]), wr_t)
        counts = cnt[:N_BUCKETS, 0].astype(jnp.int32)
        tab, pos = _dispatch_plan(route, counts, n_tok)
        n_slots = tab.shape[1] * TM_FFN
        xs = _push_rows(tab[4, :N_BUCKETS], tab[5, :N_BUCKETS], pos, xe, n_slots)
        ys = _ffn_sorted(xs, tab[0], tab[1], tab[3, :1], row(norm_ffn_w[l]),
                         row(final_norm_w), l, w_gate, w_up, w_down, final=(l == depth - 1))
    return _pull_rows(pos, ys).reshape(bsz, seq, D_MODEL)
```
